```python
import jax, jax.numpy as jnp
from jax import lax
import numpy as np

D_MODEL = 4096
BATCH = 1
SEQ = 8192
DEPTH = 1
DEC_BATCH = 128
DEC_SEQ = 8
PAST_LEN = 8192
PAGE_SIZE = 128

HEAD_DIM = 128
N_HEADS = D_MODEL // HEAD_DIM
N_KV_HEADS = N_HEADS // 4
KV_GROUP = N_HEADS // N_KV_HEADS
WINDOW = 128
SWA_BLOCK = 128
ROPE_THETA = 10000.0
GDN_DK = 128
GDN_DV = 128
GDN_V_HEADS = D_MODEL // GDN_DV
GDN_QK_HEADS = GDN_V_HEADS // 2
GDN_KEY_DIM = GDN_QK_HEADS * GDN_DK
GDN_VAL_DIM = GDN_V_HEADS * GDN_DV
GDN_CONV_DIM = 2 * GDN_KEY_DIM + GDN_VAL_DIM
CONV_W = 4
GDN_CHUNK = 64
D_FF = 11008
PLE_DIM = 256
EPS = 1e-6
IN_SPLITS = (GDN_CONV_DIM, GDN_VAL_DIM, GDN_V_HEADS, GDN_V_HEADS,
             N_HEADS * HEAD_DIM, N_KV_HEADS * HEAD_DIM, N_KV_HEADS * HEAD_DIM,
             D_MODEL, D_MODEL)
D_IN = sum(IN_SPLITS)

kernel_name = 'hybrid_gdn_swa_macaron_step'


def rmsnorm(x, w):
    xf = x.astype(jnp.float32)
    y = xf * lax.rsqrt(jnp.mean(xf * xf, axis=-1, keepdims=True) + EPS)
    return (y * w.astype(jnp.float32)).astype(x.dtype)


def l2norm(x):
    xf = x.astype(jnp.float32)
    return xf * lax.rsqrt(jnp.sum(xf * xf, axis=-1, keepdims=True) + EPS)


def swiglu(x, w_gate, w_up, w_down):
    return (jax.nn.silu(x @ w_gate) * (x @ w_up)) @ w_down


def rope(x, pos):
    half = HEAD_DIM // 2
    inv_freq = ROPE_THETA ** (-jnp.arange(half, dtype=jnp.float32) / half)
    ang = pos.astype(jnp.float32)[:, None] * inv_freq[None, :]
    cos = jnp.cos(ang)[None, :, None, :]
    sin = jnp.sin(ang)[None, :, None, :]
    xf = x.astype(jnp.float32)
    x1, x2 = xf[..., :half], xf[..., half:]
    return jnp.concatenate([x1 * cos - x2 * sin, x2 * cos + x1 * sin], axis=-1).astype(x.dtype)


def sink_attend(q, k, v, mask, sinks):
    s = jnp.einsum('...qkgd,...skd->...kgqs', q, k).astype(jnp.float32) * HEAD_DIM ** -0.5
    s = jnp.where(mask, s, -jnp.inf)
    sink = sinks.astype(jnp.float32).reshape(N_KV_HEADS, KV_GROUP, 1, 1)
    m = jnp.maximum(jnp.max(s, axis=-1, keepdims=True), sink)
    p = jnp.exp(s - m)
    p = p / (jnp.sum(p, axis=-1, keepdims=True) + jnp.exp(sink - m))
    return jnp.einsum('...kgqs,...skd->...qkgd', p.astype(v.dtype), v)


def swa_prompt(q, k, v, sinks):
    B, T = q.shape[:2]
    nb = T // SWA_BLOCK
    qb = q.reshape(B, nb, SWA_BLOCK, N_KV_HEADS, KV_GROUP, HEAD_DIM)

    def band(t):
        cur = t.reshape(B, nb, SWA_BLOCK, N_KV_HEADS, HEAD_DIM)
        prev = jnp.concatenate([jnp.zeros_like(cur[:, :1]), cur[:, :-1]], axis=1)
        return jnp.concatenate([prev, cur], axis=2)

    kb, vb = band(k), band(v)
    qi = jnp.arange(SWA_BLOCK) + SWA_BLOCK
    kj = jnp.arange(2 * SWA_BLOCK)
    dist = qi[:, None] - kj[None, :]
    in_win = (dist >= 0) & (dist < WINDOW)
    valid = (jnp.arange(nb)[:, None, None] > 0) | (kj[None, None, :] >= SWA_BLOCK)
    mask = (in_win[None] & valid)[:, None, None]
    o = sink_attend(qb, kb, vb, mask, sinks)
    return o.reshape(B, T, N_HEADS * HEAD_DIM)


def swa_sample(q, k, v, k_buf, v_buf, sinks):
    B, T = q.shape[:2]
    W = k_buf.shape[1]
    kk = jnp.concatenate([k_buf.astype(k.dtype), k], axis=1)
    vv = jnp.concatenate([v_buf.astype(v.dtype), v], axis=1)
    qpos = PAST_LEN + jnp.arange(T)
    kpos = PAST_LEN - W + jnp.arange(W + T)
    dist = qpos[:, None] - kpos[None, :]
    mask = (dist >= 0) & (dist < WINDOW)
    o = sink_attend(q.reshape(B, T, N_KV_HEADS, KV_GROUP, HEAD_DIM), kk, vv, mask, sinks)
    return o.reshape(B, T, N_HEADS * HEAD_DIM), kk[:, -W:], vv[:, -W:]


def gated_delta_chunked(q, k, v, g, beta, S0):
    B, T, H, _ = q.shape
    C = min(GDN_CHUNK, T)
    pad = (-T) % C
    N = (T + pad) // C

    def prep(t):
        t = jnp.pad(t, [(0, 0), (0, pad)] + [(0, 0)] * (t.ndim - 2))
        t = t.reshape((B, N, C) + t.shape[2:])
        return jnp.moveaxis(t, 3, 1)

    q, k, v, g, beta = (prep(t) for t in (q, k, v, g, beta))
    q = q * GDN_DK ** -0.5
    kb = k * beta[..., None]
    vb = v * beta[..., None]
    gc = jnp.cumsum(g, axis=-1)
    tril = jnp.tril(jnp.ones((C, C), bool))
    strict = jnp.tril(jnp.ones((C, C), bool), -1)
    decay = jnp.exp(jnp.where(tril, gc[..., :, None] - gc[..., None, :], -jnp.inf))
    L = jnp.where(strict, jnp.einsum('bhncd,bhnsd->bhncs', kb, k) * decay, 0.0)
    eye = jnp.eye(C, dtype=jnp.float32)
    Tinv = lax.linalg.triangular_solve(L + eye, jnp.broadcast_to(eye, L.shape),
                                       left_side=True, lower=True)
    u = Tinv @ vb
    w = Tinv @ (kb * jnp.exp(gc)[..., None])
    A = jnp.where(tril, jnp.einsum('bhncd,bhnsd->bhncs', q, k) * decay, 0.0)
    g_last = gc[..., -1]
    k_st = k * jnp.exp(g_last[..., None] - gc)[..., None]
    q_dec = q * jnp.exp(gc)[..., None]
    xs = tuple(jnp.moveaxis(t, 2, 0) for t in (u, w, A, q_dec, k_st, g_last))

    def step(S, inp):
        u_i, w_i, A_i, qd_i, ks_i, gl_i = inp
        v_new = u_i - w_i @ S
        o_i = qd_i @ S + A_i @ v_new
        S = S * jnp.exp(gl_i)[..., None, None] + jnp.einsum('bhck,bhcv->bhkv', ks_i, v_new)
        return S, o_i

    S, o = lax.scan(step, S0.astype(jnp.float32), xs)
    o = jnp.moveaxis(jnp.moveaxis(o, 0, 2), 1, 3)
    return o.reshape(B, N * C, H, GDN_DV)[:, :T], S


def gdn_mixer(qkv, z, a, b, conv_buf, S0, conv_w, A_log, dt_bias, norm_w):
    B, T, _ = qkv.shape
    xp = jnp.concatenate([conv_buf.astype(qkv.dtype), qkv], axis=1)
    y = xp[:, 0:T] * conv_w[0]
    for j in range(1, CONV_W):
        y = y + xp[:, j:j + T] * conv_w[j]
    y = jax.nn.silu(y)
    new_buf = xp[:, T:]
    q, k, v = jnp.split(y, [GDN_KEY_DIM, 2 * GDN_KEY_DIM], axis=-1)
    rep = GDN_V_HEADS // GDN_QK_HEADS
    q = jnp.repeat(l2norm(q.reshape(B, T, GDN_QK_HEADS, GDN_DK)), rep, axis=2)
    k = jnp.repeat(l2norm(k.reshape(B, T, GDN_QK_HEADS, GDN_DK)), rep, axis=2)
    v = v.reshape(B, T, GDN_V_HEADS, GDN_DV).astype(jnp.float32)
    beta = jax.nn.sigmoid(b.astype(jnp.float32))
    g = -jnp.exp(A_log.astype(jnp.float32)) * jax.nn.softplus(
        a.astype(jnp.float32) + dt_bias.astype(jnp.float32))
    o, S = gated_delta_chunked(q, k, v, g, beta, S0)
    o = rmsnorm(o, norm_w) * jax.nn.silu(z.reshape(B, T, GDN_V_HEADS, GDN_DV).astype(jnp.float32))
    return o.reshape(B, T, GDN_VAL_DIM).astype(qkv.dtype), new_buf, S


def decoder_layer(x, pe, pos, conv_buf, S0, k_buf, v_buf, w_buf,
                  ffn1_norm, ffn1_w_gate, ffn1_w_up, ffn1_w_down,
                  mix_norm, w_in, conv_w, A_log, dt_bias, gdn_norm_w,
                  q_norm_w, k_norm_w, sinks, w_out,
                  ffn2_norm, ffn2_w_gate, ffn2_w_up, ffn2_w_down,
                  ple_norm, w_ple_proj, w_ple_gate):
    B, T = x.shape[:2]
    x = x + 0.5 * swiglu(rmsnorm(x, ffn1_norm), ffn1_w_gate, ffn1_w_up, ffn1_w_down)
    h = rmsnorm(x, mix_norm)
    idx = [int(i) for i in np.cumsum(IN_SPLITS)[:-1]]
    g_qkv, g_z, g_a, g_b, s_q, s_k, s_v, gate_a, gate_b = jnp.split(h @ w_in, idx, axis=-1)
    o_a, conv_new, S_new = gdn_mixer(g_qkv, g_z, g_a, g_b, conv_buf, S0,
                                     conv_w, A_log, dt_bias, gdn_norm_w)
    s_q = rope(rmsnorm(s_q.reshape(B, T, N_HEADS, HEAD_DIM), q_norm_w), pos)
    s_k = rope(rmsnorm(s_k.reshape(B, T, N_KV_HEADS, HEAD_DIM), k_norm_w), pos)
    s_v = s_v.reshape(B, T, N_KV_HEADS, HEAD_DIM)
    if k_buf is None:
        o_b = swa_prompt(s_q, s_k, s_v, sinks)
        k_new, v_new = s_k[:, -w_buf:], s_v[:, -w_buf:]
    else:
        o_b, k_new, v_new = swa_sample(s_q, s_k, s_v, k_buf, v_buf, sinks)
    merged = jax.nn.sigmoid(gate_a) * o_a + jax.nn.sigmoid(gate_b) * o_b
    x = x + merged @ w_out
    x = x + 0.5 * swiglu(rmsnorm(x, ffn2_norm), ffn2_w_gate, ffn2_w_up, ffn2_w_down)
    x = x + (pe @ w_ple_proj) * jax.nn.sigmoid(rmsnorm(x, ple_norm) @ w_ple_gate)
    return x, S_new, conv_new, k_new, v_new


def setup_inputs(seed: int = 0) -> dict:
    key = jax.random.key(seed)
    ks = iter(jax.random.split(key, 40))

    def nrm(shape, scale):
        return jax.random.normal(next(ks), shape, jnp.float32) * scale

    def gain(shape):
        return 1.0 + 0.02 * jax.random.normal(next(ks), shape, jnp.float32)

    wbuf = min(WINDOW, PAST_LEN)
    L = DEPTH
    dt = jnp.exp(jax.random.uniform(next(ks), (L, GDN_V_HEADS), jnp.float32,
                                    minval=np.log(1e-3), maxval=np.log(1e-1)))
    return {
        'x_prompt': nrm((BATCH, SEQ, D_MODEL), 1.0),
        'x_sample': nrm((DEC_BATCH, DEC_SEQ, D_MODEL), 1.0),
        'p_prompt': nrm((DEPTH, BATCH, SEQ, PLE_DIM), 1.0),
        'p_sample': nrm((DEPTH, DEC_BATCH, DEC_SEQ, PLE_DIM), 1.0),
        'state_gdn': nrm((DEPTH, DEC_BATCH, GDN_V_HEADS, GDN_DK, GDN_DV), 0.5),
        'state_conv': nrm((DEPTH, DEC_BATCH, CONV_W - 1, GDN_CONV_DIM), 1.0),
        'cache_swa_k': nrm((DEPTH, DEC_BATCH, wbuf, N_KV_HEADS, HEAD_DIM), 1.0),
        'cache_swa_v': nrm((DEPTH, DEC_BATCH, wbuf, N_KV_HEADS, HEAD_DIM), 1.0),
        'ffn1_norm': gain((L, D_MODEL)),
        'ffn1_w_gate': nrm((L, D_MODEL, D_FF), D_MODEL ** -0.5),
        'ffn1_w_up': nrm((L, D_MODEL, D_FF), D_MODEL ** -0.5),
        'ffn1_w_down': nrm((L, D_FF, D_MODEL), D_FF ** -0.5),
        'mix_norm': gain((L, D_MODEL)),
        'w_in': nrm((L, D_MODEL, D_IN), D_MODEL ** -0.5),
        'conv_w': nrm((L, CONV_W, GDN_CONV_DIM), CONV_W ** -0.5),
        'A_log': jnp.log(jax.random.uniform(next(ks), (L, GDN_V_HEADS), jnp.float32,
                                            minval=1.0, maxval=16.0)),
        'dt_bias': dt + jnp.log(-jnp.expm1(-dt)),
        'gdn_norm_w': gain((L, GDN_DV)),
        'q_norm_w': gain((L, HEAD_DIM)),
        'k_norm_w': gain((L, HEAD_DIM)),
        'sinks': nrm((L, N_HEADS), 1.0),
        'w_out': nrm((L, D_MODEL, D_MODEL), D_MODEL ** -0.5),
        'ffn2_norm': gain((L, D_MODEL)),
        'ffn2_w_gate': nrm((L, D_MODEL, D_FF), D_MODEL ** -0.5),
        'ffn2_w_up': nrm((L, D_MODEL, D_FF), D_MODEL ** -0.5),
        'ffn2_w_down': nrm((L, D_FF, D_MODEL), D_FF ** -0.5),
        'ple_norm': gain((L, D_MODEL)),
        'w_ple_proj': nrm((L, PLE_DIM, D_MODEL), PLE_DIM ** -0.5),
        'w_ple_gate': nrm((L, D_MODEL, D_MODEL), D_MODEL ** -0.5),
    }


def reference(x_prompt, x_sample, p_prompt, p_sample, state_gdn, state_conv,
              cache_swa_k, cache_swa_v,
              ffn1_norm, ffn1_w_gate, ffn1_w_up, ffn1_w_down,
              mix_norm, w_in, conv_w, A_log, dt_bias, gdn_norm_w,
              q_norm_w, k_norm_w, sinks, w_out,
              ffn2_norm, ffn2_w_gate, ffn2_w_up, ffn2_w_down,
              ple_norm, w_ple_proj, w_ple_gate):
    w_buf = cache_swa_k.shape[2]
    Bp, Tp = x_prompt.shape[:2]
    pos_prompt = jnp.arange(Tp, dtype=jnp.int32)
    pos_sample = PAST_LEN + jnp.arange(x_sample.shape[1], dtype=jnp.int32)
    conv0 = jnp.zeros((Bp, CONV_W - 1, GDN_CONV_DIM), x_prompt.dtype)
    S_zero = jnp.zeros((Bp, GDN_V_HEADS, GDN_DK, GDN_DV), jnp.float32)
    yp, ys = x_prompt, x_sample
    sg_p, cv_p, kk_p, vv_p = [], [], [], []
    sg_s, cv_s, kk_s, vv_s = [], [], [], []
    for l in range(DEPTH):
        lw = (ffn1_norm[l], ffn1_w_gate[l], ffn1_w_up[l], ffn1_w_down[l],
              mix_norm[l], w_in[l], conv_w[l], A_log[l], dt_bias[l], gdn_norm_w[l],
              q_norm_w[l], k_norm_w[l], sinks[l], w_out[l],
              ffn2_norm[l], ffn2_w_gate[l], ffn2_w_up[l], ffn2_w_down[l],
              ple_norm[l], w_ple_proj[l], w_ple_gate[l])
        yp, s1, c1, k1, v1 = decoder_layer(yp, p_prompt[l], pos_prompt, conv0, S_zero,
                                           None, None, w_buf, *lw)
        ys, s2, c2, k2, v2 = decoder_layer(ys, p_sample[l], pos_sample, state_conv[l],
                                           state_gdn[l], cache_swa_k[l], cache_swa_v[l],
                                           w_buf, *lw)
        sg_p.append(s1); cv_p.append(c1); kk_p.append(k1); vv_p.append(v1)
        sg_s.append(s2); cv_s.append(c2); kk_s.append(k2); vv_s.append(v2)
    return (yp, ys,
            jnp.stack(sg_p), jnp.stack(cv_p), jnp.stack(kk_p), jnp.stack(vv_p),
            jnp.stack(sg_s), jnp.stack(cv_s), jnp.stack(kk_s), jnp.stack(vv_s))
```

```python
import functools

import jax
import jax.numpy as jnp
from jax import lax
from jax.experimental import pallas as pl
from jax.experimental.pallas import tpu as pltpu

F32 = jnp.float32
BF16 = jnp.bfloat16

EPS = 1e-6
HEAD_DIM = 128
KV_GROUP = 4
WINDOW = 128
ROPE_THETA = 10000.0
PAST_LEN = 8192
GDN_REP = 2
CONV_W = 4
CHUNK = 64
PLE_DIM = 256

V7X_VMEM_LIMIT_BYTES = 56 * 1024 * 1024


def _cparams(*sem):
    return pltpu.CompilerParams(dimension_semantics=sem, vmem_limit_bytes=V7X_VMEM_LIMIT_BYTES)


def _rms(x, w):
    return x * lax.rsqrt(jnp.mean(x * x, axis=-1, keepdims=True) + EPS) * w


def _dot(a, b):
    return jnp.dot(a.astype(BF16), b.astype(BF16), preferred_element_type=F32)


def _dot_nt(a, b):
    return lax.dot_general(a.astype(BF16), b.astype(BF16), (((1,), (1,)), ((), ())),
                           preferred_element_type=F32)


def _dot_tn(a, b):
    return lax.dot_general(a.astype(BF16), b.astype(BF16), (((0,), (0,)), ((), ())),
                           preferred_element_type=F32)


def _ffn_kernel(x_ref, nw_ref, wg_ref, wu_ref, wd_ref, o_ref, xn_ref):
    @pl.when(pl.program_id(1) == 0)
    def _():
        x = x_ref[...]
        xn_ref[...] = _rms(x, nw_ref[...]).astype(BF16)
        o_ref[...] = x

    xn = xn_ref[...]
    g = jnp.dot(xn, wg_ref[...], preferred_element_type=F32)
    u = jnp.dot(xn, wu_ref[...], preferred_element_type=F32)
    h = (0.5 * (g * jax.nn.sigmoid(g))) * u
    o_ref[...] += jnp.dot(h.astype(BF16), wd_ref[...], preferred_element_type=F32)


def _ffn(x, norm_w, wg, wu, wd, *, tm=512, tf=256):
    m, d = x.shape
    f = wg.shape[1]
    return pl.pallas_call(
        _ffn_kernel,
        grid=(m // tm, f // tf),
        in_specs=[
            pl.BlockSpec((tm, d), lambda i, j: (i, 0), pipeline_mode=pl.Buffered(1)),
            pl.BlockSpec((1, d), lambda i, j: (0, 0)),
            pl.BlockSpec((d, tf), lambda i, j: (0, j)),
            pl.BlockSpec((d, tf), lambda i, j: (0, j)),
            pl.BlockSpec((tf, d), lambda i, j: (j, 0)),
        ],
        out_specs=pl.BlockSpec((tm, d), lambda i, j: (i, 0)),
        out_shape=jax.ShapeDtypeStruct((m, d), F32),
        scratch_shapes=[pltpu.VMEM((tm, d), BF16)],
        compiler_params=_cparams("parallel", "arbitrary"),
        name="ffn",
    )(x, norm_w.reshape(1, d), wg, wu, wd)


def _inproj_kernel(x_ref, nw_ref, w_ref, wab_ref, o_ref, ab_ref, xn_ref):
    @pl.when(pl.program_id(1) == 0)
    def _():
        xn = _rms(x_ref[...], nw_ref[...]).astype(BF16)
        xn_ref[...] = xn
        ab_ref[...] = jnp.dot(xn, wab_ref[...], preferred_element_type=F32)

    o_ref[...] = jnp.dot(xn_ref[...], w_ref[...], preferred_element_type=F32)


def _inproj(x, norm_w, w_main, w_ab, *, tm=512, tn=1024):
    m, d = x.shape
    n = w_main.shape[1]
    nab = w_ab.shape[1]
    return pl.pallas_call(
        _inproj_kernel,
        grid=(m // tm, n // tn),
        in_specs=[
            pl.BlockSpec((tm, d), lambda i, j: (i, 0), pipeline_mode=pl.Buffered(1)),
            pl.BlockSpec((1, d), lambda i, j: (0, 0)),
            pl.BlockSpec((d, tn), lambda i, j: (0, j)),
            pl.BlockSpec((d, nab), lambda i, j: (0, 0)),
        ],
        out_specs=[
            pl.BlockSpec((tm, tn), lambda i, j: (i, j)),
            pl.BlockSpec((tm, nab), lambda i, j: (i, 0)),
        ],
        out_shape=[jax.ShapeDtypeStruct((m, n), F32), jax.ShapeDtypeStruct((m, nab), F32)],
        scratch_shapes=[pltpu.VMEM((tm, d), BF16)],
        compiler_params=_cparams("parallel", "arbitrary"),
        name="inproj",
    )(x, norm_w.reshape(1, d), w_main, w_ab)


def _conv_finish(y, j, o_ref, *, n_q_blocks, n_qk_blocks, q_scale):
    y = y * jax.nn.sigmoid(y)
    cw = y.shape[1]
    is_qk = j < n_qk_blocks
    scale = jnp.where(j < n_q_blocks, q_scale, 1.0)
    for h in range(cw // HEAD_DIM):
        yh = y[:, h * HEAD_DIM:(h + 1) * HEAD_DIM]
        nrm = lax.rsqrt(jnp.sum(yh * yh, axis=-1, keepdims=True) + EPS) * scale
        o_ref[:, h * HEAD_DIM:(h + 1) * HEAD_DIM] = yh * jnp.where(is_qk, nrm, 1.0)


def _conv_prompt_kernel(x_ref, halo_ref, cw_ref, o_ref, **kw):
    i = pl.program_id(0)
    j = pl.program_id(1)
    x = x_ref[...]
    halo = jnp.where(i > 0, halo_ref[...], 0.0)
    w = cw_ref[...]
    row8 = lax.broadcasted_iota(jnp.int32, (8, 1), 0)
    y = None
    for tap in range(CONV_W):
        sh = CONV_W - 1 - tap
        if sh == 0:
            xs = x
        else:
            rolled = pltpu.roll(x, sh, 0)
            head = jnp.where(row8 < sh, pltpu.roll(halo, sh, 0), rolled[:8])
            xs = jnp.concatenate([head, rolled[8:]], axis=0)
        term = xs * w[tap:tap + 1]
        y = term if y is None else y + term
    _conv_finish(y, j, o_ref, **kw)


def _conv_sample_kernel(x_ref, e_ref, cw_ref, o_ref, *, seq, **kw):
    j = pl.program_id(1)
    x = x_ref[...]
    e = e_ref[...]
    w = cw_ref[...]
    rows = x.shape[0]
    t = lax.broadcasted_iota(jnp.int32, (rows, 1), 0) % seq
    y = None
    for tap in range(CONV_W):
        sh = CONV_W - 1 - tap
        if sh == 0:
            xs = x
        else:
            xs = jnp.where(t >= sh, pltpu.roll(x, sh, 0), pltpu.roll(e, rows + sh - seq, 0))
        term = xs * w[tap:tap + 1]
        y = term if y is None else y + term
    _conv_finish(y, j, o_ref, **kw)


def _conv_common(n_q_cols, n_qk_cols, cw):
    return dict(n_q_blocks=n_q_cols // cw, n_qk_blocks=n_qk_cols // cw, q_scale=HEAD_DIM ** -0.5)


def _conv_prompt(proj, conv_w, *, rows, conv_dim, key_dim, tc=512, cw=512):
    tc8 = tc // 8
    return pl.pallas_call(
        functools.partial(_conv_prompt_kernel, **_conv_common(key_dim, 2 * key_dim, cw)),
        grid=(rows // tc, conv_dim // cw),
        in_specs=[
            pl.BlockSpec((tc, cw), lambda i, j: (i, j)),
            pl.BlockSpec((8, cw), lambda i, j: (jnp.maximum(i * tc8 - 1, 0), j)),
            pl.BlockSpec((CONV_W, cw), lambda i, j: (0, j)),
        ],
        out_specs=pl.BlockSpec((tc, cw), lambda i, j: (i, j)),
        out_shape=jax.ShapeDtypeStruct((rows, conv_dim), F32),
        compiler_params=_cparams("parallel", "parallel"),
        name="conv_prompt",
    )(proj, proj, conv_w)


def _conv_sample(proj, e, conv_w, *, row0, rows, seq, conv_dim, key_dim, tc=512, cw=512):
    blk0 = row0 // tc
    return pl.pallas_call(
        functools.partial(_conv_sample_kernel, seq=seq, **_conv_common(key_dim, 2 * key_dim, cw)),
        grid=(rows // tc, conv_dim // cw),
        in_specs=[
            pl.BlockSpec((tc, cw), lambda i, j: (blk0 + i, j)),
            pl.BlockSpec((tc, cw), lambda i, j: (i, j)),
            pl.BlockSpec((CONV_W, cw), lambda i, j: (0, j)),
        ],
        out_specs=pl.BlockSpec((tc, cw), lambda i, j: (i, j)),
        out_shape=jax.ShapeDtypeStruct((rows, conv_dim), F32),
        compiler_params=_cparams("parallel", "parallel"),
        name="conv_sample",
    )(proj, e, conv_w)


def _softplus(x):
    return jnp.maximum(x, 0.0) + jnp.log1p(jnp.exp(-jnp.abs(x)))


def _seg_cumsum(x, pos, seg, axis):
    d = 1
    while d < seg:
        x = x + jnp.where(pos >= d, pltpu.roll(x, d, axis), 0.0)
        d *= 2
    return x


def _gates(ab, abt8, alog_row, dtb_row, alog_col8, dtb_col8, seg):
    rows = ab.shape[0]
    pos_c = lax.broadcasted_iota(jnp.int32, (rows, 1), 0) % seg
    pos_r = lax.broadcasted_iota(jnp.int32, (1, rows), 1) % seg
    g_all = -jnp.exp(alog_row) * _softplus(ab + dtb_row)
    g_t = -jnp.exp(alog_col8) * _softplus(abt8 + dtb_col8)
    return _seg_cumsum(g_all, pos_c, seg, 0), jax.nn.sigmoid(ab), _seg_cumsum(g_t, pos_r, seg, 1)


def _pick_lane(x, idx):
    lane = lax.broadcasted_iota(jnp.int32, (1, x.shape[1]), 1)
    return jnp.sum(jnp.where(lane == idx, x, 0.0), axis=1, keepdims=True)


def _pick_sublane(x, idx):
    sub = lax.broadcasted_iota(jnp.int32, (x.shape[0], 1), 0)
    return jnp.sum(jnp.where(sub == idx, x, 0.0), axis=0, keepdims=True)


def _head_gates(gate_vals, h, n_heads):
    gc_all, sig_all, gc_t = gate_vals
    return _pick_lane(sig_all, n_heads + h), _pick_lane(gc_all, h), _pick_sublane(gc_t, h % 8)


def _load_gates(ab_ref, abt_ref, alr_ref, dtr_ref, alc_ref, dtc_ref, h0, seg):
    h8 = pl.multiple_of((h0 // 8) * 8, 8)
    return _gates(ab_ref[...], abt_ref[pl.ds(h8, 8), :], alr_ref[...], dtr_ref[...],
                  alc_ref[pl.ds(h8, 8), :], dtc_ref[pl.ds(h8, 8), :], seg)


def _chunk_masks(seg):
    r = lax.broadcasted_iota(jnp.int32, (CHUNK, CHUNK), 0)
    c = lax.broadcasted_iota(jnp.int32, (CHUNK, CHUNK), 1)
    same = (r // seg) == (c // seg)
    return same & (c <= r), same & (c < r), r == c


def _unit_lower_inverse(low, eye, seg):
    p = -low
    t = jnp.where(eye, 1.0, p)
    n = 2
    while n < seg:
        p = _dot(p, p)
        t = t + _dot(t, p)
        n *= 2
    return t


def _chunk_intra(kk, qk, k, v, beta, gc_col, gc_row, masks, seg):
    incl, strict, eye = masks
    decay = jnp.exp(jnp.where(incl, gc_col - gc_row, -jnp.inf))
    low = jnp.where(strict, beta * kk * decay, 0.0)
    tinv = _unit_lower_inverse(low, eye, seg)
    rhs = jnp.concatenate([v * beta, k * (beta * jnp.exp(gc_col))], axis=1)
    uw = _dot(tinv, rhs)
    a = jnp.where(incl, qk * decay, 0.0)
    return uw[:, :HEAD_DIM], uw[:, HEAD_DIM:], a


def _gdn_out(o, z, nw):
    return _rms(o, nw) * (z * jax.nn.sigmoid(z))


def _gdn_prompt_kernel(q_ref, k_ref, v_ref, z_ref, ab_ref, abt_ref, alr_ref, dtr_ref, alc_ref, dtc_ref,
                       nw_ref, o_ref, sfin_ref, s_ref, *, n_heads):
    hq = pl.program_id(0)
    i = pl.program_id(1)

    @pl.when(i == 0)
    def _():
        s_ref[...] = jnp.zeros_like(s_ref)

    rows = q_ref.shape[0]
    masks = _chunk_masks(CHUNK)
    nw = nw_ref[...]
    gate_vals = _load_gates(ab_ref, abt_ref, alr_ref, dtr_ref, alc_ref, dtc_ref, hq * GDN_REP, CHUNK)
    gates = [_head_gates(gate_vals, hq * GDN_REP + hh, n_heads) for hh in range(GDN_REP)]
    states = [s_ref[hh] for hh in range(GDN_REP)]
    for n in range(rows // CHUNK):
        r0, r1 = n * CHUNK, (n + 1) * CHUNK
        q = q_ref[r0:r1, :]
        k = k_ref[r0:r1, :]
        kk = _dot_nt(k, k)
        qk = _dot_nt(q, k)
        for hh in range(GDN_REP):
            c0, c1 = hh * HEAD_DIM, (hh + 1) * HEAD_DIM
            beta, gc_col, gc_row = gates[hh]
            beta, gc_col, gc_row = beta[r0:r1], gc_col[r0:r1], gc_row[:, r0:r1]
            u, w, a = _chunk_intra(kk, qk, k, v_ref[r0:r1, c0:c1], beta, gc_col, gc_row, masks, CHUNK)
            s = states[hh]
            v_new = u - _dot(w, s)
            o = _dot(q * jnp.exp(gc_col), s) + _dot(a, v_new)
            g_last = gc_col[CHUNK - 1:CHUNK]
            states[hh] = s * jnp.exp(g_last) + _dot_tn(k * jnp.exp(g_last - gc_col), v_new)
            o_ref[r0:r1, c0:c1] = _gdn_out(o, z_ref[r0:r1, c0:c1], nw)
    for hh in range(GDN_REP):
        s_ref[hh] = states[hh]

    @pl.when(i == pl.num_programs(1) - 1)
    def _():
        sfin_ref[...] = s_ref[...]


def _gate_param_specs(n_lanes, idx):
    row = pl.BlockSpec((1, n_lanes), idx)
    col = pl.BlockSpec((n_lanes, 1), idx)
    return [row, row, col, col]


def _gate_params(alog, dtb, n_lanes):
    pad = lambda v: jnp.pad(v.astype(F32), (0, n_lanes - v.shape[0]))
    al, dt = pad(alog), pad(dtb)
    return al.reshape(1, n_lanes), dt.reshape(1, n_lanes), al.reshape(n_lanes, 1), dt.reshape(n_lanes, 1)


def _gdn_prompt(act, proj, ab, ab_t, alog, dtb, norm_w, *, rows, n_qk_heads, z_col0, tc=256):
    n_heads = n_qk_heads * GDN_REP
    vw = GDN_REP * HEAD_DIM
    v_blk0 = 2 * n_qk_heads * HEAD_DIM // vw
    z_blk0 = z_col0 // vw
    n_lanes = ab.shape[1]
    return pl.pallas_call(
        functools.partial(_gdn_prompt_kernel, n_heads=n_heads),
        grid=(n_qk_heads, rows // tc),
        in_specs=[
            pl.BlockSpec((tc, HEAD_DIM), lambda h, i: (i, h)),
            pl.BlockSpec((tc, HEAD_DIM), lambda h, i: (i, n_qk_heads + h)),
            pl.BlockSpec((tc, vw), lambda h, i: (i, v_blk0 + h)),
            pl.BlockSpec((tc, vw), lambda h, i: (i, z_blk0 + h)),
            pl.BlockSpec((tc, n_lanes), lambda h, i: (i, 0)),
            pl.BlockSpec((n_lanes, tc), lambda h, i: (0, i)),
            *_gate_param_specs(n_lanes, lambda h, i: (0, 0)),
            pl.BlockSpec((1, HEAD_DIM), lambda h, i: (0, 0)),
        ],
        out_specs=[
            pl.BlockSpec((tc, vw), lambda h, i: (i, h)),
            pl.BlockSpec((GDN_REP, HEAD_DIM, HEAD_DIM), lambda h, i: (h, 0, 0)),
        ],
        out_shape=[jax.ShapeDtypeStruct((rows, n_heads * HEAD_DIM), F32),
                   jax.ShapeDtypeStruct((n_heads, HEAD_DIM, HEAD_DIM), F32)],
        scratch_shapes=[pltpu.VMEM((GDN_REP, HEAD_DIM, HEAD_DIM), F32)],
        compiler_params=_cparams("parallel", "arbitrary"),
        name="gdn_prompt",
    )(act, act, act, proj, ab, ab_t, *_gate_params(alog, dtb, n_lanes), norm_w.reshape(1, HEAD_DIM))


def _gdn_sample_kernel(q_ref, k_ref, v_ref, z_ref, ab_ref, abt_ref, alr_ref, dtr_ref, alc_ref, dtc_ref,
                       nw_ref, s0_ref, o_ref, s1_ref, *, n_heads, seq):
    hq = pl.program_id(1)
    rows = q_ref.shape[0]
    masks = _chunk_masks(seq)
    nw = nw_ref[...]
    gate_vals = _load_gates(ab_ref, abt_ref, alr_ref, dtr_ref, alc_ref, dtc_ref, hq * GDN_REP, seq)
    gates = [_head_gates(gate_vals, hq * GDN_REP + hh, n_heads) for hh in range(GDN_REP)]
    per_chunk = CHUNK // seq
    for n in range(rows // CHUNK):
        n0, n1 = n * CHUNK, (n + 1) * CHUNK
        q = q_ref[n0:n1, :]
        k = k_ref[n0:n1, :]
        kk = _dot_nt(k, k)
        qk = _dot_nt(q, k)
        for hh in range(GDN_REP):
            c0, c1 = hh * HEAD_DIM, (hh + 1) * HEAD_DIM
            beta, gc_col, gc_row = gates[hh]
            beta, gc_col, gc_row = beta[n0:n1], gc_col[n0:n1], gc_row[:, n0:n1]
            u, w, a = _chunk_intra(kk, qk, k, v_ref[n0:n1, c0:c1], beta, gc_col, gc_row, masks, seq)
            qd = q * jnp.exp(gc_col)
            v_new, qs = [], []
            for b in range(per_chunk):
                r0, r1 = b * seq, (b + 1) * seq
                s = s0_ref[n * per_chunk + b, hh]
                ws_qs = _dot(jnp.concatenate([w[r0:r1], qd[r0:r1]], axis=0), s)
                vn = u[r0:r1] - ws_qs[:seq]
                g_last = gc_col[r1 - 1:r1]
                k_st = k[r0:r1] * jnp.exp(g_last - gc_col[r0:r1])
                s1_ref[n * per_chunk + b, hh] = s * jnp.exp(g_last) + _dot_tn(k_st, vn)
                v_new.append(vn)
                qs.append(ws_qs[seq:])
            o = jnp.concatenate(qs, axis=0) + _dot(a, jnp.concatenate(v_new, axis=0))
            o_ref[n0:n1, c0:c1] = _gdn_out(o, z_ref[n0:n1, c0:c1], nw)


def _gdn_sample(act, proj, ab, ab_t, alog, dtb, norm_w, s0, *, row0, rows, seq, n_qk_heads, z_col0, tc=128):
    n_heads = n_qk_heads * GDN_REP
    vw = GDN_REP * HEAD_DIM
    v_blk0 = 2 * n_qk_heads * HEAD_DIM // vw
    z_blk0 = z_col0 // vw
    blk0 = row0 // tc
    nb = tc // seq
    n_lanes = ab.shape[1]
    state_spec = pl.BlockSpec((nb, GDN_REP, HEAD_DIM, HEAD_DIM), lambda i, h: (i, h, 0, 0))
    return pl.pallas_call(
        functools.partial(_gdn_sample_kernel, n_heads=n_heads, seq=seq),
        grid=(rows // tc, n_qk_heads),
        in_specs=[
            pl.BlockSpec((tc, HEAD_DIM), lambda i, h: (i, h)),
            pl.BlockSpec((tc, HEAD_DIM), lambda i, h: (i, n_qk_heads + h)),
            pl.BlockSpec((tc, vw), lambda i, h: (i, v_blk0 + h)),
            pl.BlockSpec((tc, vw), lambda i, h: (blk0 + i, z_blk0 + h)),
            pl.BlockSpec((tc, n_lanes), lambda i, h: (blk0 + i, 0)),
            pl.BlockSpec((n_lanes, tc), lambda i, h: (0, blk0 + i)),
            *_gate_param_specs(n_lanes, lambda i, h: (0, 0)),
            pl.BlockSpec((1, HEAD_DIM), lambda i, h: (0, 0)),
            state_spec,
        ],
        out_specs=[pl.BlockSpec((tc, vw), lambda i, h: (i, h)), state_spec],
        out_shape=[jax.ShapeDtypeStruct((rows, n_heads * HEAD_DIM), F32),
                   jax.ShapeDtypeStruct(s0.shape, F32)],
        compiler_params=_cparams("parallel", "parallel"),
        name="gdn_sample",
    )(act, act, act, proj, ab, ab_t, *_gate_params(alog, dtb, n_lanes), norm_w.reshape(1, HEAD_DIM), s0)


def _rope_kernel(sq_ref, sk_ref, cos_ref, sin_ref, qw_ref, kw_ref, qo_ref, ko_ref):
    cos = cos_ref[...]
    sin = sin_ref[...]
    for src, w_ref, dst in ((sq_ref, qw_ref, qo_ref), (sk_ref, kw_ref, ko_ref)):
        w = w_ref[...]
        for h in range(src.shape[1] // HEAD_DIM):
            c0, c1 = h * HEAD_DIM, (h + 1) * HEAD_DIM
            y = _rms(src[:, c0:c1], w)
            dst[:, c0:c1] = y * cos + pltpu.roll(y, HEAD_DIM // 2, 1) * sin


def _rope(proj, cos, sin, q_norm_w, k_norm_w, *, q_col0, q_cols, k_col0, k_cols, tr=256):
    m = proj.shape[0]
    return pl.pallas_call(
        _rope_kernel,
        grid=(m // tr,),
        in_specs=[
            pl.BlockSpec((tr, q_cols), lambda i: (i, q_col0 // q_cols)),
            pl.BlockSpec((tr, k_cols), lambda i: (i, k_col0 // k_cols)),
            pl.BlockSpec((tr, HEAD_DIM), lambda i: (i, 0)),
            pl.BlockSpec((tr, HEAD_DIM), lambda i: (i, 0)),
            pl.BlockSpec((1, HEAD_DIM), lambda i: (0, 0)),
            pl.BlockSpec((1, HEAD_DIM), lambda i: (0, 0)),
        ],
        out_specs=[
            pl.BlockSpec((tr, q_cols), lambda i: (i, 0)),
            pl.BlockSpec((tr, k_cols), lambda i: (i, 0)),
        ],
        out_shape=[jax.ShapeDtypeStruct((m, q_cols), F32), jax.ShapeDtypeStruct((m, k_cols), F32)],
        compiler_params=_cparams("parallel"),
        name="rope",
    )(proj, proj, cos, sin, q_norm_w.reshape(1, HEAD_DIM), k_norm_w.reshape(1, HEAD_DIM))


def _sink_softmax_pv(scores, values, sink):
    m = sink
    for s in scores:
        m = jnp.maximum(m, jnp.max(s, axis=-1, keepdims=True))
    den = jnp.exp(sink - m)
    acc = None
    for s, v in zip(scores, values):
        p = jnp.exp(s - m)
        den = den + jnp.sum(p, axis=-1, keepdims=True)
        pv = _dot(p, v)
        acc = pv if acc is None else acc + pv
    return acc / den


def _swa_prompt_kernel(sinks_ref, q_ref, kc_ref, kp_ref, vc_ref, vp_ref, o_ref):
    g = pl.program_id(0)
    i = pl.program_id(1)
    kc, kp, vc, vp = kc_ref[...], kp_ref[...], vc_ref[...], vp_ref[...]
    r = lax.broadcasted_iota(jnp.int32, (WINDOW, WINDOW), 0)
    c = lax.broadcasted_iota(jnp.int32, (WINDOW, WINDOW), 1)
    cur_ok = c <= r
    prev_ok = (c > r) & (i > 0)
    scale = HEAD_DIM ** -0.5
    for hh in range(KV_GROUP):
        c0, c1 = hh * HEAD_DIM, (hh + 1) * HEAD_DIM
        q = q_ref[:, c0:c1]
        s_cur = jnp.where(cur_ok, _dot_nt(q, kc) * scale, -jnp.inf)
        s_prev = jnp.where(prev_ok, _dot_nt(q, kp) * scale, -jnp.inf)
        sink = jnp.full((WINDOW, 1), sinks_ref[g * KV_GROUP + hh], F32)
        o_ref[:, c0:c1] = _sink_softmax_pv([s_prev, s_cur], [vp, vc], sink)


def _swa_prompt(qh, kh, proj, sinks, *, rows, n_kv_heads, v_col0):
    qw = KV_GROUP * HEAD_DIM
    v_blk0 = v_col0 // HEAD_DIM
    prev = lambda g, i: jnp.maximum(i - 1, 0)
    return pl.pallas_call(
        _swa_prompt_kernel,
        grid=(n_kv_heads, rows // WINDOW),
        in_specs=[
            pl.BlockSpec(memory_space=pltpu.SMEM),
            pl.BlockSpec((WINDOW, qw), lambda g, i: (i, g)),
            pl.BlockSpec((WINDOW, HEAD_DIM), lambda g, i: (i, g)),
            pl.BlockSpec((WINDOW, HEAD_DIM), lambda g, i: (prev(g, i), g)),
            pl.BlockSpec((WINDOW, HEAD_DIM), lambda g, i: (i, v_blk0 + g)),
            pl.BlockSpec((WINDOW, HEAD_DIM), lambda g, i: (prev(g, i), v_blk0 + g)),
        ],
        out_specs=pl.BlockSpec((WINDOW, qw), lambda g, i: (i, g)),
        out_shape=jax.ShapeDtypeStruct((rows, n_kv_heads * qw), F32),
        compiler_params=_cparams("parallel", "parallel"),
        name="swa_prompt",
    )(sinks, qh, kh, kh, proj, proj)


def _swa_sample_kernel(sinks_ref, q_ref, kn_ref, vn_ref, ck_ref, cv_ref, o_ref, ok_ref, ov_ref, *, seq):
    n_b, w_buf, kv_cols = ck_ref.shape
    rows = KV_GROUP * seq
    t = lax.broadcasted_iota(jnp.int32, (rows, 1), 0) % seq
    head = lax.broadcasted_iota(jnp.int32, (rows, 1), 0) // seq
    jc = lax.broadcasted_iota(jnp.int32, (1, w_buf), 1)
    jn = lax.broadcasted_iota(jnp.int32, (1, seq), 1)
    dist_c = t + w_buf - jc
    cache_ok = (dist_c >= 0) & (dist_c < WINDOW)
    new_ok = jn <= t
    scale = HEAD_DIM ** -0.5

    def per_sequence(b, carry):
        r0 = pl.multiple_of(b * seq, seq)
        qb = q_ref[pl.ds(r0, seq), :]
        knb = kn_ref[pl.ds(r0, seq), :]
        vnb = vn_ref[pl.ds(r0, seq), :]
        for g in range(kv_cols // HEAD_DIM):
            c0, c1 = g * HEAD_DIM, (g + 1) * HEAD_DIM
            q4 = jnp.concatenate(
                [qb[:, (g * KV_GROUP + hh) * HEAD_DIM:(g * KV_GROUP + hh + 1) * HEAD_DIM]
                 for hh in range(KV_GROUP)], axis=0)
            sink = jnp.zeros((rows, 1), F32)
            for hh in range(KV_GROUP):
                sink = jnp.where(head == hh, sinks_ref[g * KV_GROUP + hh], sink)
            s_c = jnp.where(cache_ok, _dot_nt(q4, ck_ref[b, :, c0:c1]) * scale, -jnp.inf)
            s_n = jnp.where(new_ok, _dot_nt(q4, knb[:, c0:c1]) * scale, -jnp.inf)
            o4 = _sink_softmax_pv([s_c, s_n], [cv_ref[b, :, c0:c1], vnb[:, c0:c1]], sink)
            for hh in range(KV_GROUP):
                h0 = (g * KV_GROUP + hh) * HEAD_DIM
                o_ref[pl.ds(r0, seq), h0:h0 + HEAD_DIM] = o4[hh * seq:(hh + 1) * seq]
        ok_ref[b, 0:w_buf - seq, :] = ck_ref[b, seq:w_buf, :]
        ok_ref[b, w_buf - seq:w_buf, :] = knb
        ov_ref[b, 0:w_buf - seq, :] = cv_ref[b, seq:w_buf, :]
        ov_ref[b, w_buf - seq:w_buf, :] = vnb
        return carry

    lax.fori_loop(0, n_b, per_sequence, 0)


def _swa_sample(qh, kh, proj, cache_k, cache_v, sinks, *, row0, seq, v_col0, nb=8):
    n_seq, w_buf, kv_cols = cache_k.shape
    q_cols = qh.shape[1]
    tr = nb * seq
    blk0 = row0 // tr
    cache_spec = pl.BlockSpec((nb, w_buf, kv_cols), lambda i: (i, 0, 0))
    return pl.pallas_call(
        functools.partial(_swa_sample_kernel, seq=seq),
        grid=(n_seq // nb,),
        in_specs=[
            pl.BlockSpec(memory_space=pltpu.SMEM),
            pl.BlockSpec((tr, q_cols), lambda i: (blk0 + i, 0)),
            pl.BlockSpec((tr, kv_cols), lambda i: (blk0 + i, 0)),
            pl.BlockSpec((tr, kv_cols), lambda i: (blk0 + i, v_col0 // kv_cols)),
            cache_spec, cache_spec,
        ],
        out_specs=[pl.BlockSpec((tr, q_cols), lambda i: (i, 0)), cache_spec, cache_spec],
        out_shape=[jax.ShapeDtypeStruct((n_seq * seq, q_cols), F32),
                   jax.ShapeDtypeStruct(cache_k.shape, F32),
                   jax.ShapeDtypeStruct(cache_v.shape, F32)],
        compiler_params=_cparams("parallel"),
        name="swa_sample",
    )(sinks, qh, kh, proj, cache_k, cache_v)


def _outproj_kernel(x_ref, oa_ref, ob_ref, ga_ref, gb_ref, w_ref, o_ref, m_ref):
    @pl.when(pl.program_id(1) == 0)
    def _():
        merged = jax.nn.sigmoid(ga_ref[...]) * oa_ref[...] + jax.nn.sigmoid(gb_ref[...]) * ob_ref[...]
        m_ref[...] = merged.astype(BF16)

    o_ref[...] = x_ref[...] + jnp.dot(m_ref[...], w_ref[...], preferred_element_type=F32)


def _outproj(x, o_a, o_b, proj, w_out, *, ga_col0, gb_col0, tm=256, tn=512):
    m, d = x.shape
    return pl.pallas_call(
        _outproj_kernel,
        grid=(m // tm, d // tn),
        in_specs=[
            pl.BlockSpec((tm, tn), lambda i, j: (i, j)),
            pl.BlockSpec((tm, d), lambda i, j: (i, 0)),
            pl.BlockSpec((tm, d), lambda i, j: (i, 0)),
            pl.BlockSpec((tm, d), lambda i, j: (i, ga_col0 // d)),
            pl.BlockSpec((tm, d), lambda i, j: (i, gb_col0 // d)),
            pl.BlockSpec((d, tn), lambda i, j: (0, j)),
        ],
        out_specs=pl.BlockSpec((tm, tn), lambda i, j: (i, j)),
        out_shape=jax.ShapeDtypeStruct((m, d), F32),
        scratch_shapes=[pltpu.VMEM((tm, d), BF16)],
        compiler_params=_cparams("parallel", "arbitrary"),
        name="outproj",
    )(x, o_a, o_b, proj, proj, w_out)


def _ple_kernel(xf_ref, x_ref, pe_ref, nw_ref, wp_ref, wg_ref, o_ref, xn_ref):
    @pl.when(pl.program_id(1) == 0)
    def _():
        xn_ref[...] = _rms(xf_ref[...], nw_ref[...]).astype(BF16)

    gate = jnp.dot(xn_ref[...], wg_ref[...], preferred_element_type=F32)
    emb = jnp.dot(pe_ref[...], wp_ref[...], preferred_element_type=F32)
    o_ref[...] = x_ref[...] + emb * jax.nn.sigmoid(gate)


def _ple(x, pe, norm_w, w_proj, w_gate, *, tm=512, tn=1024):
    m, d = x.shape
    pdim = pe.shape[1]
    return pl.pallas_call(
        _ple_kernel,
        grid=(m // tm, d // tn),
        in_specs=[
            pl.BlockSpec((tm, d), lambda i, j: (i, 0), pipeline_mode=pl.Buffered(1)),
            pl.BlockSpec((tm, tn), lambda i, j: (i, j)),
            pl.BlockSpec((tm, pdim), lambda i, j: (i, 0)),
            pl.BlockSpec((1, d), lambda i, j: (0, 0)),
            pl.BlockSpec((pdim, tn), lambda i, j: (0, j)),
            pl.BlockSpec((d, tn), lambda i, j: (0, j)),
        ],
        out_specs=pl.BlockSpec((tm, tn), lambda i, j: (i, j)),
        out_shape=jax.ShapeDtypeStruct((m, d), F32),
        scratch_shapes=[pltpu.VMEM((tm, d), BF16)],
        compiler_params=_cparams("parallel", "arbitrary"),
        name="ple",
    )(x, x, pe, norm_w.reshape(1, d), w_proj, w_gate)


def _rope_tables(seq, dec_batch, dec_seq):
    half = HEAD_DIM // 2
    inv_freq = ROPE_THETA ** (-jnp.arange(half, dtype=F32) / half)
    pos_p = jnp.arange(seq, dtype=jnp.int32)
    pos_s = jnp.tile(PAST_LEN + jnp.arange(dec_seq, dtype=jnp.int32), dec_batch)
    ang = jnp.concatenate([pos_p, pos_s]).astype(F32)[:, None] * inv_freq[None, :]
    cos, sin = jnp.cos(ang), jnp.sin(ang)
    return jnp.concatenate([cos, cos], axis=1), jnp.concatenate([-sin, sin], axis=1)


def kernel(x_prompt, x_sample, p_prompt, p_sample, state_gdn, state_conv, cache_swa_k, cache_swa_v, ffn1_norm, ffn1_w_gate, ffn1_w_up, ffn1_w_down, mix_norm, w_in, conv_w, A_log, dt_bias, gdn_norm_w, q_norm_w, k_norm_w, sinks, w_out, ffn2_norm, ffn2_w_gate, ffn2_w_up, ffn2_w_down, ple_norm, w_ple_proj, w_ple_gate):
    depth, dec_batch, n_v_heads, dk, dv = state_gdn.shape
    assert depth == 1 and dk == HEAD_DIM and dv == HEAD_DIM
    bp, seq, d = x_prompt.shape
    assert bp == 1
    dec_seq = x_sample.shape[1]
    w_buf, n_kv_heads = cache_swa_k.shape[2], cache_swa_k.shape[3]
    n_heads = sinks.shape[1]
    assert n_heads == n_kv_heads * KV_GROUP and w_buf == WINDOW and CHUNK % dec_seq == 0
    conv_dim = conv_w.shape[2]
    n_qk_heads = n_v_heads // GDN_REP
    key_dim = n_qk_heads * HEAD_DIM
    val_dim = n_v_heads * HEAD_DIM
    kv_cols = n_kv_heads * HEAD_DIM
    n_samp = dec_batch * dec_seq

    splits = (conv_dim, val_dim, n_v_heads, n_v_heads, n_heads * HEAD_DIM, kv_cols, kv_cols, d, d)
    offs = [0]
    for s in splits:
        offs.append(offs[-1] + s)
    assert offs[-1] == w_in.shape[2]
    win = w_in[0]
    seg = lambda k: win[:, offs[k]:offs[k + 1]]
    w_main = jnp.concatenate([seg(0), seg(1), seg(4), seg(7), seg(8), seg(5), seg(6)], axis=1).astype(BF16)
    n_ab = 128
    w_ab = jnp.pad(jnp.concatenate([seg(2), seg(3)], axis=1), ((0, 0), (0, n_ab - 2 * n_v_heads))).astype(BF16)
    z_col0 = conv_dim
    q_col0 = z_col0 + val_dim
    ga_col0 = q_col0 + n_heads * HEAD_DIM
    gb_col0 = ga_col0 + d
    k_col0 = gb_col0 + d
    v_col0 = k_col0 + kv_cols

    bf = lambda w: w[0].astype(BF16)
    x = jnp.concatenate([x_prompt.reshape(seq, d), x_sample.reshape(n_samp, d)], axis=0)
    pe = jnp.concatenate([p_prompt.reshape(seq, PLE_DIM), p_sample.reshape(n_samp, PLE_DIM)], axis=0)

    x = _ffn(x, ffn1_norm[0], bf(ffn1_w_gate), bf(ffn1_w_up), bf(ffn1_w_down))
    proj, ab = _inproj(x, mix_norm[0], w_main, w_ab)

    carried = jnp.pad(state_conv[0], ((0, 0), (dec_seq - (CONV_W - 1), 0), (0, 0))).reshape(n_samp, conv_dim)
    act_p = _conv_prompt(proj, conv_w[0], rows=seq, conv_dim=conv_dim, key_dim=key_dim)
    act_s = _conv_sample(proj, carried, conv_w[0], row0=seq, rows=n_samp, seq=dec_seq,
                         conv_dim=conv_dim, key_dim=key_dim)
    ab_t = ab.T
    gdn_args = dict(n_qk_heads=n_qk_heads, z_col0=z_col0)
    oa_p, sg_p = _gdn_prompt(act_p, proj, ab, ab_t, A_log[0], dt_bias[0], gdn_norm_w[0], rows=seq, **gdn_args)
    oa_s, sg_s = _gdn_sample(act_s, proj, ab, ab_t, A_log[0], dt_bias[0], gdn_norm_w[0], state_gdn[0],
                             row0=seq, rows=n_samp, seq=dec_seq, **gdn_args)

    cos, sin = _rope_tables(seq, dec_batch, dec_seq)
    qh, kh = _rope(proj, cos, sin, q_norm_w[0], k_norm_w[0], q_col0=q_col0, q_cols=n_heads * HEAD_DIM,
                   k_col0=k_col0, k_cols=kv_cols)
    ob_p = _swa_prompt(qh, kh, proj, sinks[0], rows=seq, n_kv_heads=n_kv_heads, v_col0=v_col0)
    ob_s, kk_s, vv_s = _swa_sample(qh, kh, proj, cache_swa_k[0].reshape(dec_batch, w_buf, kv_cols),
                                   cache_swa_v[0].reshape(dec_batch, w_buf, kv_cols), sinks[0],
                                   row0=seq, seq=dec_seq, v_col0=v_col0)

    o_a = jnp.concatenate([oa_p, oa_s], axis=0)
    o_b = jnp.concatenate([ob_p, ob_s], axis=0)
    x = _outproj(x, o_a, o_b, proj, bf(w_out), ga_col0=ga_col0, gb_col0=gb_col0)
    x = _ffn(x, ffn2_norm[0], bf(ffn2_w_gate), bf(ffn2_w_up), bf(ffn2_w_down))
    x = _ple(x, pe.astype(BF16), ple_norm[0], bf(w_ple_proj), bf(w_ple_gate))

    cache_shape = (1, 1, w_buf, n_kv_heads, HEAD_DIM)
    return (
        x[:seq].reshape(1, seq, d),
        x[seq:].reshape(dec_batch, dec_seq, d),
        sg_p.reshape(1, 1, n_v_heads, HEAD_DIM, HEAD_DIM),
        proj[seq - (CONV_W - 1):seq, :conv_dim].reshape(1, 1, CONV_W - 1, conv_dim),
        kh[seq - w_buf:seq].reshape(cache_shape),
        proj[seq - w_buf:seq, v_col0:v_col0 + kv_cols].reshape(cache_shape),
        sg_s.reshape(1, dec_batch, n_v_heads, HEAD_DIM, HEAD_DIM),
        proj[seq:, :conv_dim].reshape(dec_batch, dec_seq, conv_dim)[:, dec_seq - (CONV_W - 1):].reshape(
            1, dec_batch, CONV_W - 1, conv_dim),
        kk_s.reshape(1, dec_batch, w_buf, n_kv_heads, HEAD_DIM),
        vv_s.reshape(1, dec_batch, w_buf, n_kv_heads, HEAD_DIM),
    )
```

```python
import functools

import jax
import jax.numpy as jnp
from jax import lax
from jax.experimental import pallas as pl
from jax.experimental.pallas import tpu as pltpu

F32 = jnp.float32
BF16 = jnp.bfloat16

EPS = 1e-6
HEAD_DIM = 128
KV_GROUP = 4
WINDOW = 128
ROPE_THETA = 10000.0
PAST_LEN = 8192
GDN_REP = 2
CONV_W = 4
CHUNK = 64
PLE_DIM = 256
GDN_STEP_HEADS = 4
SWA_STEP_GROUPS = 2

V7X_VMEM_LIMIT_BYTES = 56 * 1024 * 1024


def _cparams(*sem):
    return pltpu.CompilerParams(dimension_semantics=sem, vmem_limit_bytes=V7X_VMEM_LIMIT_BYTES)


def _rms(x, w):
    return x * lax.rsqrt(jnp.mean(x * x, axis=-1, keepdims=True) + EPS) * w


def _dot(a, b):
    return jnp.dot(a.astype(BF16), b.astype(BF16), preferred_element_type=F32)


def _dot_nt(a, b):
    return lax.dot_general(a.astype(BF16), b.astype(BF16), (((1,), (1,)), ((), ())),
                           preferred_element_type=F32)


def _dot_tn(a, b):
    return lax.dot_general(a.astype(BF16), b.astype(BF16), (((0,), (0,)), ((), ())),
                           preferred_element_type=F32)


def _ffn_kernel(x_ref, nw_ref, wg_ref, wu_ref, wd_ref, o_ref, xn_ref):
    @pl.when(pl.program_id(1) == 0)
    def _():
        x = x_ref[...]
        xn_ref[...] = _rms(x, nw_ref[...]).astype(BF16)
        o_ref[...] = x

    xn = xn_ref[...]
    g = jnp.dot(xn, wg_ref[...], preferred_element_type=F32)
    u = jnp.dot(xn, wu_ref[...], preferred_element_type=F32)
    h = (0.5 * (g * jax.nn.sigmoid(g))) * u
    o_ref[...] += jnp.dot(h.astype(BF16), wd_ref[...], preferred_element_type=F32)


def _ffn(x, norm_w, wg, wu, wd, *, tm=512, tf=256):
    m, d = x.shape
    f = wg.shape[1]
    return pl.pallas_call(
        _ffn_kernel,
        grid=(m // tm, f // tf),
        in_specs=[
            pl.BlockSpec((tm, d), lambda i, j: (i, 0), pipeline_mode=pl.Buffered(1)),
            pl.BlockSpec((1, d), lambda i, j: (0, 0)),
            pl.BlockSpec((d, tf), lambda i, j: (0, j)),
            pl.BlockSpec((d, tf), lambda i, j: (0, j)),
            pl.BlockSpec((tf, d), lambda i, j: (j, 0)),
        ],
        out_specs=pl.BlockSpec((tm, d), lambda i, j: (i, 0)),
        out_shape=jax.ShapeDtypeStruct((m, d), F32),
        scratch_shapes=[pltpu.VMEM((tm, d), BF16)],
        compiler_params=_cparams("parallel", "arbitrary"),
        name="ffn",
    )(x, norm_w.reshape(1, d), wg, wu, wd)


def _inproj_kernel(x_ref, nw_ref, w_ref, o_ref, xn_ref):
    @pl.when(pl.program_id(1) == 0)
    def _():
        xn_ref[...] = _rms(x_ref[...], nw_ref[...]).astype(BF16)

    o_ref[...] = jnp.dot(xn_ref[...], w_ref[...], preferred_element_type=F32)


def _inproj_ab_kernel(x_ref, nw_ref, w_ref, wab_ref, o_ref, ab_ref, xn_ref):
    @pl.when(pl.program_id(1) == 0)
    def _():
        xn = _rms(x_ref[...], nw_ref[...]).astype(BF16)
        xn_ref[...] = xn
        ab_ref[...] = jnp.dot(xn, wab_ref[...], preferred_element_type=F32)

    o_ref[...] = jnp.dot(xn_ref[...], w_ref[...], preferred_element_type=F32)


def _inproj(x, norm_w, w, w_ab=None, *, tm=512, tn=1024):
    m, d = x.shape
    n = w.shape[1]
    in_specs = [
        pl.BlockSpec((tm, d), lambda i, j: (i, 0), pipeline_mode=pl.Buffered(1)),
        pl.BlockSpec((1, d), lambda i, j: (0, 0)),
        pl.BlockSpec((d, tn), lambda i, j: (0, j)),
    ]
    out_specs = [pl.BlockSpec((tm, tn), lambda i, j: (i, j))]
    out_shape = [jax.ShapeDtypeStruct((m, n), F32)]
    args = [x, norm_w.reshape(1, d), w]
    body = _inproj_kernel
    if w_ab is not None:
        nab = w_ab.shape[1]
        in_specs.append(pl.BlockSpec((d, nab), lambda i, j: (0, 0)))
        out_specs.append(pl.BlockSpec((tm, nab), lambda i, j: (i, 0)))
        out_shape.append(jax.ShapeDtypeStruct((m, nab), F32))
        args.append(w_ab)
        body = _inproj_ab_kernel
    return pl.pallas_call(
        body,
        grid=(m // tm, n // tn),
        in_specs=in_specs,
        out_specs=out_specs,
        out_shape=out_shape,
        scratch_shapes=[pltpu.VMEM((tm, d), BF16)],
        compiler_params=_cparams("parallel", "arbitrary"),
        name="inproj",
    )(*args)


def _conv_finish(y, j, o_ref, *, n_q_blocks, n_qk_blocks, q_scale):
    y = y * jax.nn.sigmoid(y)
    cw = y.shape[1]
    is_qk = j < n_qk_blocks
    scale = jnp.where(j < n_q_blocks, q_scale, 1.0)
    for h in range(cw // HEAD_DIM):
        yh = y[:, h * HEAD_DIM:(h + 1) * HEAD_DIM]
        nrm = lax.rsqrt(jnp.sum(yh * yh, axis=-1, keepdims=True) + EPS) * scale
        o_ref[:, h * HEAD_DIM:(h + 1) * HEAD_DIM] = yh * jnp.where(is_qk, nrm, 1.0)


def _conv_prompt_kernel(x_ref, halo_ref, cw_ref, o_ref, **kw):
    i = pl.program_id(0)
    j = pl.program_id(1)
    x = x_ref[...]
    halo = jnp.where(i > 0, halo_ref[...], 0.0)
    w = cw_ref[...]
    row8 = lax.broadcasted_iota(jnp.int32, (8, 1), 0)
    y = None
    for tap in range(CONV_W):
        sh = CONV_W - 1 - tap
        if sh == 0:
            xs = x
        else:
            rolled = pltpu.roll(x, sh, 0)
            head = jnp.where(row8 < sh, pltpu.roll(halo, sh, 0), rolled[:8])
            xs = jnp.concatenate([head, rolled[8:]], axis=0)
        term = xs * w[tap:tap + 1]
        y = term if y is None else y + term
    _conv_finish(y, j, o_ref, **kw)


def _conv_sample_kernel(x_ref, e_ref, cw_ref, o_ref, *, seq, **kw):
    j = pl.program_id(1)
    x = x_ref[...]
    e = e_ref[...]
    w = cw_ref[...]
    rows = x.shape[0]
    t = lax.broadcasted_iota(jnp.int32, (rows, 1), 0) % seq
    y = None
    for tap in range(CONV_W):
        sh = CONV_W - 1 - tap
        if sh == 0:
            xs = x
        else:
            xs = jnp.where(t >= sh, pltpu.roll(x, sh, 0), pltpu.roll(e, rows + sh - seq, 0))
        term = xs * w[tap:tap + 1]
        y = term if y is None else y + term
    _conv_finish(y, j, o_ref, **kw)


def _conv_common(n_q_cols, n_qk_cols, cw):
    return dict(n_q_blocks=n_q_cols // cw, n_qk_blocks=n_qk_cols // cw, q_scale=HEAD_DIM ** -0.5)


def _conv_prompt(proj, conv_w, *, rows, conv_dim, key_dim, tc=512, cw=512):
    tc8 = tc // 8
    return pl.pallas_call(
        functools.partial(_conv_prompt_kernel, **_conv_common(key_dim, 2 * key_dim, cw)),
        grid=(rows // tc, conv_dim // cw),
        in_specs=[
            pl.BlockSpec((tc, cw), lambda i, j: (i, j)),
            pl.BlockSpec((8, cw), lambda i, j: (jnp.maximum(i * tc8 - 1, 0), j)),
            pl.BlockSpec((CONV_W, cw), lambda i, j: (0, j)),
        ],
        out_specs=pl.BlockSpec((tc, cw), lambda i, j: (i, j)),
        out_shape=jax.ShapeDtypeStruct((rows, conv_dim), F32),
        compiler_params=_cparams("parallel", "parallel"),
        name="conv_prompt",
    )(proj, proj, conv_w)


def _conv_sample(proj, e, conv_w, *, row0, rows, seq, conv_dim, key_dim, tc=512, cw=512):
    blk0 = row0 // tc
    return pl.pallas_call(
        functools.partial(_conv_sample_kernel, seq=seq, **_conv_common(key_dim, 2 * key_dim, cw)),
        grid=(rows // tc, conv_dim // cw),
        in_specs=[
            pl.BlockSpec((tc, cw), lambda i, j: (blk0 + i, j)),
            pl.BlockSpec((tc, cw), lambda i, j: (i, j)),
            pl.BlockSpec((CONV_W, cw), lambda i, j: (0, j)),
        ],
        out_specs=pl.BlockSpec((tc, cw), lambda i, j: (i, j)),
        out_shape=jax.ShapeDtypeStruct((rows, conv_dim), F32),
        compiler_params=_cparams("parallel", "parallel"),
        name="conv_sample",
    )(proj, e, conv_w)


def _softplus(x):
    return jnp.maximum(x, 0.0) + jnp.log1p(jnp.exp(-jnp.abs(x)))


def _seg_cumsum(x, pos, seg, axis):
    d = 1
    while d < seg:
        x = x + jnp.where(pos >= d, pltpu.roll(x, d, axis), 0.0)
        d *= 2
    return x


def _gates(ab, abt8, alog_row, dtb_row, alog_col8, dtb_col8, seg):
    rows = ab.shape[0]
    pos_c = lax.broadcasted_iota(jnp.int32, (rows, 1), 0) % seg
    pos_r = lax.broadcasted_iota(jnp.int32, (1, rows), 1) % seg
    g_all = -jnp.exp(alog_row) * _softplus(ab + dtb_row)
    g_t = -jnp.exp(alog_col8) * _softplus(abt8 + dtb_col8)
    return _seg_cumsum(g_all, pos_c, seg, 0), jax.nn.sigmoid(ab), _seg_cumsum(g_t, pos_r, seg, 1)


def _pick_lane(x, idx):
    lane = lax.broadcasted_iota(jnp.int32, (1, x.shape[1]), 1)
    return jnp.sum(jnp.where(lane == idx, x, 0.0), axis=1, keepdims=True)


def _pick_sublane(x, idx):
    sub = lax.broadcasted_iota(jnp.int32, (x.shape[0], 1), 0)
    return jnp.sum(jnp.where(sub == idx, x, 0.0), axis=0, keepdims=True)


def _head_gates(gate_vals, h, n_heads):
    gc_all, sig_all, gc_t = gate_vals
    return _pick_lane(sig_all, n_heads + h), _pick_lane(gc_all, h), _pick_sublane(gc_t, h % 8)


def _load_gates(ab_ref, abt_ref, alr_ref, dtr_ref, alc_ref, dtc_ref, h0, seg):
    h8 = pl.multiple_of((h0 // 8) * 8, 8)
    return _gates(ab_ref[...], abt_ref[pl.ds(h8, 8), :], alr_ref[...], dtr_ref[...],
                  alc_ref[pl.ds(h8, 8), :], dtc_ref[pl.ds(h8, 8), :], seg)


def _chunk_masks(seg):
    r = lax.broadcasted_iota(jnp.int32, (CHUNK, CHUNK), 0)
    c = lax.broadcasted_iota(jnp.int32, (CHUNK, CHUNK), 1)
    same = (r // seg) == (c // seg)
    return same & (c <= r), same & (c < r), r == c


def _gdn_intra(q_ref, k_ref, v_ref, gates, seg):
    incl, strict, eye = _chunk_masks(seg)
    n_chunks = q_ref.shape[0] // CHUNK
    n_heads = len(gates)
    chunks = range(n_chunks)
    heads = range(n_heads)
    items = [(n, hh) for n in chunks for hh in heads]
    rows = lambda n: slice(n * CHUNK, (n + 1) * CHUNK)
    cols = lambda j: slice(j * HEAD_DIM, (j + 1) * HEAD_DIM)
    pairs = [(n, j) for n in chunks for j in range(n_heads // GDN_REP)]
    q = {(n, j): q_ref[rows(n), cols(j)] for n, j in pairs}
    k = {(n, j): k_ref[rows(n), cols(j)] for n, j in pairs}
    kk = {p: _dot_nt(k[p], k[p]) for p in pairs}
    qk = {p: _dot_nt(q[p], k[p]) for p in pairs}
    out = {}
    low, rhs = {}, {}
    for n, hh in items:
        beta, gc_col, gc_row = gates[hh]
        beta, gc_col, gc_row = beta[rows(n)], gc_col[rows(n)], gc_row[:, rows(n)]
        pj = (n, hh // GDN_REP)
        decay = jnp.exp(jnp.where(incl, gc_col - gc_row, -jnp.inf))
        low[n, hh] = jnp.where(strict, beta * kk[pj] * decay, 0.0)
        e_gc = jnp.exp(gc_col)
        rhs[n, hh] = jnp.concatenate([v_ref[rows(n), cols(hh)] * beta, k[pj] * (beta * e_gc)], axis=1)
        out[n, hh] = dict(a=jnp.where(incl, qk[pj] * decay, 0.0), qd=q[pj] * e_gc, gc=gc_col, k=k[pj])
    p = {it: -low[it] for it in items}
    t = {it: jnp.where(eye, 1.0, p[it]) for it in items}
    n_pow = 2
    while n_pow < seg:
        p = {it: _dot(p[it], p[it]) for it in items}
        t = {it: t[it] + _dot(t[it], p[it]) for it in items}
        n_pow *= 2
    for it in items:
        uw = _dot(t[it], rhs[it])
        out[it]["u"] = uw[:, :HEAD_DIM]
        out[it]["w"] = uw[:, HEAD_DIM:]
    return out


def _gdn_emit(o, z, gate, nw):
    return (_rms(o, nw) * (z * jax.nn.sigmoid(z)) * jax.nn.sigmoid(gate)).astype(BF16)


def _gdn_prompt_kernel(q_ref, k_ref, v_ref, z_ref, ga_ref, ab_ref, abt_ref, alr_ref, dtr_ref, alc_ref,
                       dtc_ref, nw_ref, o_ref, sfin_ref, s_ref, *, n_heads):
    h0 = pl.program_id(0) * GDN_STEP_HEADS
    i = pl.program_id(1)

    @pl.when(i == 0)
    def _():
        s_ref[...] = jnp.zeros_like(s_ref)

    nw = nw_ref[...]
    heads = range(GDN_STEP_HEADS)
    gate_vals = _load_gates(ab_ref, abt_ref, alr_ref, dtr_ref, alc_ref, dtc_ref, h0, CHUNK)
    gates = [_head_gates(gate_vals, h0 + hh, n_heads) for hh in heads]
    intra = _gdn_intra(q_ref, k_ref, v_ref, gates, CHUNK)
    state = [s_ref[hh] for hh in heads]
    for n in range(q_ref.shape[0] // CHUNK):
        rows = slice(n * CHUNK, (n + 1) * CHUNK)
        c = [intra[n, hh] for hh in heads]
        ws_qs = [_dot(jnp.concatenate([c[hh]["w"], c[hh]["qd"]], axis=0), state[hh]) for hh in heads]
        v_new = [c[hh]["u"] - ws_qs[hh][:CHUNK] for hh in heads]
        o = [ws_qs[hh][CHUNK:] + _dot(c[hh]["a"], v_new[hh]) for hh in heads]
        g_last = [c[hh]["gc"][CHUNK - 1:CHUNK] for hh in heads]
        state = [state[hh] * jnp.exp(g_last[hh])
                 + _dot_tn(c[hh]["k"] * jnp.exp(g_last[hh] - c[hh]["gc"]), v_new[hh]) for hh in heads]
        for hh in heads:
            cs = slice(hh * HEAD_DIM, (hh + 1) * HEAD_DIM)
            o_ref[rows, cs] = _gdn_emit(o[hh], z_ref[rows, cs], ga_ref[rows, cs], nw)
    for hh in heads:
        s_ref[hh] = state[hh]

    @pl.when(i == pl.num_programs(1) - 1)
    def _():
        sfin_ref[...] = s_ref[...]


def _gate_param_specs(n_lanes, idx):
    row = pl.BlockSpec((1, n_lanes), idx)
    col = pl.BlockSpec((n_lanes, 1), idx)
    return [row, row, col, col]


def _gate_params(alog, dtb, n_lanes):
    pad = lambda v: jnp.pad(v.astype(F32), (0, n_lanes - v.shape[0]))
    al, dt = pad(alog), pad(dtb)
    return al.reshape(1, n_lanes), dt.reshape(1, n_lanes), al.reshape(n_lanes, 1), dt.reshape(n_lanes, 1)


def _gdn_prompt(act, proj_a, proj_b, ab, ab_t, alog, dtb, norm_w, *, rows, n_heads, z_col0, ga_col0, tc=256):
    hs = GDN_STEP_HEADS
    qw = hs // GDN_REP * HEAD_DIM
    vw = hs * HEAD_DIM
    key_dim = n_heads // GDN_REP * HEAD_DIM
    n_lanes = ab.shape[1]
    return pl.pallas_call(
        functools.partial(_gdn_prompt_kernel, n_heads=n_heads),
        grid=(n_heads // hs, rows // tc),
        in_specs=[
            pl.BlockSpec((tc, qw), lambda h, i: (i, h)),
            pl.BlockSpec((tc, qw), lambda h, i: (i, key_dim // qw + h)),
            pl.BlockSpec((tc, vw), lambda h, i: (i, 2 * key_dim // vw + h)),
            pl.BlockSpec((tc, vw), lambda h, i: (i, z_col0 // vw + h)),
            pl.BlockSpec((tc, vw), lambda h, i: (i, ga_col0 // vw + h)),
            pl.BlockSpec((tc, n_lanes), lambda h, i: (i, 0)),
            pl.BlockSpec((n_lanes, tc), lambda h, i: (0, i)),
            *_gate_param_specs(n_lanes, lambda h, i: (0, 0)),
            pl.BlockSpec((1, HEAD_DIM), lambda h, i: (0, 0)),
        ],
        out_specs=[
            pl.BlockSpec((tc, vw), lambda h, i: (i, h)),
            pl.BlockSpec((hs, HEAD_DIM, HEAD_DIM), lambda h, i: (h, 0, 0)),
        ],
        out_shape=[jax.ShapeDtypeStruct((rows, n_heads * HEAD_DIM), BF16),
                   jax.ShapeDtypeStruct((n_heads, HEAD_DIM, HEAD_DIM), F32)],
        scratch_shapes=[pltpu.VMEM((hs, HEAD_DIM, HEAD_DIM), F32)],
        compiler_params=_cparams("parallel", "arbitrary"),
        name="gdn_prompt",
    )(act, act, act, proj_a, proj_b, ab, ab_t, *_gate_params(alog, dtb, n_lanes), norm_w.reshape(1, HEAD_DIM))


def _gdn_sample_kernel(q_ref, k_ref, v_ref, z_ref, ga_ref, ab_ref, abt_ref, alr_ref, dtr_ref, alc_ref,
                       dtc_ref, nw_ref, s0_ref, o_ref, s1_ref, *, n_heads, seq):
    h0 = pl.program_id(1) * GDN_STEP_HEADS
    nw = nw_ref[...]
    heads = range(GDN_STEP_HEADS)
    gate_vals = _load_gates(ab_ref, abt_ref, alr_ref, dtr_ref, alc_ref, dtc_ref, h0, seq)
    gates = [_head_gates(gate_vals, h0 + hh, n_heads) for hh in heads]
    intra = _gdn_intra(q_ref, k_ref, v_ref, gates, seq)
    per_chunk = CHUNK // seq
    for n in range(q_ref.shape[0] // CHUNK):
        rows = slice(n * CHUNK, (n + 1) * CHUNK)
        seqs = [(hh, b) for hh in heads for b in range(per_chunk)]
        sub = lambda b: slice(b * seq, (b + 1) * seq)
        c = [intra[n, hh] for hh in heads]
        s0 = {(hh, b): s0_ref[n * per_chunk + b, hh] for hh, b in seqs}
        ws_qs = {(hh, b): _dot(jnp.concatenate([c[hh]["w"][sub(b)], c[hh]["qd"][sub(b)]], axis=0), s0[hh, b])
                 for hh, b in seqs}
        v_new = {(hh, b): c[hh]["u"][sub(b)] - ws_qs[hh, b][:seq] for hh, b in seqs}
        for hh, b in seqs:
            gc = c[hh]["gc"][sub(b)]
            g_last = gc[seq - 1:seq]
            k_st = c[hh]["k"][sub(b)] * jnp.exp(g_last - gc)
            s1_ref[n * per_chunk + b, hh] = s0[hh, b] * jnp.exp(g_last) + _dot_tn(k_st, v_new[hh, b])
        for hh in heads:
            cs = slice(hh * HEAD_DIM, (hh + 1) * HEAD_DIM)
            qs = jnp.concatenate([ws_qs[hh, b][seq:] for b in range(per_chunk)], axis=0)
            vn = jnp.concatenate([v_new[hh, b] for b in range(per_chunk)], axis=0)
            out = _gdn_emit(qs + _dot(c[hh]["a"], vn), z_ref[rows, cs], ga_ref[rows, cs], nw)
            o_ref[rows, cs] = out.astype(o_ref.dtype)


def _gdn_sample(act, proj_a, proj_b, ab, ab_t, alog, dtb, norm_w, s0, *, row0, rows, seq, n_heads, z_col0,
                ga_col0, tc=128):
    hs = GDN_STEP_HEADS
    qw = hs // GDN_REP * HEAD_DIM
    vw = hs * HEAD_DIM
    key_dim = n_heads // GDN_REP * HEAD_DIM
    blk0 = row0 // tc
    n_lanes = ab.shape[1]
    state_spec = pl.BlockSpec((tc // seq, hs, HEAD_DIM, HEAD_DIM), lambda i, h: (i, h, 0, 0))
    return pl.pallas_call(
        functools.partial(_gdn_sample_kernel, n_heads=n_heads, seq=seq),
        grid=(rows // tc, n_heads // hs),
        in_specs=[
            pl.BlockSpec((tc, qw), lambda i, h: (i, h)),
            pl.BlockSpec((tc, qw), lambda i, h: (i, key_dim // qw + h)),
            pl.BlockSpec((tc, vw), lambda i, h: (i, 2 * key_dim // vw + h)),
            pl.BlockSpec((tc, vw), lambda i, h: (blk0 + i, z_col0 // vw + h)),
            pl.BlockSpec((tc, vw), lambda i, h: (blk0 + i, ga_col0 // vw + h)),
            pl.BlockSpec((tc, n_lanes), lambda i, h: (blk0 + i, 0)),
            pl.BlockSpec((n_lanes, tc), lambda i, h: (0, blk0 + i)),
            *_gate_param_specs(n_lanes, lambda i, h: (0, 0)),
            pl.BlockSpec((1, HEAD_DIM), lambda i, h: (0, 0)),
            state_spec,
        ],
        out_specs=[pl.BlockSpec((tc, vw), lambda i, h: (i, h)), state_spec],
        out_shape=[jax.ShapeDtypeStruct((rows, n_heads * HEAD_DIM), F32),
                   jax.ShapeDtypeStruct(s0.shape, F32)],
        compiler_params=_cparams("parallel", "parallel"),
        name="gdn_sample",
    )(act, act, act, proj_a, proj_b, ab, ab_t, *_gate_params(alog, dtb, n_lanes),
      norm_w.reshape(1, HEAD_DIM), s0)


def _rope_kernel(sq_ref, sk_ref, cos_ref, sin_ref, qw_ref, kw_ref, qo_ref, ko_ref):
    cos = cos_ref[...]
    sin = sin_ref[...]
    for src, w_ref, dst in ((sq_ref, qw_ref, qo_ref), (sk_ref, kw_ref, ko_ref)):
        w = w_ref[...]
        for h in range(src.shape[1] // HEAD_DIM):
            c0, c1 = h * HEAD_DIM, (h + 1) * HEAD_DIM
            y = _rms(src[:, c0:c1], w)
            dst[:, c0:c1] = y * cos + pltpu.roll(y, HEAD_DIM // 2, 1) * sin


def _rope(proj, cos, sin, q_norm_w, k_norm_w, *, q_col0, q_cols, k_col0, k_cols, tr=256):
    m = proj.shape[0]
    return pl.pallas_call(
        _rope_kernel,
        grid=(m // tr,),
        in_specs=[
            pl.BlockSpec((tr, q_cols), lambda i: (i, q_col0 // q_cols)),
            pl.BlockSpec((tr, k_cols), lambda i: (i, k_col0 // k_cols)),
            pl.BlockSpec((tr, HEAD_DIM), lambda i: (i, 0)),
            pl.BlockSpec((tr, HEAD_DIM), lambda i: (i, 0)),
            pl.BlockSpec((1, HEAD_DIM), lambda i: (0, 0)),
            pl.BlockSpec((1, HEAD_DIM), lambda i: (0, 0)),
        ],
        out_specs=[
            pl.BlockSpec((tr, q_cols), lambda i: (i, 0)),
            pl.BlockSpec((tr, k_cols), lambda i: (i, 0)),
        ],
        out_shape=[jax.ShapeDtypeStruct((m, q_cols), F32), jax.ShapeDtypeStruct((m, k_cols), F32)],
        compiler_params=_cparams("parallel"),
        name="rope",
    )(proj, proj, cos, sin, q_norm_w.reshape(1, HEAD_DIM), k_norm_w.reshape(1, HEAD_DIM))


def _merge(other, gate, o):
    return (other.astype(F32) + jax.nn.sigmoid(gate) * o).astype(BF16)


def _swa_prompt_kernel(sinks_ref, q_ref, kc_ref, kp_ref, vc_ref, vp_ref, ma_ref, gb_ref, o_ref):
    g0 = pl.program_id(0) * SWA_STEP_GROUPS
    i = pl.program_id(1)
    rows = KV_GROUP * WINDOW
    r = lax.broadcasted_iota(jnp.int32, (rows, 2 * WINDOW), 0) % WINDOW
    c = lax.broadcasted_iota(jnp.int32, (rows, 2 * WINDOW), 1)
    visible = ((c < WINDOW) & (c > r) & (i > 0)) | ((c >= WINDOW) & (c - WINDOW <= r))
    head = lax.broadcasted_iota(jnp.int32, (rows, 1), 0) // WINDOW
    scale = HEAD_DIM ** -0.5
    groups = range(SWA_STEP_GROUPS)
    hcols = lambda g, hh: slice((g * KV_GROUP + hh) * HEAD_DIM, (g * KV_GROUP + hh + 1) * HEAD_DIM)
    gcols = lambda g: slice(g * HEAD_DIM, (g + 1) * HEAD_DIM)
    q4 = [jnp.concatenate([q_ref[:, hcols(g, hh)] for hh in range(KV_GROUP)], axis=0) for g in groups]
    kcat = [jnp.concatenate([kp_ref[:, gcols(g)], kc_ref[:, gcols(g)]], axis=0) for g in groups]
    vcat = [jnp.concatenate([vp_ref[:, gcols(g)], vc_ref[:, gcols(g)]], axis=0) for g in groups]
    s = [_dot_nt(q4[g], kcat[g]) for g in groups]
    p, den = [], []
    for g in groups:
        sink = jnp.zeros((rows, 1), F32)
        for hh in range(KV_GROUP):
            sink = jnp.where(head == hh, sinks_ref[(g0 + g) * KV_GROUP + hh], sink)
        sg = jnp.where(visible, s[g] * scale, -jnp.inf)
        m = jnp.maximum(jnp.max(sg, axis=-1, keepdims=True), sink)
        pg = jnp.exp(sg - m)
        p.append(pg)
        den.append(jnp.sum(pg, axis=-1, keepdims=True) + jnp.exp(sink - m))
    o4 = [_dot(p[g], vcat[g]) / den[g] for g in groups]
    for g in groups:
        for hh in range(KV_GROUP):
            cs = hcols(g, hh)
            o_ref[:, cs] = _merge(ma_ref[:, cs], gb_ref[:, cs], o4[g][hh * WINDOW:(hh + 1) * WINDOW])


def _swa_prompt(qh, kh, proj, merged_a, sinks, *, rows, n_kv_heads, v_col0, gb_col0):
    sg = SWA_STEP_GROUPS
    qw = sg * KV_GROUP * HEAD_DIM
    kw = sg * HEAD_DIM
    prev = lambda i: jnp.maximum(i - 1, 0)
    return pl.pallas_call(
        _swa_prompt_kernel,
        grid=(n_kv_heads // sg, rows // WINDOW),
        in_specs=[
            pl.BlockSpec(memory_space=pltpu.SMEM),
            pl.BlockSpec((WINDOW, qw), lambda g, i: (i, g)),
            pl.BlockSpec((WINDOW, kw), lambda g, i: (i, g)),
            pl.BlockSpec((WINDOW, kw), lambda g, i: (prev(i), g)),
            pl.BlockSpec((WINDOW, kw), lambda g, i: (i, v_col0 // kw + g)),
            pl.BlockSpec((WINDOW, kw), lambda g, i: (prev(i), v_col0 // kw + g)),
            pl.BlockSpec((WINDOW, qw), lambda g, i: (i, g)),
            pl.BlockSpec((WINDOW, qw), lambda g, i: (i, gb_col0 // qw + g)),
        ],
        out_specs=pl.BlockSpec((WINDOW, qw), lambda g, i: (i, g)),
        out_shape=jax.ShapeDtypeStruct((rows, n_kv_heads * KV_GROUP * HEAD_DIM), BF16),
        compiler_params=_cparams("parallel", "parallel"),
        name="swa_prompt",
    )(sinks, qh, kh, kh, proj, proj, merged_a, proj)


def _swa_sample_kernel(sinks_ref, q_ref, kn_ref, vn_ref, ck_ref, cv_ref, ma_ref, gb_lo_ref, gb_hi_ref,
                       o_ref, ok_ref, ov_ref, *, seq):
    n_b, w_buf, kv_cols = ck_ref.shape
    half = gb_lo_ref.shape[1]
    rows = KV_GROUP * seq
    t = lax.broadcasted_iota(jnp.int32, (rows, 1), 0) % seq
    head = lax.broadcasted_iota(jnp.int32, (rows, 1), 0) // seq
    jc = lax.broadcasted_iota(jnp.int32, (1, w_buf), 1)
    jn = lax.broadcasted_iota(jnp.int32, (1, seq), 1)
    dist_c = t + w_buf - jc
    cache_ok = (dist_c >= 0) & (dist_c < WINDOW)
    new_ok = jn <= t
    scale = HEAD_DIM ** -0.5
    n_groups = kv_cols // HEAD_DIM
    sinks = []
    for g in range(n_groups):
        sink = jnp.zeros((rows, 1), F32)
        for hh in range(KV_GROUP):
            sink = jnp.where(head == hh, sinks_ref[g * KV_GROUP + hh], sink)
        sinks.append(sink)

    def per_sequence(b, carry):
        r0 = pl.multiple_of(b * seq, seq)
        qb = q_ref[pl.ds(r0, seq), :]
        knb = kn_ref[pl.ds(r0, seq), :]
        vnb = vn_ref[pl.ds(r0, seq), :]
        mab = ma_ref[pl.ds(r0, seq), :]
        gbb = (gb_lo_ref[pl.ds(r0, seq), :], gb_hi_ref[pl.ds(r0, seq), :])
        groups = range(n_groups)
        gcols = lambda g: slice(g * HEAD_DIM, (g + 1) * HEAD_DIM)
        hcols = lambda g, hh: slice((g * KV_GROUP + hh) * HEAD_DIM, (g * KV_GROUP + hh + 1) * HEAD_DIM)
        q4 = [jnp.concatenate([qb[:, hcols(g, hh)] for hh in range(KV_GROUP)], axis=0) for g in groups]
        s_c = [_dot_nt(q4[g], ck_ref[b, :, gcols(g)]) for g in groups]
        s_n = [_dot_nt(q4[g], knb[:, gcols(g)]) for g in groups]
        p_c, p_n, den = [], [], []
        for g in groups:
            sc = jnp.where(cache_ok, s_c[g] * scale, -jnp.inf)
            sn = jnp.where(new_ok, s_n[g] * scale, -jnp.inf)
            m = jnp.maximum(jnp.maximum(jnp.max(sc, axis=-1, keepdims=True),
                                        jnp.max(sn, axis=-1, keepdims=True)), sinks[g])
            pc, pn = jnp.exp(sc - m), jnp.exp(sn - m)
            p_c.append(pc)
            p_n.append(pn)
            den.append(jnp.sum(pc, axis=-1, keepdims=True) + jnp.sum(pn, axis=-1, keepdims=True)
                       + jnp.exp(sinks[g] - m))
        o4 = [(_dot(p_c[g], cv_ref[b, :, gcols(g)]) + _dot(p_n[g], vnb[:, gcols(g)])) / den[g] for g in groups]
        for g in groups:
            for hh in range(KV_GROUP):
                cs = hcols(g, hh)
                gate = gbb[cs.start // half][:, cs.start % half:cs.start % half + HEAD_DIM]
                o_ref[pl.ds(r0, seq), cs] = _merge(mab[:, cs], gate, o4[g][hh * seq:(hh + 1) * seq])
        ok_ref[b, 0:w_buf - seq, :] = ck_ref[b, seq:w_buf, :]
        ok_ref[b, w_buf - seq:w_buf, :] = knb
        ov_ref[b, 0:w_buf - seq, :] = cv_ref[b, seq:w_buf, :]
        ov_ref[b, w_buf - seq:w_buf, :] = vnb
        return carry

    lax.fori_loop(0, n_b, per_sequence, 0)


def _swa_sample(qh, kh, proj, cache_k, cache_v, merged_a, sinks, *, row0, seq, v_col0, gb_col0, nb=8):
    n_seq, w_buf, kv_cols = cache_k.shape
    q_cols = qh.shape[1]
    half = q_cols // 2
    tr = nb * seq
    blk0 = row0 // tr
    cache_spec = pl.BlockSpec((nb, w_buf, kv_cols), lambda i: (i, 0, 0))
    return pl.pallas_call(
        functools.partial(_swa_sample_kernel, seq=seq),
        grid=(n_seq // nb,),
        in_specs=[
            pl.BlockSpec(memory_space=pltpu.SMEM),
            pl.BlockSpec((tr, q_cols), lambda i: (blk0 + i, 0)),
            pl.BlockSpec((tr, kv_cols), lambda i: (blk0 + i, 0)),
            pl.BlockSpec((tr, kv_cols), lambda i: (blk0 + i, v_col0 // kv_cols)),
            cache_spec, cache_spec,
            pl.BlockSpec((tr, q_cols), lambda i: (i, 0)),
            pl.BlockSpec((tr, half), lambda i: (blk0 + i, gb_col0 // half)),
            pl.BlockSpec((tr, half), lambda i: (blk0 + i, gb_col0 // half + 1)),
        ],
        out_specs=[pl.BlockSpec((tr, q_cols), lambda i: (i, 0)), cache_spec, cache_spec],
        out_shape=[jax.ShapeDtypeStruct((n_seq * seq, q_cols), BF16),
                   jax.ShapeDtypeStruct(cache_k.shape, F32),
                   jax.ShapeDtypeStruct(cache_v.shape, F32)],
        compiler_params=_cparams("parallel"),
        name="swa_sample",
    )(sinks, qh, kh, proj, cache_k, cache_v, merged_a, proj, proj)


def _outproj_kernel(x_ref, mp_ref, ms_ref, w_ref, o_ref, *, n_prompt_tiles):
    i = pl.program_id(0)

    @pl.when(i < n_prompt_tiles)
    def _():
        o_ref[...] = x_ref[...] + jnp.dot(mp_ref[...], w_ref[...], preferred_element_type=F32)

    @pl.when(i >= n_prompt_tiles)
    def _():
        o_ref[...] = x_ref[...] + jnp.dot(ms_ref[...], w_ref[...], preferred_element_type=F32)


def _outproj(x, merged_p, merged_s, w_out, *, tm=512, tn=1024):
    m, d = x.shape
    npt = merged_p.shape[0] // tm
    return pl.pallas_call(
        functools.partial(_outproj_kernel, n_prompt_tiles=npt),
        grid=(m // tm, d // tn),
        in_specs=[
            pl.BlockSpec((tm, tn), lambda i, j: (i, j)),
            pl.BlockSpec((tm, d), lambda i, j: (jnp.minimum(i, npt - 1), 0)),
            pl.BlockSpec((tm, d), lambda i, j: (jnp.maximum(i - npt, 0), 0)),
            pl.BlockSpec((d, tn), lambda i, j: (0, j)),
        ],
        out_specs=pl.BlockSpec((tm, tn), lambda i, j: (i, j)),
        out_shape=jax.ShapeDtypeStruct((m, d), F32),
        compiler_params=_cparams("parallel", "arbitrary"),
        name="outproj",
    )(x, merged_p, merged_s, w_out)


def _ple_kernel(xf_ref, x_ref, pe_ref, nw_ref, wp_ref, wg_ref, o_ref, xn_ref):
    @pl.when(pl.program_id(1) == 0)
    def _():
        xn_ref[...] = _rms(xf_ref[...], nw_ref[...]).astype(BF16)

    gate = jnp.dot(xn_ref[...], wg_ref[...], preferred_element_type=F32)
    emb = jnp.dot(pe_ref[...], wp_ref[...], preferred_element_type=F32)
    o_ref[...] = x_ref[...] + emb * jax.nn.sigmoid(gate)


def _ple(x, pe, norm_w, w_proj, w_gate, *, tm=512, tn=1024):
    m, d = x.shape
    pdim = pe.shape[1]
    return pl.pallas_call(
        _ple_kernel,
        grid=(m // tm, d // tn),
        in_specs=[
            pl.BlockSpec((tm, d), lambda i, j: (i, 0), pipeline_mode=pl.Buffered(1)),
            pl.BlockSpec((tm, tn), lambda i, j: (i, j)),
            pl.BlockSpec((tm, pdim), lambda i, j: (i, 0)),
            pl.BlockSpec((1, d), lambda i, j: (0, 0)),
            pl.BlockSpec((pdim, tn), lambda i, j: (0, j)),
            pl.BlockSpec((d, tn), lambda i, j: (0, j)),
        ],
        out_specs=pl.BlockSpec((tm, tn), lambda i, j: (i, j)),
        out_shape=jax.ShapeDtypeStruct((m, d), F32),
        scratch_shapes=[pltpu.VMEM((tm, d), BF16)],
        compiler_params=_cparams("parallel", "arbitrary"),
        name="ple",
    )(x, x, pe, norm_w.reshape(1, d), w_proj, w_gate)


def _rope_tables(seq, dec_batch, dec_seq):
    half = HEAD_DIM // 2
    inv_freq = ROPE_THETA ** (-jnp.arange(half, dtype=F32) / half)
    pos_p = jnp.arange(seq, dtype=jnp.int32)
    pos_s = jnp.tile(PAST_LEN + jnp.arange(dec_seq, dtype=jnp.int32), dec_batch)
    ang = jnp.concatenate([pos_p, pos_s]).astype(F32)[:, None] * inv_freq[None, :]
    cos, sin = jnp.cos(ang), jnp.sin(ang)
    return jnp.concatenate([cos, cos], axis=1), jnp.concatenate([-sin, sin], axis=1)


def kernel(x_prompt, x_sample, p_prompt, p_sample, state_gdn, state_conv, cache_swa_k, cache_swa_v, ffn1_norm, ffn1_w_gate, ffn1_w_up, ffn1_w_down, mix_norm, w_in, conv_w, A_log, dt_bias, gdn_norm_w, q_norm_w, k_norm_w, sinks, w_out, ffn2_norm, ffn2_w_gate, ffn2_w_up, ffn2_w_down, ple_norm, w_ple_proj, w_ple_gate):
    depth, dec_batch, n_v_heads, dk, dv = state_gdn.shape
    assert depth == 1 and dk == HEAD_DIM and dv == HEAD_DIM
    bp, seq, d = x_prompt.shape
    assert bp == 1
    dec_seq = x_sample.shape[1]
    w_buf, n_kv_heads = cache_swa_k.shape[2], cache_swa_k.shape[3]
    n_heads = sinks.shape[1]
    assert n_heads == n_kv_heads * KV_GROUP and w_buf == WINDOW and CHUNK % dec_seq == 0
    conv_dim = conv_w.shape[2]
    key_dim = n_v_heads // GDN_REP * HEAD_DIM
    val_dim = n_v_heads * HEAD_DIM
    q_dim = n_heads * HEAD_DIM
    kv_cols = n_kv_heads * HEAD_DIM
    n_samp = dec_batch * dec_seq

    ab0 = conv_dim + val_dim
    b0 = ab0 + 2 * n_v_heads
    assert w_in.shape[2] == b0 + q_dim + 2 * kv_cols + 2 * d
    win = w_in[0]
    w_a = win[:, :ab0].astype(BF16)
    w_b = win[:, b0:].astype(BF16)
    n_ab = 128
    w_ab = jnp.pad(win[:, ab0:b0], ((0, 0), (0, n_ab - 2 * n_v_heads))).astype(BF16)
    z_col0 = conv_dim
    q_col0, k_col0, v_col0 = 0, q_dim, q_dim + kv_cols
    ga_col0 = q_dim + 2 * kv_cols
    gb_col0 = ga_col0 + d

    bf = lambda w: w[0].astype(BF16)
    x = jnp.concatenate([x_prompt.reshape(seq, d), x_sample.reshape(n_samp, d)], axis=0)
    pe = jnp.concatenate([p_prompt.reshape(seq, PLE_DIM), p_sample.reshape(n_samp, PLE_DIM)], axis=0)

    x = _ffn(x, ffn1_norm[0], bf(ffn1_w_gate), bf(ffn1_w_up), bf(ffn1_w_down))
    proj_a, ab = _inproj(x, mix_norm[0], w_a, w_ab)
    (proj_b,) = _inproj(x, mix_norm[0], w_b)

    carried = jnp.pad(state_conv[0], ((0, 0), (dec_seq - (CONV_W - 1), 0), (0, 0))).reshape(n_samp, conv_dim)
    act_p = _conv_prompt(proj_a, conv_w[0], rows=seq, conv_dim=conv_dim, key_dim=key_dim)
    act_s = _conv_sample(proj_a, carried, conv_w[0], row0=seq, rows=n_samp, seq=dec_seq,
                         conv_dim=conv_dim, key_dim=key_dim)
    ab_t = ab.T
    gdn_args = dict(n_heads=n_v_heads, z_col0=z_col0, ga_col0=ga_col0)
    ma_p, sg_p = _gdn_prompt(act_p, proj_a, proj_b, ab, ab_t, A_log[0], dt_bias[0], gdn_norm_w[0],
                             rows=seq, **gdn_args)
    ma_s, sg_s = _gdn_sample(act_s, proj_a, proj_b, ab, ab_t, A_log[0], dt_bias[0], gdn_norm_w[0],
                             state_gdn[0], row0=seq, rows=n_samp, seq=dec_seq, **gdn_args)

    cos, sin = _rope_tables(seq, dec_batch, dec_seq)
    qh, kh = _rope(proj_b, cos, sin, q_norm_w[0], k_norm_w[0], q_col0=q_col0, q_cols=q_dim,
                   k_col0=k_col0, k_cols=kv_cols)
    merged_p = _swa_prompt(qh, kh, proj_b, ma_p, sinks[0], rows=seq, n_kv_heads=n_kv_heads,
                           v_col0=v_col0, gb_col0=gb_col0)
    merged_s, kk_s, vv_s = _swa_sample(qh, kh, proj_b, cache_swa_k[0].reshape(dec_batch, w_buf, kv_cols),
                                       cache_swa_v[0].reshape(dec_batch, w_buf, kv_cols),
                                       ma_s.astype(F32), sinks[0], row0=seq, seq=dec_seq,
                                       v_col0=v_col0, gb_col0=gb_col0)

    x = _outproj(x, merged_p, merged_s, bf(w_out))
    x = _ffn(x, ffn2_norm[0], bf(ffn2_w_gate), bf(ffn2_w_up), bf(ffn2_w_down))
    x = _ple(x, pe.astype(BF16), ple_norm[0], bf(w_ple_proj), bf(w_ple_gate))

    cache_shape = (1, 1, w_buf, n_kv_heads, HEAD_DIM)
    return (
        x[:seq].reshape(1, seq, d),
        x[seq:].reshape(dec_batch, dec_seq, d),
        sg_p.reshape(1, 1, n_v_heads, HEAD_DIM, HEAD_DIM),
        proj_a[seq - (CONV_W - 1):seq, :conv_dim].reshape(1, 1, CONV_W - 1, conv_dim),
        kh[seq - w_buf:seq].reshape(cache_shape),
        proj_b[seq - w_buf:seq, v_col0:v_col0 + kv_cols].reshape(cache_shape),
        sg_s.reshape(1, dec_batch, n_v_heads, HEAD_DIM, HEAD_DIM),
        proj_a[seq:, :conv_dim].reshape(dec_batch, dec_seq, conv_dim)[:, dec_seq - (CONV_W - 1):].reshape(
            1, dec_batch, CONV_W - 1, conv_dim),
        kk_s.reshape(1, dec_batch, w_buf, n_kv_heads, HEAD_DIM),
        vv_s.reshape(1, dec_batch, w_buf, n_kv_heads, HEAD_DIM),
    )
```

```python
import functools

import jax
import jax.numpy as jnp
from jax import lax
from jax.experimental import pallas as pl
from jax.experimental.pallas import tpu as pltpu

F32 = jnp.float32
BF16 = jnp.bfloat16

EPS = 1e-6
HEAD_DIM = 128
KV_GROUP = 4
WINDOW = 128
ROPE_THETA = 10000.0
PAST_LEN = 8192
GDN_REP = 2
CONV_W = 4
CHUNK = 64
PLE_DIM = 256
GDN_STEP_HEADS = 4
FFN_OUT_CHUNK = 1024
SWA_STEP_GROUPS = 4

V7X_VMEM_LIMIT_BYTES = 62 * 1024 * 1024


def _cparams(*sem):
    return pltpu.CompilerParams(dimension_semantics=sem, vmem_limit_bytes=V7X_VMEM_LIMIT_BYTES)


def _rms(x, w):
    return x * lax.rsqrt(jnp.mean(x * x, axis=-1, keepdims=True) + EPS) * w


def _dot(a, b):
    return jnp.dot(a.astype(BF16), b.astype(BF16), preferred_element_type=F32)


def _dot_nt(a, b):
    return lax.dot_general(a.astype(BF16), b.astype(BF16), (((1,), (1,)), ((), ())),
                           preferred_element_type=F32)


def _dot_tn(a, b):
    return lax.dot_general(a.astype(BF16), b.astype(BF16), (((0,), (0,)), ((), ())),
                           preferred_element_type=F32)


def _row_part_specs(parts, tm, width):
    specs, spans, start = [], [], 0
    for part in parts:
        n = part.shape[0] // tm
        specs.append(pl.BlockSpec(
            (tm, width), functools.partial(lambda i, j, s, n: (jnp.clip(i - s, 0, n - 1), 0), s=start, n=n),
            pipeline_mode=pl.Buffered(1)))
        spans.append((start, n))
        start += n
    return specs, spans


def _ffn_kernel(*refs, spans):
    n_parts = len(spans)
    x_refs = refs[:n_parts]
    nw_ref, wg_ref, wu_ref, wd_ref, o_ref, xn_ref = refs[n_parts:]
    i = pl.program_id(0)
    j = pl.program_id(1)
    for x_ref, (start, n) in zip(x_refs, spans):
        @pl.when((j == 0) & (i >= start) & (i < start + n))
        def _():
            x = x_ref[...]
            xn_ref[...] = _rms(x, nw_ref[...]).astype(BF16)
            o_ref[...] = x

    xn = xn_ref[...]
    g = jnp.dot(xn, wg_ref[...], preferred_element_type=F32)
    u = jnp.dot(xn, wu_ref[...], preferred_element_type=F32)
    h = ((0.5 * (g * jax.nn.sigmoid(g))) * u).astype(BF16)
    d = o_ref.shape[1]
    for c in range(0, d, FFN_OUT_CHUNK):
        cs = slice(c, min(c + FFN_OUT_CHUNK, d))
        o_ref[:, cs] += jnp.dot(h, wd_ref[:, cs], preferred_element_type=F32)


def _ffn(x_parts, norm_w, wg, wu, wd, *, tm=512, tf=512, single_buffer_out=False):
    d = x_parts[0].shape[1]
    m = sum(p.shape[0] for p in x_parts)
    f = wg.shape[1]
    x_specs, spans = _row_part_specs(x_parts, tm, d)
    out_mode = dict(pipeline_mode=pl.Buffered(1)) if single_buffer_out else {}
    return pl.pallas_call(
        functools.partial(_ffn_kernel, spans=spans),
        grid=(m // tm, f // tf),
        in_specs=[
            *x_specs,
            pl.BlockSpec((1, d), lambda i, j: (0, 0)),
            pl.BlockSpec((d, tf), lambda i, j: (0, j)),
            pl.BlockSpec((d, tf), lambda i, j: (0, j)),
            pl.BlockSpec((tf, d), lambda i, j: (j, 0)),
        ],
        out_specs=pl.BlockSpec((tm, d), lambda i, j: (i, 0), **out_mode),
        out_shape=jax.ShapeDtypeStruct((m, d), F32),
        scratch_shapes=[pltpu.VMEM((tm, d), BF16)],
        compiler_params=_cparams("parallel", "arbitrary"),
        name="ffn",
    )(*x_parts, norm_w.reshape(1, d), wg, wu, wd)


def _inproj_kernel(x_ref, nw_ref, w_ref, o_ref, xn_ref):
    @pl.when(pl.program_id(1) == 0)
    def _():
        xn_ref[...] = _rms(x_ref[...], nw_ref[...]).astype(BF16)

    o_ref[...] = jnp.dot(xn_ref[...], w_ref[...], preferred_element_type=F32)


def _inproj_ab_kernel(x_ref, nw_ref, w_ref, wab_ref, o_ref, ab_ref, xn_ref):
    @pl.when(pl.program_id(1) == 0)
    def _():
        xn = _rms(x_ref[...], nw_ref[...]).astype(BF16)
        xn_ref[...] = xn
        ab_ref[...] = jnp.dot(xn, wab_ref[...], preferred_element_type=F32)

    o_ref[...] = jnp.dot(xn_ref[...], w_ref[...], preferred_element_type=F32)


def _inproj(x, norm_w, w, w_ab=None, *, n=None, tm=512, tn=1024):
    m, d = x.shape
    n = w.shape[1] if n is None else n
    in_specs = [
        pl.BlockSpec((tm, d), lambda i, j: (i, 0), pipeline_mode=pl.Buffered(1)),
        pl.BlockSpec((1, d), lambda i, j: (0, 0)),
        pl.BlockSpec((d, tn), lambda i, j: (0, j)),
    ]
    out_specs = [pl.BlockSpec((tm, tn), lambda i, j: (i, j))]
    out_shape = [jax.ShapeDtypeStruct((m, n), F32)]
    args = [x, norm_w.reshape(1, d), w]
    body = _inproj_kernel
    if w_ab is not None:
        nab = w_ab.shape[1]
        in_specs.append(pl.BlockSpec((d, nab), lambda i, j: (0, 0)))
        out_specs.append(pl.BlockSpec((tm, nab), lambda i, j: (i, 0)))
        out_shape.append(jax.ShapeDtypeStruct((m, nab), F32))
        args.append(w_ab)
        body = _inproj_ab_kernel
    return pl.pallas_call(
        body,
        grid=(m // tm, n // tn),
        in_specs=in_specs,
        out_specs=out_specs,
        out_shape=out_shape,
        scratch_shapes=[pltpu.VMEM((tm, d), BF16)],
        compiler_params=_cparams("parallel", "arbitrary"),
        name="inproj",
    )(*args)


def _conv_finish(y, j, o_ref, *, n_q_blocks, n_qk_blocks, q_scale):
    y = y * jax.nn.sigmoid(y)
    cw = y.shape[1]

    @pl.when(j < n_qk_blocks)
    def _():
        scale = jnp.where(j < n_q_blocks, q_scale, 1.0)
        for h in range(cw // HEAD_DIM):
            yh = y[:, h * HEAD_DIM:(h + 1) * HEAD_DIM]
            nrm = lax.rsqrt(jnp.sum(yh * yh, axis=-1, keepdims=True) + EPS) * scale
            o_ref[:, h * HEAD_DIM:(h + 1) * HEAD_DIM] = yh * nrm

    @pl.when(j >= n_qk_blocks)
    def _():
        o_ref[...] = y


def _conv_prompt_kernel(x_ref, halo_ref, cw_ref, o_ref, **kw):
    i = pl.program_id(0)
    j = pl.program_id(1)
    x = x_ref[...]
    halo = jnp.where(i > 0, halo_ref[...], 0.0)
    w = cw_ref[...]
    row8 = lax.broadcasted_iota(jnp.int32, (8, 1), 0)
    y = None
    for tap in range(CONV_W):
        sh = CONV_W - 1 - tap
        if sh == 0:
            xs = x
        else:
            rolled = pltpu.roll(x, sh, 0)
            head = jnp.where(row8 < sh, pltpu.roll(halo, sh, 0), rolled[:8])
            xs = jnp.concatenate([head, rolled[8:]], axis=0)
        term = xs * w[tap:tap + 1]
        y = term if y is None else y + term
    _conv_finish(y, j, o_ref, **kw)


def _conv_sample_kernel(x_ref, e_ref, cw_ref, o_ref, *, seq, **kw):
    j = pl.program_id(1)
    x = x_ref[...]
    e = e_ref[...]
    w = cw_ref[...]
    rows = x.shape[0]
    t = lax.broadcasted_iota(jnp.int32, (rows, 1), 0) % seq
    y = None
    for tap in range(CONV_W):
        sh = CONV_W - 1 - tap
        if sh == 0:
            xs = x
        else:
            xs = jnp.where(t >= sh, pltpu.roll(x, sh, 0), pltpu.roll(e, rows + sh - seq, 0))
        term = xs * w[tap:tap + 1]
        y = term if y is None else y + term
    _conv_finish(y, j, o_ref, **kw)


def _conv_common(n_q_cols, n_qk_cols, cw):
    return dict(n_q_blocks=n_q_cols // cw, n_qk_blocks=n_qk_cols // cw, q_scale=HEAD_DIM ** -0.5)


def _conv_prompt(proj, conv_w, *, rows, conv_dim, key_dim, tc=512, cw=512):
    tc8 = tc // 8
    return pl.pallas_call(
        functools.partial(_conv_prompt_kernel, **_conv_common(key_dim, 2 * key_dim, cw)),
        grid=(rows // tc, conv_dim // cw),
        in_specs=[
            pl.BlockSpec((tc, cw), lambda i, j: (i, j)),
            pl.BlockSpec((8, cw), lambda i, j: (jnp.maximum(i * tc8 - 1, 0), j)),
            pl.BlockSpec((CONV_W, cw), lambda i, j: (0, j)),
        ],
        out_specs=pl.BlockSpec((tc, cw), lambda i, j: (i, j)),
        out_shape=jax.ShapeDtypeStruct((rows, conv_dim), F32),
        compiler_params=_cparams("parallel", "parallel"),
        name="conv_prompt",
    )(proj, proj, conv_w)


def _conv_sample(proj, e, conv_w, *, row0, rows, seq, conv_dim, key_dim, tc=512, cw=512):
    blk0 = row0 // tc
    return pl.pallas_call(
        functools.partial(_conv_sample_kernel, seq=seq, **_conv_common(key_dim, 2 * key_dim, cw)),
        grid=(rows // tc, conv_dim // cw),
        in_specs=[
            pl.BlockSpec((tc, cw), lambda i, j: (blk0 + i, j)),
            pl.BlockSpec((tc, cw), lambda i, j: (i, j)),
            pl.BlockSpec((CONV_W, cw), lambda i, j: (0, j)),
        ],
        out_specs=pl.BlockSpec((tc, cw), lambda i, j: (i, j)),
        out_shape=jax.ShapeDtypeStruct((rows, conv_dim), F32),
        compiler_params=_cparams("parallel", "parallel"),
        name="conv_sample",
    )(proj, e, conv_w)


def _softplus(x):
    return jnp.maximum(x, 0.0) + jnp.log1p(jnp.exp(-jnp.abs(x)))


def _seg_cumsum(x, pos, seg, axis):
    d = 1
    while d < seg:
        x = x + jnp.where(pos >= d, pltpu.roll(x, d, axis), 0.0)
        d *= 2
    return x


def _gates(ab, abt8, alog_row, dtb_row, alog_col8, dtb_col8, seg):
    rows = ab.shape[0]
    pos_c = lax.broadcasted_iota(jnp.int32, (rows, 1), 0) % seg
    pos_r = lax.broadcasted_iota(jnp.int32, (1, rows), 1) % seg
    g_all = -jnp.exp(alog_row) * _softplus(ab + dtb_row)
    g_t = -jnp.exp(alog_col8) * _softplus(abt8 + dtb_col8)
    return _seg_cumsum(g_all, pos_c, seg, 0), jax.nn.sigmoid(ab), _seg_cumsum(g_t, pos_r, seg, 1)


def _pick_lane(x, idx):
    lane = lax.broadcasted_iota(jnp.int32, (1, x.shape[1]), 1)
    return jnp.sum(jnp.where(lane == idx, x, 0.0), axis=1, keepdims=True)


def _pick_sublane(x, idx):
    sub = lax.broadcasted_iota(jnp.int32, (x.shape[0], 1), 0)
    return jnp.sum(jnp.where(sub == idx, x, 0.0), axis=0, keepdims=True)


def _head_gates(gate_vals, h, n_heads):
    gc_all, sig_all, gc_t = gate_vals
    return _pick_lane(sig_all, n_heads + h), _pick_lane(gc_all, h), _pick_sublane(gc_t, h % 8)


def _load_gates(ab_ref, abt_ref, alr_ref, dtr_ref, alc_ref, dtc_ref, h0, seg):
    h8 = pl.multiple_of((h0 // 8) * 8, 8)
    return _gates(ab_ref[...], abt_ref[pl.ds(h8, 8), :], alr_ref[...], dtr_ref[...],
                  alc_ref[pl.ds(h8, 8), :], dtc_ref[pl.ds(h8, 8), :], seg)


def _chunk_masks(seg):
    r = lax.broadcasted_iota(jnp.int32, (CHUNK, CHUNK), 0)
    c = lax.broadcasted_iota(jnp.int32, (CHUNK, CHUNK), 1)
    same = (r // seg) == (c // seg)
    return same & (c <= r), same & (c < r), r == c


def _gdn_intra(q_ref, k_ref, v_ref, gates, seg):
    incl, strict, eye = _chunk_masks(seg)
    n_chunks = q_ref.shape[0] // CHUNK
    n_heads = len(gates)
    chunks = range(n_chunks)
    heads = range(n_heads)
    items = [(n, hh) for n in chunks for hh in heads]
    rows = lambda n: slice(n * CHUNK, (n + 1) * CHUNK)
    cols = lambda j: slice(j * HEAD_DIM, (j + 1) * HEAD_DIM)
    pairs = [(n, j) for n in chunks for j in range(n_heads // GDN_REP)]
    q = {(n, j): q_ref[rows(n), cols(j)] for n, j in pairs}
    k = {(n, j): k_ref[rows(n), cols(j)] for n, j in pairs}
    kk_qk = {p: _dot_nt(jnp.concatenate([k[p], q[p]], axis=0), k[p]) for p in pairs}
    kk = {p: kk_qk[p][:CHUNK] for p in pairs}
    qk = {p: kk_qk[p][CHUNK:] for p in pairs}
    out = {}
    low, rhs = {}, {}
    for n, hh in items:
        beta, gc_col, gc_row = gates[hh]
        beta, gc_col, gc_row = beta[rows(n)], gc_col[rows(n)], gc_row[:, rows(n)]
        pj = (n, hh // GDN_REP)
        decay = jnp.exp(jnp.where(incl, gc_col - gc_row, -jnp.inf))
        low[n, hh] = jnp.where(strict, beta * kk[pj] * decay, 0.0)
        e_gc = jnp.exp(gc_col)
        rhs[n, hh] = jnp.concatenate([v_ref[rows(n), cols(hh)] * beta, k[pj] * (beta * e_gc)], axis=1)
        out[n, hh] = dict(a=jnp.where(incl, qk[pj] * decay, 0.0), qd=q[pj] * e_gc, gc=gc_col, k=k[pj])
    p = {it: -low[it] for it in items}
    t = {it: jnp.where(eye, 1.0, p[it]) for it in items}
    if seg > 2:
        p = {it: _dot(p[it], p[it]) for it in items}
        n_pow = 4
        while n_pow < seg:
            both = {it: _dot(jnp.concatenate([t[it], p[it]], axis=0), p[it]) for it in items}
            t = {it: t[it] + both[it][:CHUNK] for it in items}
            p = {it: both[it][CHUNK:] for it in items}
            n_pow *= 2
        t = {it: t[it] + _dot(t[it], p[it]) for it in items}
    for it in items:
        uw = _dot(t[it], rhs[it])
        out[it]["u"] = uw[:, :HEAD_DIM]
        out[it]["w"] = uw[:, HEAD_DIM:]
    return out


def _gdn_emit(o, z, gate, nw):
    return (_rms(o, nw) * (z * jax.nn.sigmoid(z)) * jax.nn.sigmoid(gate)).astype(BF16)


def _gdn_prompt_kernel(q_ref, k_ref, v_ref, z_ref, ga_ref, ab_ref, abt_ref, alr_ref, dtr_ref, alc_ref,
                       dtc_ref, nw_ref, o_ref, sfin_ref, s_ref, *, n_heads):
    h0 = pl.program_id(0) * GDN_STEP_HEADS
    i = pl.program_id(1)

    @pl.when(i == 0)
    def _():
        s_ref[...] = jnp.zeros_like(s_ref)

    nw = nw_ref[...]
    heads = range(GDN_STEP_HEADS)
    gate_vals = _load_gates(ab_ref, abt_ref, alr_ref, dtr_ref, alc_ref, dtc_ref, h0, CHUNK)
    gates = [_head_gates(gate_vals, h0 + hh, n_heads) for hh in heads]
    intra = _gdn_intra(q_ref, k_ref, v_ref, gates, CHUNK)
    state = [s_ref[hh] for hh in heads]
    for n in range(q_ref.shape[0] // CHUNK):
        rows = slice(n * CHUNK, (n + 1) * CHUNK)
        c = [intra[n, hh] for hh in heads]
        ws_qs = [_dot(jnp.concatenate([c[hh]["w"], c[hh]["qd"]], axis=0), state[hh]) for hh in heads]
        v_new = [c[hh]["u"] - ws_qs[hh][:CHUNK] for hh in heads]
        g_last = [c[hh]["gc"][CHUNK - 1:CHUNK] for hh in heads]
        k_st = [(c[hh]["k"] * jnp.exp(g_last[hh] - c[hh]["gc"])).T for hh in heads]
        av_kv = [_dot(jnp.concatenate([c[hh]["a"], k_st[hh]], axis=0), v_new[hh]) for hh in heads]
        o = [ws_qs[hh][CHUNK:] + av_kv[hh][:CHUNK] for hh in heads]
        state = [state[hh] * jnp.exp(g_last[hh]) + av_kv[hh][CHUNK:] for hh in heads]
        for hh in heads:
            cs = slice(hh * HEAD_DIM, (hh + 1) * HEAD_DIM)
            o_ref[rows, cs] = _gdn_emit(o[hh], z_ref[rows, cs], ga_ref[rows, cs], nw)
    for hh in heads:
        s_ref[hh] = state[hh]

    @pl.when(i == pl.num_programs(1) - 1)
    def _():
        sfin_ref[...] = s_ref[...]


def _gate_param_specs(n_lanes, idx):
    row = pl.BlockSpec((1, n_lanes), idx)
    col = pl.BlockSpec((n_lanes, 1), idx)
    return [row, row, col, col]


def _gate_params(alog, dtb, n_lanes):
    pad = lambda v: jnp.pad(v.astype(F32), (0, n_lanes - v.shape[0]))
    al, dt = pad(alog), pad(dtb)
    return al.reshape(1, n_lanes), dt.reshape(1, n_lanes), al.reshape(n_lanes, 1), dt.reshape(n_lanes, 1)


def _gdn_prompt(act, proj_a, proj_b, ab, ab_t, alog, dtb, norm_w, *, rows, n_heads, z_col0, ga_col0, tc=512):
    hs = GDN_STEP_HEADS
    qw = hs // GDN_REP * HEAD_DIM
    vw = hs * HEAD_DIM
    key_dim = n_heads // GDN_REP * HEAD_DIM
    n_lanes = ab.shape[1]
    return pl.pallas_call(
        functools.partial(_gdn_prompt_kernel, n_heads=n_heads),
        grid=(n_heads // hs, rows // tc),
        in_specs=[
            pl.BlockSpec((tc, qw), lambda h, i: (i, h)),
            pl.BlockSpec((tc, qw), lambda h, i: (i, key_dim // qw + h)),
            pl.BlockSpec((tc, vw), lambda h, i: (i, 2 * key_dim // vw + h)),
            pl.BlockSpec((tc, vw), lambda h, i: (i, z_col0 // vw + h)),
            pl.BlockSpec((tc, vw), lambda h, i: (i, ga_col0 // vw + h)),
            pl.BlockSpec((tc, n_lanes), lambda h, i: (i, 0)),
            pl.BlockSpec((n_lanes, tc), lambda h, i: (0, i)),
            *_gate_param_specs(n_lanes, lambda h, i: (0, 0)),
            pl.BlockSpec((1, HEAD_DIM), lambda h, i: (0, 0)),
        ],
        out_specs=[
            pl.BlockSpec((tc, vw), lambda h, i: (i, h)),
            pl.BlockSpec((hs, HEAD_DIM, HEAD_DIM), lambda h, i: (h, 0, 0)),
        ],
        out_shape=[jax.ShapeDtypeStruct((rows, n_heads * HEAD_DIM), BF16),
                   jax.ShapeDtypeStruct((n_heads, HEAD_DIM, HEAD_DIM), F32)],
        scratch_shapes=[pltpu.VMEM((hs, HEAD_DIM, HEAD_DIM), F32)],
        compiler_params=_cparams("parallel", "arbitrary"),
        name="gdn_prompt",
    )(act, act, act, proj_a, proj_b, ab, ab_t, *_gate_params(alog, dtb, n_lanes), norm_w.reshape(1, HEAD_DIM))


def _gdn_sample_kernel(q_ref, k_ref, v_ref, z_ref, ga_ref, ab_ref, abt_ref, alr_ref, dtr_ref, alc_ref,
                       dtc_ref, nw_ref, s0_ref, o_ref, s1_ref, *, n_heads, seq):
    h0 = pl.program_id(1) * GDN_STEP_HEADS
    nw = nw_ref[...]
    heads = range(GDN_STEP_HEADS)
    gate_vals = _load_gates(ab_ref, abt_ref, alr_ref, dtr_ref, alc_ref, dtc_ref, h0, seq)
    gates = [_head_gates(gate_vals, h0 + hh, n_heads) for hh in heads]
    intra = _gdn_intra(q_ref, k_ref, v_ref, gates, seq)
    per_chunk = CHUNK // seq
    for n in range(q_ref.shape[0] // CHUNK):
        rows = slice(n * CHUNK, (n + 1) * CHUNK)
        seqs = [(hh, b) for hh in heads for b in range(per_chunk)]
        sub = lambda b: slice(b * seq, (b + 1) * seq)
        c = [intra[n, hh] for hh in heads]
        s0 = {(hh, b): s0_ref[n * per_chunk + b, hh] for hh, b in seqs}
        ws_qs = {(hh, b): _dot(jnp.concatenate([c[hh]["w"][sub(b)], c[hh]["qd"][sub(b)]], axis=0), s0[hh, b])
                 for hh, b in seqs}
        v_new = {(hh, b): c[hh]["u"][sub(b)] - ws_qs[hh, b][:seq] for hh, b in seqs}
        for hh, b in seqs:
            gc = c[hh]["gc"][sub(b)]
            g_last = gc[seq - 1:seq]
            k_st = c[hh]["k"][sub(b)] * jnp.exp(g_last - gc)
            s1_ref[n * per_chunk + b, hh] = s0[hh, b] * jnp.exp(g_last) + _dot_tn(k_st, v_new[hh, b])
        for hh in heads:
            cs = slice(hh * HEAD_DIM, (hh + 1) * HEAD_DIM)
            qs = jnp.concatenate([ws_qs[hh, b][seq:] for b in range(per_chunk)], axis=0)
            vn = jnp.concatenate([v_new[hh, b] for b in range(per_chunk)], axis=0)
            out = _gdn_emit(qs + _dot(c[hh]["a"], vn), z_ref[rows, cs], ga_ref[rows, cs], nw)
            o_ref[rows, cs] = out.astype(o_ref.dtype)


def _gdn_sample(act, proj_a, proj_b, ab, ab_t, alog, dtb, norm_w, s0, *, row0, rows, seq, n_heads, z_col0,
                ga_col0, tc=128):
    hs = GDN_STEP_HEADS
    qw = hs // GDN_REP * HEAD_DIM
    vw = hs * HEAD_DIM
    key_dim = n_heads // GDN_REP * HEAD_DIM
    blk0 = row0 // tc
    n_lanes = ab.shape[1]
    state_spec = pl.BlockSpec((tc // seq, hs, HEAD_DIM, HEAD_DIM), lambda i, h: (i, h, 0, 0))
    return pl.pallas_call(
        functools.partial(_gdn_sample_kernel, n_heads=n_heads, seq=seq),
        grid=(rows // tc, n_heads // hs),
        in_specs=[
            pl.BlockSpec((tc, qw), lambda i, h: (i, h)),
            pl.BlockSpec((tc, qw), lambda i, h: (i, key_dim // qw + h)),
            pl.BlockSpec((tc, vw), lambda i, h: (i, 2 * key_dim // vw + h)),
            pl.BlockSpec((tc, vw), lambda i, h: (blk0 + i, z_col0 // vw + h)),
            pl.BlockSpec((tc, vw), lambda i, h: (blk0 + i, ga_col0 // vw + h)),
            pl.BlockSpec((tc, n_lanes), lambda i, h: (blk0 + i, 0)),
            pl.BlockSpec((n_lanes, tc), lambda i, h: (0, blk0 + i)),
            *_gate_param_specs(n_lanes, lambda i, h: (0, 0)),
            pl.BlockSpec((1, HEAD_DIM), lambda i, h: (0, 0)),
            state_spec,
        ],
        out_specs=[pl.BlockSpec((tc, vw), lambda i, h: (i, h)), state_spec],
        out_shape=[jax.ShapeDtypeStruct((rows, n_heads * HEAD_DIM), F32),
                   jax.ShapeDtypeStruct(s0.shape, F32)],
        compiler_params=_cparams("parallel", "parallel"),
        name="gdn_sample",
    )(act, act, act, proj_a, proj_b, ab, ab_t, *_gate_params(alog, dtb, n_lanes),
      norm_w.reshape(1, HEAD_DIM), s0)


def _rope_kernel(sq_ref, sk_ref, cos_ref, sin_ref, qw_ref, kw_ref, qo_ref, ko_ref):
    cos = cos_ref[...]
    sin = sin_ref[...]
    for src, w_ref, dst in ((sq_ref, qw_ref, qo_ref), (sk_ref, kw_ref, ko_ref)):
        w = w_ref[...]
        for h in range(src.shape[1] // HEAD_DIM):
            c0, c1 = h * HEAD_DIM, (h + 1) * HEAD_DIM
            y = _rms(src[:, c0:c1], w)
            dst[:, c0:c1] = y * cos + pltpu.roll(y, HEAD_DIM // 2, 1) * sin


def _rope(proj, cos, sin, q_norm_w, k_norm_w, *, q_col0, q_cols, k_col0, k_cols, tr=256):
    m = proj.shape[0]
    return pl.pallas_call(
        _rope_kernel,
        grid=(m // tr,),
        in_specs=[
            pl.BlockSpec((tr, q_cols), lambda i: (i, q_col0 // q_cols)),
            pl.BlockSpec((tr, k_cols), lambda i: (i, k_col0 // k_cols)),
            pl.BlockSpec((tr, HEAD_DIM), lambda i: (i, 0)),
            pl.BlockSpec((tr, HEAD_DIM), lambda i: (i, 0)),
            pl.BlockSpec((1, HEAD_DIM), lambda i: (0, 0)),
            pl.BlockSpec((1, HEAD_DIM), lambda i: (0, 0)),
        ],
        out_specs=[
            pl.BlockSpec((tr, q_cols), lambda i: (i, 0)),
            pl.BlockSpec((tr, k_cols), lambda i: (i, 0)),
        ],
        out_shape=[jax.ShapeDtypeStruct((m, q_cols), F32), jax.ShapeDtypeStruct((m, k_cols), F32)],
        compiler_params=_cparams("parallel"),
        name="rope",
    )(proj, proj, cos, sin, q_norm_w.reshape(1, HEAD_DIM), k_norm_w.reshape(1, HEAD_DIM))


def _merge(other, gate, o):
    return (other.astype(F32) + jax.nn.sigmoid(gate) * o).astype(BF16)


def _swa_prompt_kernel(sinks_ref, q_ref, kc_ref, kp_ref, vc_ref, vp_ref, ma_ref, gb_ref, o_ref):
    g0 = pl.program_id(0) * SWA_STEP_GROUPS
    i = pl.program_id(1)
    rows = KV_GROUP * WINDOW
    r = lax.broadcasted_iota(jnp.int32, (rows, 2 * WINDOW), 0) % WINDOW
    c = lax.broadcasted_iota(jnp.int32, (rows, 2 * WINDOW), 1)
    visible = ((c < WINDOW) & (c > r) & (i > 0)) | ((c >= WINDOW) & (c - WINDOW <= r))
    head = lax.broadcasted_iota(jnp.int32, (rows, 1), 0) // WINDOW
    scale = HEAD_DIM ** -0.5
    groups = range(SWA_STEP_GROUPS)
    hcols = lambda g, hh: slice((g * KV_GROUP + hh) * HEAD_DIM, (g * KV_GROUP + hh + 1) * HEAD_DIM)
    gcols = lambda g: slice(g * HEAD_DIM, (g + 1) * HEAD_DIM)
    q4 = [jnp.concatenate([q_ref[:, hcols(g, hh)] for hh in range(KV_GROUP)], axis=0) for g in groups]
    kcat = [jnp.concatenate([kp_ref[:, gcols(g)], kc_ref[:, gcols(g)]], axis=0) for g in groups]
    vcat = [jnp.concatenate([vp_ref[:, gcols(g)], vc_ref[:, gcols(g)]], axis=0) for g in groups]
    s = [_dot_nt(q4[g], kcat[g]) for g in groups]
    p, den = [], []
    for g in groups:
        sink = jnp.zeros((rows, 1), F32)
        for hh in range(KV_GROUP):
            sink = jnp.where(head == hh, sinks_ref[(g0 + g) * KV_GROUP + hh], sink)
        sg = jnp.where(visible, s[g] * scale, -jnp.inf)
        m = jnp.maximum(jnp.max(sg, axis=-1, keepdims=True), sink)
        pg = jnp.exp(sg - m)
        p.append(pg)
        den.append(jnp.sum(pg, axis=-1, keepdims=True) + jnp.exp(sink - m))
    o4 = [_dot(p[g], vcat[g]) / den[g] for g in groups]
    for g in groups:
        for hh in range(KV_GROUP):
            cs = hcols(g, hh)
            o_ref[:, cs] = _merge(ma_ref[:, cs], gb_ref[:, cs], o4[g][hh * WINDOW:(hh + 1) * WINDOW])


def _swa_prompt(qh, kh, proj, merged_a, sinks, *, rows, n_kv_heads, v_col0, gb_col0):
    sg = SWA_STEP_GROUPS
    qw = sg * KV_GROUP * HEAD_DIM
    kw = sg * HEAD_DIM
    prev = lambda i: jnp.maximum(i - 1, 0)
    return pl.pallas_call(
        _swa_prompt_kernel,
        grid=(n_kv_heads // sg, rows // WINDOW),
        in_specs=[
            pl.BlockSpec(memory_space=pltpu.SMEM),
            pl.BlockSpec((WINDOW, qw), lambda g, i: (i, g)),
            pl.BlockSpec((WINDOW, kw), lambda g, i: (i, g)),
            pl.BlockSpec((WINDOW, kw), lambda g, i: (prev(i), g)),
            pl.BlockSpec((WINDOW, kw), lambda g, i: (i, v_col0 // kw + g)),
            pl.BlockSpec((WINDOW, kw), lambda g, i: (prev(i), v_col0 // kw + g)),
            pl.BlockSpec((WINDOW, qw), lambda g, i: (i, g)),
            pl.BlockSpec((WINDOW, qw), lambda g, i: (i, gb_col0 // qw + g)),
        ],
        out_specs=pl.BlockSpec((WINDOW, qw), lambda g, i: (i, g)),
        out_shape=jax.ShapeDtypeStruct((rows, n_kv_heads * KV_GROUP * HEAD_DIM), BF16),
        compiler_params=_cparams("parallel", "parallel"),
        name="swa_prompt",
    )(sinks, qh, kh, kh, proj, proj, merged_a, proj)


def _swa_sample_kernel(sinks_ref, q_ref, kn_ref, vn_ref, ck_ref, cv_ref, ma_ref, gb_lo_ref, gb_hi_ref,
                       o_ref, ok_ref, ov_ref, *, seq):
    n_b, w_buf, kv_cols = ck_ref.shape
    half = gb_lo_ref.shape[1]
    rows = KV_GROUP * seq
    t = lax.broadcasted_iota(jnp.int32, (rows, 1), 0) % seq
    head = lax.broadcasted_iota(jnp.int32, (rows, 1), 0) // seq
    jc = lax.broadcasted_iota(jnp.int32, (1, w_buf), 1)
    jn = lax.broadcasted_iota(jnp.int32, (1, seq), 1)
    dist_c = t + w_buf - jc
    cache_ok = (dist_c >= 0) & (dist_c < WINDOW)
    new_ok = jn <= t
    scale = HEAD_DIM ** -0.5
    n_groups = kv_cols // HEAD_DIM
    sinks = []
    for g in range(n_groups):
        sink = jnp.zeros((rows, 1), F32)
        for hh in range(KV_GROUP):
            sink = jnp.where(head == hh, sinks_ref[g * KV_GROUP + hh], sink)
        sinks.append(sink)

    def per_sequence(b, carry):
        r0 = pl.multiple_of(b * seq, seq)
        qb = q_ref[pl.ds(r0, seq), :]
        knb = kn_ref[pl.ds(r0, seq), :]
        vnb = vn_ref[pl.ds(r0, seq), :]
        mab = ma_ref[pl.ds(r0, seq), :]
        gbb = (gb_lo_ref[pl.ds(r0, seq), :], gb_hi_ref[pl.ds(r0, seq), :])
        groups = range(n_groups)
        gcols = lambda g: slice(g * HEAD_DIM, (g + 1) * HEAD_DIM)
        hcols = lambda g, hh: slice((g * KV_GROUP + hh) * HEAD_DIM, (g * KV_GROUP + hh + 1) * HEAD_DIM)
        q4 = [jnp.concatenate([qb[:, hcols(g, hh)] for hh in range(KV_GROUP)], axis=0) for g in groups]
        s_c = [_dot_nt(q4[g], ck_ref[b, :, gcols(g)]) for g in groups]
        s_n = [_dot_nt(q4[g], knb[:, gcols(g)]) for g in groups]
        p_c, p_n, den = [], [], []
        for g in groups:
            sc = jnp.where(cache_ok, s_c[g] * scale, -jnp.inf)
            sn = jnp.where(new_ok, s_n[g] * scale, -jnp.inf)
            m = jnp.maximum(jnp.maximum(jnp.max(sc, axis=-1, keepdims=True),
                                        jnp.max(sn, axis=-1, keepdims=True)), sinks[g])
            pc, pn = jnp.exp(sc - m), jnp.exp(sn - m)
            p_c.append(pc)
            p_n.append(pn)
            den.append(jnp.sum(pc, axis=-1, keepdims=True) + jnp.sum(pn, axis=-1, keepdims=True)
                       + jnp.exp(sinks[g] - m))
        o4 = [(_dot(p_c[g], cv_ref[b, :, gcols(g)]) + _dot(p_n[g], vnb[:, gcols(g)])) / den[g] for g in groups]
        for g in groups:
            for hh in range(KV_GROUP):
                cs = hcols(g, hh)
                gate = gbb[cs.start // half][:, cs.start % half:cs.start % half + HEAD_DIM]
                o_ref[pl.ds(r0, seq), cs] = _merge(mab[:, cs], gate, o4[g][hh * seq:(hh + 1) * seq])
        ok_ref[b, 0:w_buf - seq, :] = ck_ref[b, seq:w_buf, :]
        ok_ref[b, w_buf - seq:w_buf, :] = knb
        ov_ref[b, 0:w_buf - seq, :] = cv_ref[b, seq:w_buf, :]
        ov_ref[b, w_buf - seq:w_buf, :] = vnb
        return carry

    lax.fori_loop(0, n_b, per_sequence, 0)


def _swa_sample(qh, kh, proj, cache_k, cache_v, merged_a, sinks, *, row0, seq, v_col0, gb_col0, nb=8):
    n_seq, w_buf, kv_cols = cache_k.shape
    q_cols = qh.shape[1]
    half = q_cols // 2
    tr = nb * seq
    blk0 = row0 // tr
    cache_spec = pl.BlockSpec((nb, w_buf, kv_cols), lambda i: (i, 0, 0))
    return pl.pallas_call(
        functools.partial(_swa_sample_kernel, seq=seq),
        grid=(n_seq // nb,),
        in_specs=[
            pl.BlockSpec(memory_space=pltpu.SMEM),
            pl.BlockSpec((tr, q_cols), lambda i: (blk0 + i, 0)),
            pl.BlockSpec((tr, kv_cols), lambda i: (blk0 + i, 0)),
            pl.BlockSpec((tr, kv_cols), lambda i: (blk0 + i, v_col0 // kv_cols)),
            cache_spec, cache_spec,
            pl.BlockSpec((tr, q_cols), lambda i: (i, 0)),
            pl.BlockSpec((tr, half), lambda i: (blk0 + i, gb_col0 // half)),
            pl.BlockSpec((tr, half), lambda i: (blk0 + i, gb_col0 // half + 1)),
        ],
        out_specs=[pl.BlockSpec((tr, q_cols), lambda i: (i, 0)), cache_spec, cache_spec],
        out_shape=[jax.ShapeDtypeStruct((n_seq * seq, q_cols), BF16),
                   jax.ShapeDtypeStruct(cache_k.shape, F32),
                   jax.ShapeDtypeStruct(cache_v.shape, F32)],
        compiler_params=_cparams("parallel"),
        name="swa_sample",
    )(sinks, qh, kh, proj, cache_k, cache_v, merged_a, proj, proj)


def _outproj_kernel(x_ref, mp_ref, ms_ref, w_ref, o_ref, *, n_prompt_tiles):
    i = pl.program_id(0)

    @pl.when(i < n_prompt_tiles)
    def _():
        o_ref[...] = x_ref[...] + jnp.dot(mp_ref[...], w_ref[...], preferred_element_type=F32)

    @pl.when(i >= n_prompt_tiles)
    def _():
        o_ref[...] = x_ref[...] + jnp.dot(ms_ref[...], w_ref[...], preferred_element_type=F32)


def _outproj(x, merged_p, merged_s, w_out, *, tm=512, tn=1024):
    m, d = x.shape
    npt = merged_p.shape[0] // tm
    return pl.pallas_call(
        functools.partial(_outproj_kernel, n_prompt_tiles=npt),
        grid=(m // tm, d // tn),
        in_specs=[
            pl.BlockSpec((tm, tn), lambda i, j: (i, j)),
            pl.BlockSpec((tm, d), lambda i, j: (jnp.minimum(i, npt - 1), 0)),
            pl.BlockSpec((tm, d), lambda i, j: (jnp.maximum(i - npt, 0), 0)),
            pl.BlockSpec((d, tn), lambda i, j: (0, j)),
        ],
        out_specs=pl.BlockSpec((tm, tn), lambda i, j: (i, j)),
        out_shape=jax.ShapeDtypeStruct((m, d), F32),
        compiler_params=_cparams("parallel", "arbitrary"),
        name="outproj",
    )(x, merged_p, merged_s, w_out)


def _ple_kernel(xf_ref, x_ref, pe_ref, nw_ref, wp_ref, wg_ref, o_head_ref, o_tail_ref, xn_ref, *, n_head_tiles):
    i = pl.program_id(0)

    @pl.when(pl.program_id(1) == 0)
    def _():
        xn_ref[...] = _rms(xf_ref[...], nw_ref[...]).astype(BF16)

    gate = jnp.dot(xn_ref[...], wg_ref[...], preferred_element_type=F32)
    emb = jnp.dot(pe_ref[...], wp_ref[...], preferred_element_type=F32)
    out = x_ref[...] + emb * jax.nn.sigmoid(gate)

    @pl.when(i < n_head_tiles)
    def _():
        o_head_ref[...] = out

    @pl.when(i >= n_head_tiles)
    def _():
        o_tail_ref[...] = out


def _ple(x, pe, norm_w, w_proj, w_gate, *, head_rows, tm=512, tn=1024):
    m, d = x.shape
    pdim = pe.shape[1]
    nh = head_rows // tm
    nj = d // tn
    head_idx = lambda i, j: (jnp.minimum(i, nh - 1), jnp.where(i < nh, j, nj - 1))
    tail_idx = lambda i, j: (jnp.maximum(i - nh, 0), jnp.where(i >= nh, j, 0))
    return pl.pallas_call(
        functools.partial(_ple_kernel, n_head_tiles=nh),
        grid=(m // tm, nj),
        in_specs=[
            pl.BlockSpec((tm, d), lambda i, j: (i, 0), pipeline_mode=pl.Buffered(1)),
            pl.BlockSpec((tm, tn), lambda i, j: (i, j)),
            pl.BlockSpec((tm, pdim), lambda i, j: (i, 0)),
            pl.BlockSpec((1, d), lambda i, j: (0, 0)),
            pl.BlockSpec((pdim, tn), lambda i, j: (0, j)),
            pl.BlockSpec((d, tn), lambda i, j: (0, j)),
        ],
        out_specs=[pl.BlockSpec((tm, tn), head_idx), pl.BlockSpec((tm, tn), tail_idx)],
        out_shape=[jax.ShapeDtypeStruct((head_rows, d), F32), jax.ShapeDtypeStruct((m - head_rows, d), F32)],
        scratch_shapes=[pltpu.VMEM((tm, d), BF16)],
        compiler_params=_cparams("arbitrary", "arbitrary"),
        name="ple",
    )(x, x, pe, norm_w.reshape(1, d), w_proj, w_gate)


def _rope_tables(seq, dec_batch, dec_seq):
    half = HEAD_DIM // 2
    inv_freq = ROPE_THETA ** (-jnp.arange(half, dtype=F32) / half)
    pos_p = jnp.arange(seq, dtype=jnp.int32)
    pos_s = jnp.tile(PAST_LEN + jnp.arange(dec_seq, dtype=jnp.int32), dec_batch)
    ang = jnp.concatenate([pos_p, pos_s]).astype(F32)[:, None] * inv_freq[None, :]
    cos, sin = jnp.cos(ang), jnp.sin(ang)
    return jnp.concatenate([cos, cos], axis=1), jnp.concatenate([-sin, sin], axis=1)


def kernel(x_prompt, x_sample, p_prompt, p_sample, state_gdn, state_conv, cache_swa_k, cache_swa_v, ffn1_norm, ffn1_w_gate, ffn1_w_up, ffn1_w_down, mix_norm, w_in, conv_w, A_log, dt_bias, gdn_norm_w, q_norm_w, k_norm_w, sinks, w_out, ffn2_norm, ffn2_w_gate, ffn2_w_up, ffn2_w_down, ple_norm, w_ple_proj, w_ple_gate):
    depth, dec_batch, n_v_heads, dk, dv = state_gdn.shape
    assert depth == 1 and dk == HEAD_DIM and dv == HEAD_DIM
    bp, seq, d = x_prompt.shape
    assert bp == 1
    dec_seq = x_sample.shape[1]
    w_buf, n_kv_heads = cache_swa_k.shape[2], cache_swa_k.shape[3]
    n_heads = sinks.shape[1]
    assert n_heads == n_kv_heads * KV_GROUP and w_buf == WINDOW and CHUNK % dec_seq == 0
    conv_dim = conv_w.shape[2]
    key_dim = n_v_heads // GDN_REP * HEAD_DIM
    val_dim = n_v_heads * HEAD_DIM
    q_dim = n_heads * HEAD_DIM
    kv_cols = n_kv_heads * HEAD_DIM
    n_samp = dec_batch * dec_seq

    ab0 = conv_dim + val_dim
    b0 = ab0 + 2 * n_v_heads
    assert w_in.shape[2] == b0 + q_dim + 2 * kv_cols + 2 * d
    w_all = w_in[0].astype(BF16)
    w_b = w_all[:, b0:]
    n_ab = 128
    w_ab = jnp.pad(w_all[:, ab0:b0], ((0, 0), (0, n_ab - 2 * n_v_heads)))
    z_col0 = conv_dim
    q_col0, k_col0, v_col0 = 0, q_dim, q_dim + kv_cols
    ga_col0 = q_dim + 2 * kv_cols
    gb_col0 = ga_col0 + d

    bf = lambda w: w[0].astype(BF16)
    def ffn_w(wg, wu, wd, tf):
        f_pad = -wg.shape[2] % tf
        return (jnp.pad(bf(wg), ((0, 0), (0, f_pad))), jnp.pad(bf(wu), ((0, 0), (0, f_pad))),
                jnp.pad(bf(wd), ((0, f_pad), (0, 0))))

    pe = jnp.concatenate([p_prompt.reshape(seq, PLE_DIM), p_sample.reshape(n_samp, PLE_DIM)], axis=0)

    x = _ffn([x_prompt.reshape(seq, d), x_sample.reshape(n_samp, d)], ffn1_norm[0],
             *ffn_w(ffn1_w_gate, ffn1_w_up, ffn1_w_down, 256), tf=256)
    proj_a, ab = _inproj(x, mix_norm[0], w_all, w_ab, n=ab0)
    (proj_b,) = _inproj(x, mix_norm[0], w_b)

    carried = jnp.pad(state_conv[0], ((0, 0), (dec_seq - (CONV_W - 1), 0), (0, 0))).reshape(n_samp, conv_dim)
    act_p = _conv_prompt(proj_a, conv_w[0], rows=seq, conv_dim=conv_dim, key_dim=key_dim)
    act_s = _conv_sample(proj_a, carried, conv_w[0], row0=seq, rows=n_samp, seq=dec_seq,
                         conv_dim=conv_dim, key_dim=key_dim)
    ab_t = ab.T
    gdn_args = dict(n_heads=n_v_heads, z_col0=z_col0, ga_col0=ga_col0)
    ma_p, sg_p = _gdn_prompt(act_p, proj_a, proj_b, ab, ab_t, A_log[0], dt_bias[0], gdn_norm_w[0],
                             rows=seq, **gdn_args)
    ma_s, sg_s = _gdn_sample(act_s, proj_a, proj_b, ab, ab_t, A_log[0], dt_bias[0], gdn_norm_w[0],
                             state_gdn[0], row0=seq, rows=n_samp, seq=dec_seq, **gdn_args)

    cos, sin = _rope_tables(seq, dec_batch, dec_seq)
    qh, kh = _rope(proj_b, cos, sin, q_norm_w[0], k_norm_w[0], q_col0=q_col0, q_cols=q_dim,
                   k_col0=k_col0, k_cols=kv_cols)
    merged_p = _swa_prompt(qh, kh, proj_b, ma_p, sinks[0], rows=seq, n_kv_heads=n_kv_heads,
                           v_col0=v_col0, gb_col0=gb_col0)
    merged_s, kk_s, vv_s = _swa_sample(qh, kh, proj_b, cache_swa_k[0].reshape(dec_batch, w_buf, kv_cols),
                                       cache_swa_v[0].reshape(dec_batch, w_buf, kv_cols),
                                       ma_s, sinks[0], row0=seq, seq=dec_seq,
                                       v_col0=v_col0, gb_col0=gb_col0)

    x = _outproj(x, merged_p, merged_s, bf(w_out))
    x = _ffn([x], ffn2_norm[0], *ffn_w(ffn2_w_gate, ffn2_w_up, ffn2_w_down, 512), tf=512,
             single_buffer_out=True)
    y_p, y_s = _ple(x, pe.astype(BF16), ple_norm[0], bf(w_ple_proj), bf(w_ple_gate), head_rows=seq)

    cache_shape = (1, 1, w_buf, n_kv_heads, HEAD_DIM)
    return (
        y_p.reshape(1, seq, d),
        y_s.reshape(dec_batch, dec_seq, d),
        sg_p.reshape(1, 1, n_v_heads, HEAD_DIM, HEAD_DIM),
        proj_a[seq - (CONV_W - 1):seq, :conv_dim].reshape(1, 1, CONV_W - 1, conv_dim),
        kh[seq - w_buf:seq].reshape(cache_shape),
        proj_b[seq - w_buf:seq, v_col0:v_col0 + kv_cols].reshape(cache_shape),
        sg_s.reshape(1, dec_batch, n_v_heads, HEAD_DIM, HEAD_DIM),
        proj_a[seq:, :conv_dim].reshape(dec_batch, dec_seq, conv_dim)[:, dec_seq - (CONV_W - 1):].reshape(
            1, dec_batch, CONV_W - 1, conv_dim),
        kk_s.reshape(1, dec_batch, w_buf, n_kv_heads, HEAD_DIM),
        vv_s.reshape(1, dec_batch, w_buf, n_kv_heads, HEAD_DIM),
    )
```

```python
import functools

import jax
import jax.numpy as jnp
from jax import lax
from jax.experimental import pallas as pl
from jax.experimental.pallas import tpu as pltpu

F32 = jnp.float32
BF16 = jnp.bfloat16

EPS = 1e-6
HEAD_DIM = 128
KV_GROUP = 4
WINDOW = 128
ROPE_THETA = 10000.0
PAST_LEN = 8192
GDN_REP = 2
CONV_W = 4
CHUNK = 64
PLE_DIM = 256
GDN_STEP_HEADS = 4
FFN_OUT_CHUNK = 1024
SWA_STEP_GROUPS = 4

V7X_VMEM_LIMIT_BYTES = 62 * 1024 * 1024


def _cparams(*sem):
    return pltpu.CompilerParams(dimension_semantics=sem, vmem_limit_bytes=V7X_VMEM_LIMIT_BYTES)


def _rms(x, w):
    return x * lax.rsqrt(jnp.mean(x * x, axis=-1, keepdims=True) + EPS) * w


def _dot(a, b):
    return jnp.dot(a.astype(BF16), b.astype(BF16), preferred_element_type=F32)


def _dot_nt(a, b):
    return lax.dot_general(a.astype(BF16), b.astype(BF16), (((1,), (1,)), ((), ())),
                           preferred_element_type=F32)


def _dot_tn(a, b):
    return lax.dot_general(a.astype(BF16), b.astype(BF16), (((0,), (0,)), ((), ())),
                           preferred_element_type=F32)


def _row_part_specs(parts, tm, width):
    specs, spans, start = [], [], 0
    for part in parts:
        n = part.shape[0] // tm
        specs.append(pl.BlockSpec(
            (tm, width), functools.partial(lambda i, j, s, n: (jnp.clip(i - s, 0, n - 1), 0), s=start, n=n),
            pipeline_mode=pl.Buffered(1)))
        spans.append((start, n))
        start += n
    return specs, spans


def _ffn_kernel(*refs, spans):
    n_parts = len(spans)
    x_refs = refs[:n_parts]
    nw_ref, wg_ref, wu_ref, wd_ref, o_ref, xn_ref = refs[n_parts:]
    i = pl.program_id(0)
    j = pl.program_id(1)
    for x_ref, (start, n) in zip(x_refs, spans):
        @pl.when((j == 0) & (i >= start) & (i < start + n))
        def _():
            x = x_ref[...]
            xn_ref[...] = _rms(x, nw_ref[...]).astype(BF16)
            o_ref[...] = x

    xn = xn_ref[...]
    g = jnp.dot(xn, wg_ref[...], preferred_element_type=F32)
    u = jnp.dot(xn, wu_ref[...], preferred_element_type=F32)
    h = ((0.5 * (g * jax.nn.sigmoid(g))) * u).astype(BF16)
    d = o_ref.shape[1]
    for c in range(0, d, FFN_OUT_CHUNK):
        cs = slice(c, min(c + FFN_OUT_CHUNK, d))
        o_ref[:, cs] += jnp.dot(h, wd_ref[:, cs], preferred_element_type=F32)


def _ffn(x_parts, norm_w, wg, wu, wd, *, tm=512, tf=256):
    d = x_parts[0].shape[1]
    m = sum(p.shape[0] for p in x_parts)
    f = wg.shape[1]
    x_specs, spans = _row_part_specs(x_parts, tm, d)
    return pl.pallas_call(
        functools.partial(_ffn_kernel, spans=spans),
        grid=(m // tm, f // tf),
        in_specs=[
            *x_specs,
            pl.BlockSpec((1, d), lambda i, j: (0, 0)),
            pl.BlockSpec((d, tf), lambda i, j: (0, j)),
            pl.BlockSpec((d, tf), lambda i, j: (0, j)),
            pl.BlockSpec((tf, d), lambda i, j: (j, 0)),
        ],
        out_specs=pl.BlockSpec((tm, d), lambda i, j: (i, 0)),
        out_shape=jax.ShapeDtypeStruct((m, d), F32),
        scratch_shapes=[pltpu.VMEM((tm, d), BF16)],
        compiler_params=_cparams("parallel", "arbitrary"),
        name="ffn",
    )(*x_parts, norm_w.reshape(1, d), wg, wu, wd)


def _inproj_kernel(x_ref, nw_ref, w_ref, o_ref, xn_ref):
    @pl.when(pl.program_id(1) == 0)
    def _():
        xn_ref[...] = _rms(x_ref[...], nw_ref[...]).astype(BF16)

    o_ref[...] = jnp.dot(xn_ref[...], w_ref[...], preferred_element_type=F32)


def _inproj_ab_kernel(x_ref, nw_ref, w_ref, wab_ref, o_ref, ab_ref, xn_ref):
    @pl.when(pl.program_id(1) == 0)
    def _():
        xn = _rms(x_ref[...], nw_ref[...]).astype(BF16)
        xn_ref[...] = xn
        ab_ref[...] = jnp.dot(xn, wab_ref[...], preferred_element_type=F32)

    o_ref[...] = jnp.dot(xn_ref[...], w_ref[...], preferred_element_type=F32)


def _inproj_rope_kernel(x_ref, nw_ref, w_ref, cos_ref, sin_ref, qw_ref, kw_ref, o_ref, xn_ref, *,
                        n_q_blocks, n_rope_blocks):
    j = pl.program_id(1)

    @pl.when(j == 0)
    def _():
        xn_ref[...] = _rms(x_ref[...], nw_ref[...]).astype(BF16)

    acc = jnp.dot(xn_ref[...], w_ref[...], preferred_element_type=F32)

    @pl.when(j < n_rope_blocks)
    def _():
        w = jnp.where(j < n_q_blocks, qw_ref[...], kw_ref[...])
        cos = cos_ref[...]
        sin = sin_ref[...]
        for h in range(acc.shape[1] // HEAD_DIM):
            cs = slice(h * HEAD_DIM, (h + 1) * HEAD_DIM)
            y = _rms(acc[:, cs], w)
            o_ref[:, cs] = y * cos + pltpu.roll(y, HEAD_DIM // 2, 1) * sin

    @pl.when(j >= n_rope_blocks)
    def _():
        o_ref[...] = acc


def _inproj_rope(x, norm_w, w, cos, sin, q_norm_w, k_norm_w, *, q_cols, k_cols, tm=512, tn=1024):
    m, d = x.shape
    n = w.shape[1]
    row = lambda i, j: (i, 0)
    fixed = lambda i, j: (0, 0)
    return pl.pallas_call(
        functools.partial(_inproj_rope_kernel, n_q_blocks=q_cols // tn, n_rope_blocks=(q_cols + k_cols) // tn),
        grid=(m // tm, n // tn),
        in_specs=[
            pl.BlockSpec((tm, d), row, pipeline_mode=pl.Buffered(1)),
            pl.BlockSpec((1, d), fixed),
            pl.BlockSpec((d, tn), lambda i, j: (0, j)),
            pl.BlockSpec((tm, HEAD_DIM), row),
            pl.BlockSpec((tm, HEAD_DIM), row),
            pl.BlockSpec((1, HEAD_DIM), fixed),
            pl.BlockSpec((1, HEAD_DIM), fixed),
        ],
        out_specs=pl.BlockSpec((tm, tn), lambda i, j: (i, j)),
        out_shape=jax.ShapeDtypeStruct((m, n), F32),
        scratch_shapes=[pltpu.VMEM((tm, d), BF16)],
        compiler_params=_cparams("parallel", "arbitrary"),
        name="inproj_rope",
    )(x, norm_w.reshape(1, d), w, cos, sin, q_norm_w.reshape(1, HEAD_DIM), k_norm_w.reshape(1, HEAD_DIM))


def _inproj(x, norm_w, w, w_ab=None, *, n=None, tm=512, tn=1024):
    m, d = x.shape
    n = w.shape[1] if n is None else n
    in_specs = [
        pl.BlockSpec((tm, d), lambda i, j: (i, 0), pipeline_mode=pl.Buffered(1)),
        pl.BlockSpec((1, d), lambda i, j: (0, 0)),
        pl.BlockSpec((d, tn), lambda i, j: (0, j)),
    ]
    out_specs = [pl.BlockSpec((tm, tn), lambda i, j: (i, j))]
    out_shape = [jax.ShapeDtypeStruct((m, n), F32)]
    args = [x, norm_w.reshape(1, d), w]
    body = _inproj_kernel
    if w_ab is not None:
        nab = w_ab.shape[1]
        in_specs.append(pl.BlockSpec((d, nab), lambda i, j: (0, 0)))
        out_specs.append(pl.BlockSpec((tm, nab), lambda i, j: (i, 0)))
        out_shape.append(jax.ShapeDtypeStruct((m, nab), F32))
        args.append(w_ab)
        body = _inproj_ab_kernel
    return pl.pallas_call(
        body,
        grid=(m // tm, n // tn),
        in_specs=in_specs,
        out_specs=out_specs,
        out_shape=out_shape,
        scratch_shapes=[pltpu.VMEM((tm, d), BF16)],
        compiler_params=_cparams("parallel", "arbitrary"),
        name="inproj",
    )(*args)


def _conv_silu(x, w, shifted):
    y = None
    for tap in range(CONV_W):
        sh = CONV_W - 1 - tap
        term = (x if sh == 0 else shifted(sh)) * w[tap:tap + 1]
        y = term if y is None else y + term
    return y * jax.nn.sigmoid(y)


def _shift_in_sequence(x, halo):
    row8 = lax.broadcasted_iota(jnp.int32, (8, 1), 0)

    def shifted(sh):
        rolled = pltpu.roll(x, sh, 0)
        head = jnp.where(row8 < sh, pltpu.roll(halo, sh, 0), rolled[:8])
        return jnp.concatenate([head, rolled[8:]], axis=0)

    return shifted


def _shift_in_groups(x, carried, seq):
    rows = x.shape[0]
    t = lax.broadcasted_iota(jnp.int32, (rows, 1), 0) % seq
    return lambda sh: jnp.where(t >= sh, pltpu.roll(x, sh, 0), pltpu.roll(carried, rows + sh - seq, 0))


def _l2norm_heads(y, scale):
    heads = []
    for h in range(y.shape[1] // HEAD_DIM):
        yh = y[:, h * HEAD_DIM:(h + 1) * HEAD_DIM]
        heads.append(yh * (lax.rsqrt(jnp.sum(yh * yh, axis=-1, keepdims=True) + EPS) * scale))
    return jnp.concatenate(heads, axis=1)


def _gdn_activations(pre, conv_w, shifts):
    q, k, v = (_conv_silu(x, w, s) for x, w, s in zip(pre, conv_w, shifts))
    return _l2norm_heads(q, HEAD_DIM ** -0.5), _l2norm_heads(k, 1.0), v


def _softplus(x):
    return jnp.maximum(x, 0.0) + jnp.log1p(jnp.exp(-jnp.abs(x)))


def _seg_cumsum(x, pos, seg, axis):
    d = 1
    while d < seg:
        x = x + jnp.where(pos >= d, pltpu.roll(x, d, axis), 0.0)
        d *= 2
    return x


def _gates(ab, abt8, alog_row, dtb_row, alog_col8, dtb_col8, seg):
    rows = ab.shape[0]
    pos_c = lax.broadcasted_iota(jnp.int32, (rows, 1), 0) % seg
    pos_r = lax.broadcasted_iota(jnp.int32, (1, rows), 1) % seg
    g_all = -jnp.exp(alog_row) * _softplus(ab + dtb_row)
    g_t = -jnp.exp(alog_col8) * _softplus(abt8 + dtb_col8)
    return _seg_cumsum(g_all, pos_c, seg, 0), jax.nn.sigmoid(ab), _seg_cumsum(g_t, pos_r, seg, 1)


def _pick_lane(x, idx):
    lane = lax.broadcasted_iota(jnp.int32, (1, x.shape[1]), 1)
    return jnp.sum(jnp.where(lane == idx, x, 0.0), axis=1, keepdims=True)


def _pick_sublane(x, idx):
    sub = lax.broadcasted_iota(jnp.int32, (x.shape[0], 1), 0)
    return jnp.sum(jnp.where(sub == idx, x, 0.0), axis=0, keepdims=True)


def _head_gates(gate_vals, h, n_heads):
    gc_all, sig_all, gc_t = gate_vals
    return _pick_lane(sig_all, n_heads + h), _pick_lane(gc_all, h), _pick_sublane(gc_t, h % 8)


def _load_gates(ab_ref, abt_ref, alr_ref, dtr_ref, alc_ref, dtc_ref, h0, seg):
    h8 = pl.multiple_of((h0 // 8) * 8, 8)
    return _gates(ab_ref[...], abt_ref[pl.ds(h8, 8), :], alr_ref[...], dtr_ref[...],
                  alc_ref[pl.ds(h8, 8), :], dtc_ref[pl.ds(h8, 8), :], seg)


def _chunk_masks(seg):
    r = lax.broadcasted_iota(jnp.int32, (CHUNK, CHUNK), 0)
    c = lax.broadcasted_iota(jnp.int32, (CHUNK, CHUNK), 1)
    same = (r // seg) == (c // seg)
    return same & (c <= r), same & (c < r), r == c


def _gdn_intra(q_all, k_all, v_all, gates, seg):
    incl, strict, eye = _chunk_masks(seg)
    n_chunks = q_all.shape[0] // CHUNK
    n_heads = len(gates)
    chunks = range(n_chunks)
    heads = range(n_heads)
    items = [(n, hh) for n in chunks for hh in heads]
    rows = lambda n: slice(n * CHUNK, (n + 1) * CHUNK)
    cols = lambda j: slice(j * HEAD_DIM, (j + 1) * HEAD_DIM)
    pairs = [(n, j) for n in chunks for j in range(n_heads // GDN_REP)]
    q = {(n, j): q_all[rows(n), cols(j)] for n, j in pairs}
    k = {(n, j): k_all[rows(n), cols(j)] for n, j in pairs}
    kk_qk = {p: _dot_nt(jnp.concatenate([k[p], q[p]], axis=0), k[p]) for p in pairs}
    kk = {p: kk_qk[p][:CHUNK] for p in pairs}
    qk = {p: kk_qk[p][CHUNK:] for p in pairs}
    out = {}
    low, rhs = {}, {}
    for n, hh in items:
        beta, gc_col, gc_row = gates[hh]
        beta, gc_col, gc_row = beta[rows(n)], gc_col[rows(n)], gc_row[:, rows(n)]
        pj = (n, hh // GDN_REP)
        decay = jnp.exp(jnp.where(incl, gc_col - gc_row, -jnp.inf))
        low[n, hh] = jnp.where(strict, beta * kk[pj] * decay, 0.0)
        e_gc = jnp.exp(gc_col)
        rhs[n, hh] = jnp.concatenate([v_all[rows(n), cols(hh)] * beta, k[pj] * (beta * e_gc)], axis=1)
        out[n, hh] = dict(a=jnp.where(incl, qk[pj] * decay, 0.0), qd=q[pj] * e_gc, gc=gc_col, k=k[pj])
    p = {it: -low[it] for it in items}
    t = {it: jnp.where(eye, 1.0, p[it]) for it in items}
    if seg > 2:
        p = {it: _dot(p[it], p[it]) for it in items}
        n_pow = 4
        while n_pow < seg:
            both = {it: _dot(jnp.concatenate([t[it], p[it]], axis=0), p[it]) for it in items}
            t = {it: t[it] + both[it][:CHUNK] for it in items}
            p = {it: both[it][CHUNK:] for it in items}
            n_pow *= 2
        t = {it: t[it] + _dot(t[it], p[it]) for it in items}
    for it in items:
        uw = _dot(t[it], rhs[it])
        out[it]["u"] = uw[:, :HEAD_DIM]
        out[it]["w"] = uw[:, HEAD_DIM:]
    return out


def _gdn_emit(o, z, gate, nw):
    return (_rms(o, nw) * (z * jax.nn.sigmoid(z)) * jax.nn.sigmoid(gate)).astype(BF16)


def _gdn_prompt_kernel(q_ref, k_ref, v_ref, qh_ref, kh_ref, vh_ref, qw_ref, kw_ref, vw_ref, z_ref, ga_ref,
                       ab_ref, abt_ref, alr_ref, dtr_ref, alc_ref, dtc_ref, nw_ref, o_ref, sfin_ref, s_ref, *,
                       n_heads):
    h0 = pl.program_id(0) * GDN_STEP_HEADS
    i = pl.program_id(1)

    @pl.when(i == 0)
    def _():
        s_ref[...] = jnp.zeros_like(s_ref)

    nw = nw_ref[...]
    heads = range(GDN_STEP_HEADS)
    gate_vals = _load_gates(ab_ref, abt_ref, alr_ref, dtr_ref, alc_ref, dtc_ref, h0, CHUNK)
    gates = [_head_gates(gate_vals, h0 + hh, n_heads) for hh in heads]
    pre = [r[...] for r in (q_ref, k_ref, v_ref)]
    halos = [jnp.where(i > 0, r[...], 0.0) for r in (qh_ref, kh_ref, vh_ref)]
    taps = [r[...] for r in (qw_ref, kw_ref, vw_ref)]
    q, k, v = _gdn_activations(pre, taps, [_shift_in_sequence(x, h) for x, h in zip(pre, halos)])
    intra = _gdn_intra(q, k, v, gates, CHUNK)
    state = [s_ref[hh] for hh in heads]
    for n in range(q.shape[0] // CHUNK):
        rows = slice(n * CHUNK, (n + 1) * CHUNK)
        c = [intra[n, hh] for hh in heads]
        ws_qs = [_dot(jnp.concatenate([c[hh]["w"], c[hh]["qd"]], axis=0), state[hh]) for hh in heads]
        v_new = [c[hh]["u"] - ws_qs[hh][:CHUNK] for hh in heads]
        g_last = [c[hh]["gc"][CHUNK - 1:CHUNK] for hh in heads]
        k_st = [(c[hh]["k"] * jnp.exp(g_last[hh] - c[hh]["gc"])).T for hh in heads]
        av_kv = [_dot(jnp.concatenate([c[hh]["a"], k_st[hh]], axis=0), v_new[hh]) for hh in heads]
        o = [ws_qs[hh][CHUNK:] + av_kv[hh][:CHUNK] for hh in heads]
        state = [state[hh] * jnp.exp(g_last[hh]) + av_kv[hh][CHUNK:] for hh in heads]
        for hh in heads:
            cs = slice(hh * HEAD_DIM, (hh + 1) * HEAD_DIM)
            o_ref[rows, cs] = _gdn_emit(o[hh], z_ref[rows, cs], ga_ref[rows, cs], nw)
    for hh in heads:
        s_ref[hh] = state[hh]

    @pl.when(i == pl.num_programs(1) - 1)
    def _():
        sfin_ref[...] = s_ref[...]


def _gate_param_specs(n_lanes, idx):
    row = pl.BlockSpec((1, n_lanes), idx)
    col = pl.BlockSpec((n_lanes, 1), idx)
    return [row, row, col, col]


def _gate_params(alog, dtb, n_lanes):
    pad = lambda v: jnp.pad(v.astype(F32), (0, n_lanes - v.shape[0]))
    al, dt = pad(alog), pad(dtb)
    return al.reshape(1, n_lanes), dt.reshape(1, n_lanes), al.reshape(n_lanes, 1), dt.reshape(n_lanes, 1)


def _qkv_specs(block_rows, row_idx, head_idx, qw, vw, key_dim):
    return [
        pl.BlockSpec((block_rows, qw), lambda a, b: (row_idx(a, b), head_idx(a, b))),
        pl.BlockSpec((block_rows, qw), lambda a, b: (row_idx(a, b), key_dim // qw + head_idx(a, b))),
        pl.BlockSpec((block_rows, vw), lambda a, b: (row_idx(a, b), 2 * key_dim // vw + head_idx(a, b))),
    ]


def _gdn_prompt(proj_a, proj_b, conv_w, ab, ab_t, alog, dtb, norm_w, *, rows, n_heads, z_col0, ga_col0, tc=512):
    hs = GDN_STEP_HEADS
    qw = hs // GDN_REP * HEAD_DIM
    vw = hs * HEAD_DIM
    key_dim = n_heads // GDN_REP * HEAD_DIM
    n_lanes = ab.shape[1]
    tc8 = tc // 8
    head = lambda h, i: h
    return pl.pallas_call(
        functools.partial(_gdn_prompt_kernel, n_heads=n_heads),
        grid=(n_heads // hs, rows // tc),
        in_specs=[
            *_qkv_specs(tc, lambda h, i: i, head, qw, vw, key_dim),
            *_qkv_specs(8, lambda h, i: jnp.maximum(i * tc8 - 1, 0), head, qw, vw, key_dim),
            *_qkv_specs(CONV_W, lambda h, i: 0, head, qw, vw, key_dim),
            pl.BlockSpec((tc, vw), lambda h, i: (i, z_col0 // vw + h)),
            pl.BlockSpec((tc, vw), lambda h, i: (i, ga_col0 // vw + h)),
            pl.BlockSpec((tc, n_lanes), lambda h, i: (i, 0)),
            pl.BlockSpec((n_lanes, tc), lambda h, i: (0, i)),
            *_gate_param_specs(n_lanes, lambda h, i: (0, 0)),
            pl.BlockSpec((1, HEAD_DIM), lambda h, i: (0, 0)),
        ],
        out_specs=[
            pl.BlockSpec((tc, vw), lambda h, i: (i, h)),
            pl.BlockSpec((hs, HEAD_DIM, HEAD_DIM), lambda h, i: (h, 0, 0)),
        ],
        out_shape=[jax.ShapeDtypeStruct((rows, n_heads * HEAD_DIM), BF16),
                   jax.ShapeDtypeStruct((n_heads, HEAD_DIM, HEAD_DIM), F32)],
        scratch_shapes=[pltpu.VMEM((hs, HEAD_DIM, HEAD_DIM), F32)],
        compiler_params=_cparams("parallel", "arbitrary"),
        name="gdn_prompt",
    )(*[proj_a] * 6, *[conv_w] * 3, proj_a, proj_b, ab, ab_t, *_gate_params(alog, dtb, n_lanes),
      norm_w.reshape(1, HEAD_DIM))


def _gdn_sample_kernel(q_ref, k_ref, v_ref, qc_ref, kc_ref, vc_ref, qw_ref, kw_ref, vw_ref, z_ref, ga_ref,
                       ab_ref, abt_ref, alr_ref, dtr_ref, alc_ref, dtc_ref, nw_ref, s0_ref, o_ref, s1_ref, *,
                       n_heads, seq):
    h0 = pl.program_id(1) * GDN_STEP_HEADS
    nw = nw_ref[...]
    heads = range(GDN_STEP_HEADS)
    gate_vals = _load_gates(ab_ref, abt_ref, alr_ref, dtr_ref, alc_ref, dtc_ref, h0, seq)
    gates = [_head_gates(gate_vals, h0 + hh, n_heads) for hh in heads]
    pre = [r[...] for r in (q_ref, k_ref, v_ref)]
    carried = [r[...] for r in (qc_ref, kc_ref, vc_ref)]
    taps = [r[...] for r in (qw_ref, kw_ref, vw_ref)]
    q, k, v = _gdn_activations(pre, taps, [_shift_in_groups(x, c, seq) for x, c in zip(pre, carried)])
    intra = _gdn_intra(q, k, v, gates, seq)
    per_chunk = CHUNK // seq
    for n in range(q.shape[0] // CHUNK):
        rows = slice(n * CHUNK, (n + 1) * CHUNK)
        seqs = [(hh, b) for hh in heads for b in range(per_chunk)]
        sub = lambda b: slice(b * seq, (b + 1) * seq)
        c = [intra[n, hh] for hh in heads]
        s0 = {(hh, b): s0_ref[n * per_chunk + b, hh] for hh, b in seqs}
        ws_qs = {(hh, b): _dot(jnp.concatenate([c[hh]["w"][sub(b)], c[hh]["qd"][sub(b)]], axis=0), s0[hh, b])
                 for hh, b in seqs}
        v_new = {(hh, b): c[hh]["u"][sub(b)] - ws_qs[hh, b][:seq] for hh, b in seqs}
        for hh, b in seqs:
            gc = c[hh]["gc"][sub(b)]
            g_last = gc[seq - 1:seq]
            k_st = c[hh]["k"][sub(b)] * jnp.exp(g_last - gc)
            s1_ref[n * per_chunk + b, hh] = s0[hh, b] * jnp.exp(g_last) + _dot_tn(k_st, v_new[hh, b])
        for hh in heads:
            cs = slice(hh * HEAD_DIM, (hh + 1) * HEAD_DIM)
            qs = jnp.concatenate([ws_qs[hh, b][seq:] for b in range(per_chunk)], axis=0)
            vn = jnp.concatenate([v_new[hh, b] for b in range(per_chunk)], axis=0)
            out = _gdn_emit(qs + _dot(c[hh]["a"], vn), z_ref[rows, cs], ga_ref[rows, cs], nw)
            o_ref[rows, cs] = out.astype(o_ref.dtype)


def _gdn_sample(proj_a, proj_b, carried, conv_w, ab, ab_t, alog, dtb, norm_w, s0, *, row0, rows, seq, n_heads,
                z_col0, ga_col0, tc=128):
    hs = GDN_STEP_HEADS
    head = lambda i, h: h
    qw = hs // GDN_REP * HEAD_DIM
    vw = hs * HEAD_DIM
    key_dim = n_heads // GDN_REP * HEAD_DIM
    blk0 = row0 // tc
    n_lanes = ab.shape[1]
    state_spec = pl.BlockSpec((tc // seq, hs, HEAD_DIM, HEAD_DIM), lambda i, h: (i, h, 0, 0))
    return pl.pallas_call(
        functools.partial(_gdn_sample_kernel, n_heads=n_heads, seq=seq),
        grid=(rows // tc, n_heads // hs),
        in_specs=[
            *_qkv_specs(tc, lambda i, h: blk0 + i, head, qw, vw, key_dim),
            *_qkv_specs(tc, lambda i, h: i, head, qw, vw, key_dim),
            *_qkv_specs(CONV_W, lambda i, h: 0, head, qw, vw, key_dim),
            pl.BlockSpec((tc, vw), lambda i, h: (blk0 + i, z_col0 // vw + h)),
            pl.BlockSpec((tc, vw), lambda i, h: (blk0 + i, ga_col0 // vw + h)),
            pl.BlockSpec((tc, n_lanes), lambda i, h: (blk0 + i, 0)),
            pl.BlockSpec((n_lanes, tc), lambda i, h: (0, blk0 + i)),
            *_gate_param_specs(n_lanes, lambda i, h: (0, 0)),
            pl.BlockSpec((1, HEAD_DIM), lambda i, h: (0, 0)),
            state_spec,
        ],
        out_specs=[pl.BlockSpec((tc, vw), lambda i, h: (i, h)), state_spec],
        out_shape=[jax.ShapeDtypeStruct((rows, n_heads * HEAD_DIM), F32),
                   jax.ShapeDtypeStruct(s0.shape, F32)],
        compiler_params=_cparams("parallel", "parallel"),
        name="gdn_sample",
    )(*[proj_a] * 3, *[carried] * 3, *[conv_w] * 3, proj_a, proj_b, ab, ab_t, *_gate_params(alog, dtb, n_lanes),
      norm_w.reshape(1, HEAD_DIM), s0)


def _merge(other, gate, o):
    return (other.astype(F32) + jax.nn.sigmoid(gate) * o).astype(BF16)


def _swa_prompt_kernel(sinks_ref, q_ref, kc_ref, kp_ref, vc_ref, vp_ref, ma_ref, gb_ref, o_ref):
    g0 = pl.program_id(0) * SWA_STEP_GROUPS
    i = pl.program_id(1)
    rows = KV_GROUP * WINDOW
    r = lax.broadcasted_iota(jnp.int32, (rows, 2 * WINDOW), 0) % WINDOW
    c = lax.broadcasted_iota(jnp.int32, (rows, 2 * WINDOW), 1)
    visible = ((c < WINDOW) & (c > r) & (i > 0)) | ((c >= WINDOW) & (c - WINDOW <= r))
    head = lax.broadcasted_iota(jnp.int32, (rows, 1), 0) // WINDOW
    scale = HEAD_DIM ** -0.5
    groups = range(SWA_STEP_GROUPS)
    hcols = lambda g, hh: slice((g * KV_GROUP + hh) * HEAD_DIM, (g * KV_GROUP + hh + 1) * HEAD_DIM)
    gcols = lambda g: slice(g * HEAD_DIM, (g + 1) * HEAD_DIM)
    q4 = [jnp.concatenate([q_ref[:, hcols(g, hh)] for hh in range(KV_GROUP)], axis=0) for g in groups]
    kcat = [jnp.concatenate([kp_ref[:, gcols(g)], kc_ref[:, gcols(g)]], axis=0) for g in groups]
    vcat = [jnp.concatenate([vp_ref[:, gcols(g)], vc_ref[:, gcols(g)]], axis=0) for g in groups]
    s = [_dot_nt(q4[g], kcat[g]) for g in groups]
    p, den = [], []
    for g in groups:
        sink = jnp.zeros((rows, 1), F32)
        for hh in range(KV_GROUP):
            sink = jnp.where(head == hh, sinks_ref[(g0 + g) * KV_GROUP + hh], sink)
        sg = jnp.where(visible, s[g] * scale, -jnp.inf)
        m = jnp.maximum(jnp.max(sg, axis=-1, keepdims=True), sink)
        pg = jnp.exp(sg - m)
        p.append(pg)
        den.append(jnp.sum(pg, axis=-1, keepdims=True) + jnp.exp(sink - m))
    o4 = [_dot(p[g], vcat[g]) / den[g] for g in groups]
    for g in groups:
        for hh in range(KV_GROUP):
            cs = hcols(g, hh)
            o_ref[:, cs] = _merge(ma_ref[:, cs], gb_ref[:, cs], o4[g][hh * WINDOW:(hh + 1) * WINDOW])


def _swa_prompt(proj, merged_a, sinks, *, rows, n_kv_heads, k_col0, v_col0, gb_col0):
    sg = SWA_STEP_GROUPS
    qw = sg * KV_GROUP * HEAD_DIM
    kw = sg * HEAD_DIM
    prev = lambda i: jnp.maximum(i - 1, 0)
    return pl.pallas_call(
        _swa_prompt_kernel,
        grid=(n_kv_heads // sg, rows // WINDOW),
        in_specs=[
            pl.BlockSpec(memory_space=pltpu.SMEM),
            pl.BlockSpec((WINDOW, qw), lambda g, i: (i, g)),
            pl.BlockSpec((WINDOW, kw), lambda g, i: (i, k_col0 // kw + g)),
            pl.BlockSpec((WINDOW, kw), lambda g, i: (prev(i), k_col0 // kw + g)),
            pl.BlockSpec((WINDOW, kw), lambda g, i: (i, v_col0 // kw + g)),
            pl.BlockSpec((WINDOW, kw), lambda g, i: (prev(i), v_col0 // kw + g)),
            pl.BlockSpec((WINDOW, qw), lambda g, i: (i, g)),
            pl.BlockSpec((WINDOW, qw), lambda g, i: (i, gb_col0 // qw + g)),
        ],
        out_specs=pl.BlockSpec((WINDOW, qw), lambda g, i: (i, g)),
        out_shape=jax.ShapeDtypeStruct((rows, n_kv_heads * KV_GROUP * HEAD_DIM), BF16),
        compiler_params=_cparams("parallel", "parallel"),
        name="swa_prompt",
    )(sinks, proj, proj, proj, proj, proj, merged_a, proj)


def _swa_sample_kernel(sinks_ref, q_ref, kn_ref, vn_ref, ck_ref, cv_ref, ma_ref, gb_lo_ref, gb_hi_ref,
                       o_ref, ok_ref, ov_ref, *, seq):
    n_b, w_buf, kv_cols = ck_ref.shape
    half = gb_lo_ref.shape[1]
    rows = KV_GROUP * seq
    t = lax.broadcasted_iota(jnp.int32, (rows, 1), 0) % seq
    head = lax.broadcasted_iota(jnp.int32, (rows, 1), 0) // seq
    jc = lax.broadcasted_iota(jnp.int32, (1, w_buf), 1)
    jn = lax.broadcasted_iota(jnp.int32, (1, seq), 1)
    dist_c = t + w_buf - jc
    cache_ok = (dist_c >= 0) & (dist_c < WINDOW)
    new_ok = jn <= t
    scale = HEAD_DIM ** -0.5
    n_groups = kv_cols // HEAD_DIM
    sinks = []
    for g in range(n_groups):
        sink = jnp.zeros((rows, 1), F32)
        for hh in range(KV_GROUP):
            sink = jnp.where(head == hh, sinks_ref[g * KV_GROUP + hh], sink)
        sinks.append(sink)

    def per_sequence(b, carry):
        r0 = pl.multiple_of(b * seq, seq)
        qb = q_ref[pl.ds(r0, seq), :]
        knb = kn_ref[pl.ds(r0, seq), :]
        vnb = vn_ref[pl.ds(r0, seq), :]
        mab = ma_ref[pl.ds(r0, seq), :]
        gbb = (gb_lo_ref[pl.ds(r0, seq), :], gb_hi_ref[pl.ds(r0, seq), :])
        groups = range(n_groups)
        gcols = lambda g: slice(g * HEAD_DIM, (g + 1) * HEAD_DIM)
        hcols = lambda g, hh: slice((g * KV_GROUP + hh) * HEAD_DIM, (g * KV_GROUP + hh + 1) * HEAD_DIM)
        q4 = [jnp.concatenate([qb[:, hcols(g, hh)] for hh in range(KV_GROUP)], axis=0) for g in groups]
        s_c = [_dot_nt(q4[g], ck_ref[b, :, gcols(g)]) for g in groups]
        s_n = [_dot_nt(q4[g], knb[:, gcols(g)]) for g in groups]
        p_c, p_n, den = [], [], []
        for g in groups:
            sc = jnp.where(cache_ok, s_c[g] * scale, -jnp.inf)
            sn = jnp.where(new_ok, s_n[g] * scale, -jnp.inf)
            m = jnp.maximum(jnp.maximum(jnp.max(sc, axis=-1, keepdims=True),
                                        jnp.max(sn, axis=-1, keepdims=True)), sinks[g])
            pc, pn = jnp.exp(sc - m), jnp.exp(sn - m)
            p_c.append(pc)
            p_n.append(pn)
            den.append(jnp.sum(pc, axis=-1, keepdims=True) + jnp.sum(pn, axis=-1, keepdims=True)
                       + jnp.exp(sinks[g] - m))
        o4 = [(_dot(p_c[g], cv_ref[b, :, gcols(g)]) + _dot(p_n[g], vnb[:, gcols(g)])) / den[g] for g in groups]
        for g in groups:
            for hh in range(KV_GROUP):
                cs = hcols(g, hh)
                gate = gbb[cs.start // half][:, cs.start % half:cs.start % half + HEAD_DIM]
                o_ref[pl.ds(r0, seq), cs] = _merge(mab[:, cs], gate, o4[g][hh * seq:(hh + 1) * seq])
        ok_ref[b, 0:w_buf - seq, :] = ck_ref[b, seq:w_buf, :]
        ok_ref[b, w_buf - seq:w_buf, :] = knb
        ov_ref[b, 0:w_buf - seq, :] = cv_ref[b, seq:w_buf, :]
        ov_ref[b, w_buf - seq:w_buf, :] = vnb
        return carry

    lax.fori_loop(0, n_b, per_sequence, 0)


def _swa_sample(proj, cache_k, cache_v, merged_a, sinks, *, row0, seq, k_col0, v_col0, gb_col0, nb=8):
    n_seq, w_buf, kv_cols = cache_k.shape
    q_cols = merged_a.shape[1]
    half = q_cols // 2
    tr = nb * seq
    blk0 = row0 // tr
    cache_spec = pl.BlockSpec((nb, w_buf, kv_cols), lambda i: (i, 0, 0))
    return pl.pallas_call(
        functools.partial(_swa_sample_kernel, seq=seq),
        grid=(n_seq // nb,),
        in_specs=[
            pl.BlockSpec(memory_space=pltpu.SMEM),
            pl.BlockSpec((tr, q_cols), lambda i: (blk0 + i, 0)),
            pl.BlockSpec((tr, kv_cols), lambda i: (blk0 + i, k_col0 // kv_cols)),
            pl.BlockSpec((tr, kv_cols), lambda i: (blk0 + i, v_col0 // kv_cols)),
            cache_spec, cache_spec,
            pl.BlockSpec((tr, q_cols), lambda i: (i, 0)),
            pl.BlockSpec((tr, half), lambda i: (blk0 + i, gb_col0 // half)),
            pl.BlockSpec((tr, half), lambda i: (blk0 + i, gb_col0 // half + 1)),
        ],
        out_specs=[pl.BlockSpec((tr, q_cols), lambda i: (i, 0)), cache_spec, cache_spec],
        out_shape=[jax.ShapeDtypeStruct((n_seq * seq, q_cols), BF16),
                   jax.ShapeDtypeStruct(cache_k.shape, F32),
                   jax.ShapeDtypeStruct(cache_v.shape, F32)],
        compiler_params=_cparams("parallel"),
        name="swa_sample",
    )(sinks, proj, proj, proj, cache_k, cache_v, merged_a, proj, proj)


def _outproj_kernel(x_ref, mp_ref, ms_ref, w_ref, o_ref, *, n_prompt_tiles):
    i = pl.program_id(0)

    @pl.when(i < n_prompt_tiles)
    def _():
        o_ref[...] = x_ref[...] + jnp.dot(mp_ref[...], w_ref[...], preferred_element_type=F32)

    @pl.when(i >= n_prompt_tiles)
    def _():
        o_ref[...] = x_ref[...] + jnp.dot(ms_ref[...], w_ref[...], preferred_element_type=F32)


def _outproj(x, merged_p, merged_s, w_out, *, tm=512, tn=1024):
    m, d = x.shape
    npt = merged_p.shape[0] // tm
    return pl.pallas_call(
        functools.partial(_outproj_kernel, n_prompt_tiles=npt),
        grid=(m // tm, d // tn),
        in_specs=[
            pl.BlockSpec((tm, tn), lambda i, j: (i, j)),
            pl.BlockSpec((tm, d), lambda i, j: (jnp.minimum(i, npt - 1), 0)),
            pl.BlockSpec((tm, d), lambda i, j: (jnp.maximum(i - npt, 0), 0)),
            pl.BlockSpec((d, tn), lambda i, j: (0, j)),
        ],
        out_specs=pl.BlockSpec((tm, tn), lambda i, j: (i, j)),
        out_shape=jax.ShapeDtypeStruct((m, d), F32),
        compiler_params=_cparams("parallel", "arbitrary"),
        name="outproj",
    )(x, merged_p, merged_s, w_out)


def _ple_kernel(xf_ref, x_ref, pe_ref, nw_ref, wp_ref, wg_ref, o_head_ref, o_tail_ref, xn_ref, *, n_head_tiles):
    i = pl.program_id(0)

    @pl.when(pl.program_id(1) == 0)
    def _():
        xn_ref[...] = _rms(xf_ref[...], nw_ref[...]).astype(BF16)

    gate = jnp.dot(xn_ref[...], wg_ref[...], preferred_element_type=F32)
    emb = jnp.dot(pe_ref[...], wp_ref[...], preferred_element_type=F32)
    out = x_ref[...] + emb * jax.nn.sigmoid(gate)

    @pl.when(i < n_head_tiles)
    def _():
        o_head_ref[...] = out

    @pl.when(i >= n_head_tiles)
    def _():
        o_tail_ref[...] = out


def _ple(x, pe, norm_w, w_proj, w_gate, *, head_rows, tm=512, tn=1024):
    m, d = x.shape
    pdim = pe.shape[1]
    nh = head_rows // tm
    nj = d // tn
    head_idx = lambda i, j: (jnp.minimum(i, nh - 1), jnp.where(i < nh, j, nj - 1))
    tail_idx = lambda i, j: (jnp.maximum(i - nh, 0), jnp.where(i >= nh, j, 0))
    return pl.pallas_call(
        functools.partial(_ple_kernel, n_head_tiles=nh),
        grid=(m // tm, nj),
        in_specs=[
            pl.BlockSpec((tm, d), lambda i, j: (i, 0), pipeline_mode=pl.Buffered(1)),
            pl.BlockSpec((tm, tn), lambda i, j: (i, j)),
            pl.BlockSpec((tm, pdim), lambda i, j: (i, 0)),
            pl.BlockSpec((1, d), lambda i, j: (0, 0)),
            pl.BlockSpec((pdim, tn), lambda i, j: (0, j)),
            pl.BlockSpec((d, tn), lambda i, j: (0, j)),
        ],
        out_specs=[pl.BlockSpec((tm, tn), head_idx), pl.BlockSpec((tm, tn), tail_idx)],
        out_shape=[jax.ShapeDtypeStruct((head_rows, d), F32), jax.ShapeDtypeStruct((m - head_rows, d), F32)],
        scratch_shapes=[pltpu.VMEM((tm, d), BF16)],
        compiler_params=_cparams("arbitrary", "arbitrary"),
        name="ple",
    )(x, x, pe, norm_w.reshape(1, d), w_proj, w_gate)


def _rope_tables(seq, dec_batch, dec_seq):
    half = HEAD_DIM // 2
    inv_freq = ROPE_THETA ** (-jnp.arange(half, dtype=F32) / half)
    pos_p = jnp.arange(seq, dtype=jnp.int32)
    pos_s = jnp.tile(PAST_LEN + jnp.arange(dec_seq, dtype=jnp.int32), dec_batch)
    ang = jnp.concatenate([pos_p, pos_s]).astype(F32)[:, None] * inv_freq[None, :]
    cos, sin = jnp.cos(ang), jnp.sin(ang)
    return jnp.concatenate([cos, cos], axis=1), jnp.concatenate([-sin, sin], axis=1)


def kernel(x_prompt, x_sample, p_prompt, p_sample, state_gdn, state_conv, cache_swa_k, cache_swa_v, ffn1_norm, ffn1_w_gate, ffn1_w_up, ffn1_w_down, mix_norm, w_in, conv_w, A_log, dt_bias, gdn_norm_w, q_norm_w, k_norm_w, sinks, w_out, ffn2_norm, ffn2_w_gate, ffn2_w_up, ffn2_w_down, ple_norm, w_ple_proj, w_ple_gate):
    depth, dec_batch, n_v_heads, dk, dv = state_gdn.shape
    assert depth == 1 and dk == HEAD_DIM and dv == HEAD_DIM
    bp, seq, d = x_prompt.shape
    assert bp == 1
    dec_seq = x_sample.shape[1]
    w_buf, n_kv_heads = cache_swa_k.shape[2], cache_swa_k.shape[3]
    n_heads = sinks.shape[1]
    assert n_heads == n_kv_heads * KV_GROUP and w_buf == WINDOW and CHUNK % dec_seq == 0
    conv_dim = conv_w.shape[2]
    key_dim = n_v_heads // GDN_REP * HEAD_DIM
    val_dim = n_v_heads * HEAD_DIM
    q_dim = n_heads * HEAD_DIM
    kv_cols = n_kv_heads * HEAD_DIM
    n_samp = dec_batch * dec_seq

    ab0 = conv_dim + val_dim
    b0 = ab0 + 2 * n_v_heads
    assert w_in.shape[2] == b0 + q_dim + 2 * kv_cols + 2 * d
    w_all = w_in[0].astype(BF16)
    w_b = w_all[:, b0:]
    n_ab = 128
    w_ab = jnp.pad(w_all[:, ab0:b0], ((0, 0), (0, n_ab - 2 * n_v_heads)))
    z_col0 = conv_dim
    k_col0, v_col0 = q_dim, q_dim + kv_cols
    ga_col0 = q_dim + 2 * kv_cols
    gb_col0 = ga_col0 + d

    bf = lambda w: w[0].astype(BF16)
    pe = jnp.concatenate([p_prompt.reshape(seq, PLE_DIM), p_sample.reshape(n_samp, PLE_DIM)], axis=0)

    x = _ffn([x_prompt.reshape(seq, d), x_sample.reshape(n_samp, d)], ffn1_norm[0],
             bf(ffn1_w_gate), bf(ffn1_w_up), bf(ffn1_w_down))
    proj_a, ab = _inproj(x, mix_norm[0], w_all, w_ab, n=ab0)
    cos, sin = _rope_tables(seq, dec_batch, dec_seq)
    proj_b = _inproj_rope(x, mix_norm[0], w_b, cos, sin, q_norm_w[0], k_norm_w[0], q_cols=q_dim, k_cols=kv_cols)

    carried = jnp.pad(state_conv[0], ((0, 0), (dec_seq - (CONV_W - 1), 0), (0, 0))).reshape(n_samp, conv_dim)
    ab_t = ab.T
    gdn_args = dict(n_heads=n_v_heads, z_col0=z_col0, ga_col0=ga_col0)
    ma_p, sg_p = _gdn_prompt(proj_a, proj_b, conv_w[0], ab, ab_t, A_log[0], dt_bias[0], gdn_norm_w[0],
                             rows=seq, **gdn_args)
    ma_s, sg_s = _gdn_sample(proj_a, proj_b, carried, conv_w[0], ab, ab_t, A_log[0], dt_bias[0], gdn_norm_w[0],
                             state_gdn[0], row0=seq, rows=n_samp, seq=dec_seq, **gdn_args)

    swa_args = dict(k_col0=k_col0, v_col0=v_col0, gb_col0=gb_col0)
    merged_p = _swa_prompt(proj_b, ma_p, sinks[0], rows=seq, n_kv_heads=n_kv_heads, **swa_args)
    merged_s, kk_s, vv_s = _swa_sample(proj_b, cache_swa_k[0].reshape(dec_batch, w_buf, kv_cols),
                                       cache_swa_v[0].reshape(dec_batch, w_buf, kv_cols),
                                       ma_s, sinks[0], row0=seq, seq=dec_seq, **swa_args)

    x = _outproj(x, merged_p, merged_s, bf(w_out))
    x = _ffn([x], ffn2_norm[0], bf(ffn2_w_gate), bf(ffn2_w_up), bf(ffn2_w_down))
    y_p, y_s = _ple(x, pe.astype(BF16), ple_norm[0], bf(w_ple_proj), bf(w_ple_gate), head_rows=seq)

    cache_shape = (1, 1, w_buf, n_kv_heads, HEAD_DIM)
    return (
        y_p.reshape(1, seq, d),
        y_s.reshape(dec_batch, dec_seq, d),
        sg_p.reshape(1, 1, n_v_heads, HEAD_DIM, HEAD_DIM),
        proj_a[seq - (CONV_W - 1):seq, :conv_dim].reshape(1, 1, CONV_W - 1, conv_dim),
        proj_b[seq - w_buf:seq, k_col0:k_col0 + kv_cols].reshape(cache_shape),
        proj_b[seq - w_buf:seq, v_col0:v_col0 + kv_cols].reshape(cache_shape),
        sg_s.reshape(1, dec_batch, n_v_heads, HEAD_DIM, HEAD_DIM),
        proj_a[seq:, :conv_dim].reshape(dec_batch, dec_seq, conv_dim)[:, dec_seq - (CONV_W - 1):].reshape(
            1, dec_batch, CONV_W - 1, conv_dim),
        kk_s.reshape(1, dec_batch, w_buf, n_kv_heads, HEAD_DIM),
        vv_s.reshape(1, dec_batch, w_buf, n_kv_heads, HEAD_DIM),
    )
```

```python
import functools

import jax
import jax.numpy as jnp
from jax import lax
from jax.experimental import pallas as pl
from jax.experimental.pallas import tpu as pltpu

F32 = jnp.float32
BF16 = jnp.bfloat16

EPS = 1e-6
HEAD_DIM = 128
KV_GROUP = 4
WINDOW = 128
ROPE_THETA = 10000.0
PAST_LEN = 8192
GDN_REP = 2
CONV_W = 4
CHUNK = 64
PLE_DIM = 256
GDN_STEP_HEADS = 4
PLE_OUT_CHUNK = 256
FFN_OUT_CHUNK = 1024
SWA_STEP_GROUPS = 4

V7X_VMEM_LIMIT_BYTES = 62 * 1024 * 1024


def _cparams(*sem):
    return pltpu.CompilerParams(dimension_semantics=sem, vmem_limit_bytes=V7X_VMEM_LIMIT_BYTES)


def _rms(x, w):
    return x * lax.rsqrt(jnp.mean(x * x, axis=-1, keepdims=True) + EPS) * w


def _dot(a, b):
    return jnp.dot(a.astype(BF16), b.astype(BF16), preferred_element_type=F32)


def _dot_nt(a, b):
    return lax.dot_general(a.astype(BF16), b.astype(BF16), (((1,), (1,)), ((), ())),
                           preferred_element_type=F32)


def _dot_tn(a, b):
    return lax.dot_general(a.astype(BF16), b.astype(BF16), (((0,), (0,)), ((), ())),
                           preferred_element_type=F32)


def _row_part_specs(parts, tm, width):
    specs, spans, start = [], [], 0
    for part in parts:
        n = part.shape[0] // tm
        specs.append(pl.BlockSpec(
            (tm, width), functools.partial(lambda i, j, s, n: (jnp.clip(i - s, 0, n - 1), 0), s=start, n=n),
            pipeline_mode=pl.Buffered(1)))
        spans.append((start, n))
        start += n
    return specs, spans


def _ffn_kernel(*refs, spans):
    n_parts = len(spans)
    x_refs = refs[:n_parts]
    nw_ref, wg_ref, wu_ref, wd_ref, o_ref, xn_ref = refs[n_parts:]
    i = pl.program_id(0)
    j = pl.program_id(1)
    for x_ref, (start, n) in zip(x_refs, spans):
        @pl.when((j == 0) & (i >= start) & (i < start + n))
        def _():
            x = x_ref[...]
            xn_ref[...] = _rms(x, nw_ref[...]).astype(BF16)
            o_ref[...] = x

    xn = xn_ref[...]
    g = jnp.dot(xn, wg_ref[...], preferred_element_type=F32)
    u = jnp.dot(xn, wu_ref[...], preferred_element_type=F32)
    h = ((0.5 * (g * jax.nn.sigmoid(g))) * u).astype(BF16)
    d = o_ref.shape[1]
    for c in range(0, d, FFN_OUT_CHUNK):
        cs = slice(c, min(c + FFN_OUT_CHUNK, d))
        o_ref[:, cs] += jnp.dot(h, wd_ref[:, cs], preferred_element_type=F32)


def _ffn(x_parts, norm_w, wg, wu, wd, *, tm=512, tf=256):
    d = x_parts[0].shape[1]
    m = sum(p.shape[0] for p in x_parts)
    f = wg.shape[1]
    x_specs, spans = _row_part_specs(x_parts, tm, d)
    return pl.pallas_call(
        functools.partial(_ffn_kernel, spans=spans),
        grid=(m // tm, f // tf),
        in_specs=[
            *x_specs,
            pl.BlockSpec((1, d), lambda i, j: (0, 0)),
            pl.BlockSpec((d, tf), lambda i, j: (0, j)),
            pl.BlockSpec((d, tf), lambda i, j: (0, j)),
            pl.BlockSpec((tf, d), lambda i, j: (j, 0)),
        ],
        out_specs=pl.BlockSpec((tm, d), lambda i, j: (i, 0)),
        out_shape=jax.ShapeDtypeStruct((m, d), F32),
        scratch_shapes=[pltpu.VMEM((tm, d), BF16)],
        compiler_params=_cparams("parallel", "arbitrary"),
        name="ffn",
    )(*x_parts, norm_w.reshape(1, d), wg, wu, wd)


def _inproj_ab_kernel(x_ref, nw_ref, w_ref, wab_ref, o_ref, ab_ref, xn_ref):
    @pl.when(pl.program_id(1) == 0)
    def _():
        xn = _rms(x_ref[...], nw_ref[...]).astype(BF16)
        xn_ref[...] = xn
        ab_ref[...] = jnp.dot(xn, wab_ref[...], preferred_element_type=F32)

    o_ref[...] = jnp.dot(xn_ref[...], w_ref[...].astype(BF16), preferred_element_type=F32)


def _inproj_rope_kernel(x_ref, nw_ref, w_ref, cos_ref, sin_ref, qw_ref, kw_ref, o_ref, xn_ref, *,
                        n_q_blocks, n_rope_blocks):
    j = pl.program_id(1)

    @pl.when(j == 0)
    def _():
        xn_ref[...] = _rms(x_ref[...], nw_ref[...]).astype(BF16)

    @pl.when(j < n_rope_blocks)
    def _():
        w = jnp.where(j < n_q_blocks, qw_ref[...], kw_ref[...])
        cos = cos_ref[...]
        sin = sin_ref[...]
        xn = xn_ref[...]
        step = 2 * HEAD_DIM
        for c in range(0, o_ref.shape[1], step):
            acc = jnp.dot(xn, w_ref[:, c:c + step], preferred_element_type=F32)
            for h in range(step // HEAD_DIM):
                y = _rms(acc[:, h * HEAD_DIM:(h + 1) * HEAD_DIM], w)
                cs = slice(c + h * HEAD_DIM, c + (h + 1) * HEAD_DIM)
                o_ref[:, cs] = y * cos + pltpu.roll(y, HEAD_DIM // 2, 1) * sin

    @pl.when(j >= n_rope_blocks)
    def _():
        o_ref[...] = jnp.dot(xn_ref[...], w_ref[...], preferred_element_type=F32)


def _inproj_rope(x, norm_w, w, cos, sin, q_norm_w, k_norm_w, *, q_cols, k_cols, tm=512, tn=1024):
    m, d = x.shape
    n = w.shape[1]
    row = lambda i, j: (i, 0)
    fixed = lambda i, j: (0, 0)
    return pl.pallas_call(
        functools.partial(_inproj_rope_kernel, n_q_blocks=q_cols // tn, n_rope_blocks=(q_cols + k_cols) // tn),
        grid=(m // tm, n // tn),
        in_specs=[
            pl.BlockSpec((tm, d), row, pipeline_mode=pl.Buffered(1)),
            pl.BlockSpec((1, d), fixed),
            pl.BlockSpec((d, tn), lambda i, j: (0, j)),
            pl.BlockSpec((tm, HEAD_DIM), row),
            pl.BlockSpec((tm, HEAD_DIM), row),
            pl.BlockSpec((1, HEAD_DIM), fixed),
            pl.BlockSpec((1, HEAD_DIM), fixed),
        ],
        out_specs=pl.BlockSpec((tm, tn), lambda i, j: (i, j)),
        out_shape=jax.ShapeDtypeStruct((m, n), F32),
        scratch_shapes=[pltpu.VMEM((tm, d), BF16)],
        compiler_params=_cparams("parallel", "arbitrary"),
        name="inproj_rope",
    )(x, norm_w.reshape(1, d), w, cos, sin, q_norm_w.reshape(1, HEAD_DIM), k_norm_w.reshape(1, HEAD_DIM))


def _inproj_ab(x, norm_w, w, w_ab, *, n, tm=1024, tn=256):
    m, d = x.shape
    nab = w_ab.shape[1]
    return pl.pallas_call(
        _inproj_ab_kernel,
        grid=(m // tm, n // tn),
        in_specs=[
            pl.BlockSpec((tm, d), lambda i, j: (i, 0), pipeline_mode=pl.Buffered(1)),
            pl.BlockSpec((1, d), lambda i, j: (0, 0)),
            pl.BlockSpec((d, tn), lambda i, j: (0, j)),
            pl.BlockSpec((d, nab), lambda i, j: (0, 0)),
        ],
        out_specs=[pl.BlockSpec((tm, tn), lambda i, j: (i, j)), pl.BlockSpec((tm, nab), lambda i, j: (i, 0))],
        out_shape=[jax.ShapeDtypeStruct((m, n), F32), jax.ShapeDtypeStruct((m, nab), F32)],
        scratch_shapes=[pltpu.VMEM((tm, d), BF16)],
        compiler_params=_cparams("parallel", "arbitrary"),
        name="inproj_ab",
    )(x, norm_w.reshape(1, d), w, w_ab)


def _conv_silu(x, w, shifted):
    y = None
    for tap in range(CONV_W):
        sh = CONV_W - 1 - tap
        term = (x if sh == 0 else shifted(sh)) * w[tap:tap + 1]
        y = term if y is None else y + term
    return y * jax.nn.sigmoid(y)


def _shift_in_sequence(x, halo):
    row8 = lax.broadcasted_iota(jnp.int32, (8, 1), 0)

    def shifted(sh):
        rolled = pltpu.roll(x, sh, 0)
        head = jnp.where(row8 < sh, pltpu.roll(halo, sh, 0), rolled[:8])
        return jnp.concatenate([head, rolled[8:]], axis=0)

    return shifted


def _shift_in_groups(x, carried, seq):
    rows = x.shape[0]
    t = lax.broadcasted_iota(jnp.int32, (rows, 1), 0) % seq
    return lambda sh: jnp.where(t >= sh, pltpu.roll(x, sh, 0), pltpu.roll(carried, rows + sh - seq, 0))


def _l2norm_heads(y, scale):
    heads = []
    for h in range(y.shape[1] // HEAD_DIM):
        yh = y[:, h * HEAD_DIM:(h + 1) * HEAD_DIM]
        heads.append(yh * (lax.rsqrt(jnp.sum(yh * yh, axis=-1, keepdims=True) + EPS) * scale))
    return jnp.concatenate(heads, axis=1)


def _gdn_activations(pre, conv_w, shifts):
    q, k, v = (_conv_silu(x, w, s) for x, w, s in zip(pre, conv_w, shifts))
    return _l2norm_heads(q, HEAD_DIM ** -0.5), _l2norm_heads(k, 1.0), v


def _softplus(x):
    return jnp.maximum(x, 0.0) + jnp.log1p(jnp.exp(-jnp.abs(x)))


def _seg_cumsum(x, pos, seg, axis):
    d = 1
    while d < seg:
        x = x + jnp.where(pos >= d, pltpu.roll(x, d, axis), 0.0)
        d *= 2
    return x


def _gates(ab, abt8, alog_row, dtb_row, alog_col8, dtb_col8, seg):
    rows = ab.shape[0]
    pos_c = lax.broadcasted_iota(jnp.int32, (rows, 1), 0) % seg
    pos_r = lax.broadcasted_iota(jnp.int32, (1, rows), 1) % seg
    g_all = -jnp.exp(alog_row) * _softplus(ab + dtb_row)
    g_t = -jnp.exp(alog_col8) * _softplus(abt8 + dtb_col8)
    return _seg_cumsum(g_all, pos_c, seg, 0), jax.nn.sigmoid(ab), _seg_cumsum(g_t, pos_r, seg, 1)


def _pick_lane(x, idx):
    lane = lax.broadcasted_iota(jnp.int32, (1, x.shape[1]), 1)
    return jnp.sum(jnp.where(lane == idx, x, 0.0), axis=1, keepdims=True)


def _pick_sublane(x, idx):
    sub = lax.broadcasted_iota(jnp.int32, (x.shape[0], 1), 0)
    return jnp.sum(jnp.where(sub == idx, x, 0.0), axis=0, keepdims=True)


def _head_gates(gate_vals, h, n_heads):
    gc_all, sig_all, gc_t = gate_vals
    return _pick_lane(sig_all, n_heads + h), _pick_lane(gc_all, h), _pick_sublane(gc_t, h % 8)


def _load_gates(ab_ref, abt_ref, alr_ref, dtr_ref, alc_ref, dtc_ref, h0, seg):
    h8 = pl.multiple_of((h0 // 8) * 8, 8)
    return _gates(ab_ref[...], abt_ref[pl.ds(h8, 8), :], alr_ref[...], dtr_ref[...],
                  alc_ref[pl.ds(h8, 8), :], dtc_ref[pl.ds(h8, 8), :], seg)


def _chunk_masks(seg):
    r = lax.broadcasted_iota(jnp.int32, (CHUNK, CHUNK), 0)
    c = lax.broadcasted_iota(jnp.int32, (CHUNK, CHUNK), 1)
    same = (r // seg) == (c // seg)
    return same & (c <= r), same & (c < r), r == c


def _gdn_intra(q_all, k_all, v_all, gates, seg):
    incl, strict, eye = _chunk_masks(seg)
    n_chunks = q_all.shape[0] // CHUNK
    n_heads = len(gates)
    chunks = range(n_chunks)
    heads = range(n_heads)
    items = [(n, hh) for n in chunks for hh in heads]
    rows = lambda n: slice(n * CHUNK, (n + 1) * CHUNK)
    cols = lambda j: slice(j * HEAD_DIM, (j + 1) * HEAD_DIM)
    pairs = [(n, j) for n in chunks for j in range(n_heads // GDN_REP)]
    q = {(n, j): q_all[rows(n), cols(j)] for n, j in pairs}
    k = {(n, j): k_all[rows(n), cols(j)] for n, j in pairs}
    kk_qk = {p: _dot_nt(jnp.concatenate([k[p], q[p]], axis=0), k[p]) for p in pairs}
    kk = {p: kk_qk[p][:CHUNK] for p in pairs}
    qk = {p: kk_qk[p][CHUNK:] for p in pairs}
    out = {}
    low, rhs = {}, {}
    for n, hh in items:
        beta, gc_col, gc_row = gates[hh]
        beta, gc_col, gc_row = beta[rows(n)], gc_col[rows(n)], gc_row[:, rows(n)]
        pj = (n, hh // GDN_REP)
        decay = jnp.exp(jnp.where(incl, gc_col - gc_row, -jnp.inf))
        low[n, hh] = jnp.where(strict, beta * kk[pj] * decay, 0.0)
        e_gc = jnp.exp(gc_col)
        rhs[n, hh] = jnp.concatenate([v_all[rows(n), cols(hh)] * beta, k[pj] * (beta * e_gc)], axis=1)
        out[n, hh] = dict(a=jnp.where(incl, qk[pj] * decay, 0.0), qd=q[pj] * e_gc, gc=gc_col, k=k[pj])
    p = {it: -low[it] for it in items}
    t = {it: jnp.where(eye, 1.0, p[it]) for it in items}
    if seg > 2:
        p = {it: _dot(p[it], p[it]) for it in items}
        n_pow = 4
        while n_pow < seg:
            both = {it: _dot(jnp.concatenate([t[it], p[it]], axis=0), p[it]) for it in items}
            t = {it: t[it] + both[it][:CHUNK] for it in items}
            p = {it: both[it][CHUNK:] for it in items}
            n_pow *= 2
        t = {it: t[it] + _dot(t[it], p[it]) for it in items}
    for it in items:
        uw = _dot(t[it], rhs[it])
        out[it]["u"] = uw[:, :HEAD_DIM]
        out[it]["w"] = uw[:, HEAD_DIM:]
    return out


def _gdn_emit(o, z, gate, nw):
    return (_rms(o, nw) * (z * jax.nn.sigmoid(z)) * jax.nn.sigmoid(gate)).astype(BF16)


def _gdn_prompt_kernel(q_ref, k_ref, v_ref, qh_ref, kh_ref, vh_ref, qw_ref, kw_ref, vw_ref, z_ref, ga_ref,
                       ab_ref, abt_ref, alr_ref, dtr_ref, alc_ref, dtc_ref, nw_ref, o_ref, sfin_ref, s_ref, *,
                       n_heads):
    h0 = pl.program_id(0) * GDN_STEP_HEADS
    i = pl.program_id(1)

    @pl.when(i == 0)
    def _():
        s_ref[...] = jnp.zeros_like(s_ref)

    nw = nw_ref[...]
    heads = range(GDN_STEP_HEADS)
    gate_vals = _load_gates(ab_ref, abt_ref, alr_ref, dtr_ref, alc_ref, dtc_ref, h0, CHUNK)
    gates = [_head_gates(gate_vals, h0 + hh, n_heads) for hh in heads]
    pre = [r[...] for r in (q_ref, k_ref, v_ref)]
    halos = [jnp.where(i > 0, r[...], 0.0) for r in (qh_ref, kh_ref, vh_ref)]
    taps = [r[...] for r in (qw_ref, kw_ref, vw_ref)]
    q, k, v = _gdn_activations(pre, taps, [_shift_in_sequence(x, h) for x, h in zip(pre, halos)])
    intra = _gdn_intra(q, k, v, gates, CHUNK)
    state = [s_ref[hh] for hh in heads]
    for n in range(q.shape[0] // CHUNK):
        rows = slice(n * CHUNK, (n + 1) * CHUNK)
        c = [intra[n, hh] for hh in heads]
        ws_qs = [_dot(jnp.concatenate([c[hh]["w"], c[hh]["qd"]], axis=0), state[hh]) for hh in heads]
        v_new = [c[hh]["u"] - ws_qs[hh][:CHUNK] for hh in heads]
        g_last = [c[hh]["gc"][CHUNK - 1:CHUNK] for hh in heads]
        k_st = [(c[hh]["k"] * jnp.exp(g_last[hh] - c[hh]["gc"])).T for hh in heads]
        av_kv = [_dot(jnp.concatenate([c[hh]["a"], k_st[hh]], axis=0), v_new[hh]) for hh in heads]
        o = [ws_qs[hh][CHUNK:] + av_kv[hh][:CHUNK] for hh in heads]
        state = [state[hh] * jnp.exp(g_last[hh]) + av_kv[hh][CHUNK:] for hh in heads]
        for hh in heads:
            cs = slice(hh * HEAD_DIM, (hh + 1) * HEAD_DIM)
            o_ref[rows, cs] = _gdn_emit(o[hh], z_ref[rows, cs], ga_ref[rows, cs], nw)
    for hh in heads:
        s_ref[hh] = state[hh]

    @pl.when(i == pl.num_programs(1) - 1)
    def _():
        sfin_ref[...] = s_ref[...]


def _gate_param_specs(n_lanes, idx):
    row = pl.BlockSpec((1, n_lanes), idx)
    col = pl.BlockSpec((n_lanes, 1), idx)
    return [row, row, col, col]


def _gate_params(alog, dtb, n_lanes):
    pad = lambda v: jnp.pad(v.astype(F32), (0, n_lanes - v.shape[0]))
    al, dt = pad(alog), pad(dtb)
    return al.reshape(1, n_lanes), dt.reshape(1, n_lanes), al.reshape(n_lanes, 1), dt.reshape(n_lanes, 1)


def _qkv_specs(block_rows, row_idx, head_idx, qw, vw, key_dim):
    return [
        pl.BlockSpec((block_rows, qw), lambda a, b: (row_idx(a, b), head_idx(a, b))),
        pl.BlockSpec((block_rows, qw), lambda a, b: (row_idx(a, b), key_dim // qw + head_idx(a, b))),
        pl.BlockSpec((block_rows, vw), lambda a, b: (row_idx(a, b), 2 * key_dim // vw + head_idx(a, b))),
    ]


def _gdn_prompt(proj_a, proj_b, conv_w, ab, ab_t, alog, dtb, norm_w, *, rows, n_heads, z_col0, ga_col0, tc=512):
    hs = GDN_STEP_HEADS
    qw = hs // GDN_REP * HEAD_DIM
    vw = hs * HEAD_DIM
    key_dim = n_heads // GDN_REP * HEAD_DIM
    n_lanes = ab.shape[1]
    tc8 = tc // 8
    head = lambda h, i: h
    return pl.pallas_call(
        functools.partial(_gdn_prompt_kernel, n_heads=n_heads),
        grid=(n_heads // hs, rows // tc),
        in_specs=[
            *_qkv_specs(tc, lambda h, i: i, head, qw, vw, key_dim),
            *_qkv_specs(8, lambda h, i: jnp.maximum(i * tc8 - 1, 0), head, qw, vw, key_dim),
            *_qkv_specs(CONV_W, lambda h, i: 0, head, qw, vw, key_dim),
            pl.BlockSpec((tc, vw), lambda h, i: (i, z_col0 // vw + h)),
            pl.BlockSpec((tc, vw), lambda h, i: (i, ga_col0 // vw + h)),
            pl.BlockSpec((tc, n_lanes), lambda h, i: (i, 0)),
            pl.BlockSpec((n_lanes, tc), lambda h, i: (0, i)),
            *_gate_param_specs(n_lanes, lambda h, i: (0, 0)),
            pl.BlockSpec((1, HEAD_DIM), lambda h, i: (0, 0)),
        ],
        out_specs=[
            pl.BlockSpec((tc, vw), lambda h, i: (i, h)),
            pl.BlockSpec((hs, HEAD_DIM, HEAD_DIM), lambda h, i: (h, 0, 0)),
        ],
        out_shape=[jax.ShapeDtypeStruct((rows, n_heads * HEAD_DIM), BF16),
                   jax.ShapeDtypeStruct((n_heads, HEAD_DIM, HEAD_DIM), F32)],
        scratch_shapes=[pltpu.VMEM((hs, HEAD_DIM, HEAD_DIM), F32)],
        compiler_params=_cparams("parallel", "arbitrary"),
        name="gdn_prompt",
    )(*[proj_a] * 6, *[conv_w] * 3, proj_a, proj_b, ab, ab_t, *_gate_params(alog, dtb, n_lanes),
      norm_w.reshape(1, HEAD_DIM))


def _gdn_sample_kernel(q_ref, k_ref, v_ref, qc_ref, kc_ref, vc_ref, qw_ref, kw_ref, vw_ref, z_ref, ga_ref,
                       ab_ref, abt_ref, alr_ref, dtr_ref, alc_ref, dtc_ref, nw_ref, s0_ref, o_ref, s1_ref, *,
                       n_heads, seq):
    h0 = pl.program_id(1) * GDN_STEP_HEADS
    nw = nw_ref[...]
    heads = range(GDN_STEP_HEADS)
    gate_vals = _load_gates(ab_ref, abt_ref, alr_ref, dtr_ref, alc_ref, dtc_ref, h0, seq)
    gates = [_head_gates(gate_vals, h0 + hh, n_heads) for hh in heads]
    pre = [r[...] for r in (q_ref, k_ref, v_ref)]
    carried = [r[...] for r in (qc_ref, kc_ref, vc_ref)]
    taps = [r[...] for r in (qw_ref, kw_ref, vw_ref)]
    q, k, v = _gdn_activations(pre, taps, [_shift_in_groups(x, c, seq) for x, c in zip(pre, carried)])
    intra = _gdn_intra(q, k, v, gates, seq)
    per_chunk = CHUNK // seq
    for n in range(q.shape[0] // CHUNK):
        rows = slice(n * CHUNK, (n + 1) * CHUNK)
        seqs = [(hh, b) for hh in heads for b in range(per_chunk)]
        sub = lambda b: slice(b * seq, (b + 1) * seq)
        c = [intra[n, hh] for hh in heads]
        s0 = {(hh, b): s0_ref[n * per_chunk + b, hh] for hh, b in seqs}
        ws_qs = {(hh, b): _dot(jnp.concatenate([c[hh]["w"][sub(b)], c[hh]["qd"][sub(b)]], axis=0), s0[hh, b])
                 for hh, b in seqs}
        v_new = {(hh, b): c[hh]["u"][sub(b)] - ws_qs[hh, b][:seq] for hh, b in seqs}
        for hh, b in seqs:
            gc = c[hh]["gc"][sub(b)]
            g_last = gc[seq - 1:seq]
            k_st = c[hh]["k"][sub(b)] * jnp.exp(g_last - gc)
            s1_ref[n * per_chunk + b, hh] = s0[hh, b] * jnp.exp(g_last) + _dot_tn(k_st, v_new[hh, b])
        for hh in heads:
            cs = slice(hh * HEAD_DIM, (hh + 1) * HEAD_DIM)
            qs = jnp.concatenate([ws_qs[hh, b][seq:] for b in range(per_chunk)], axis=0)
            vn = jnp.concatenate([v_new[hh, b] for b in range(per_chunk)], axis=0)
            out = _gdn_emit(qs + _dot(c[hh]["a"], vn), z_ref[rows, cs], ga_ref[rows, cs], nw)
            o_ref[rows, cs] = out.astype(o_ref.dtype)


def _gdn_sample(proj_a, proj_b, carried, conv_w, ab, ab_t, alog, dtb, norm_w, s0, *, row0, rows, seq, n_heads,
                z_col0, ga_col0, tc=128):
    hs = GDN_STEP_HEADS
    head = lambda i, h: h
    qw = hs // GDN_REP * HEAD_DIM
    vw = hs * HEAD_DIM
    key_dim = n_heads // GDN_REP * HEAD_DIM
    blk0 = row0 // tc
    n_lanes = ab.shape[1]
    state_spec = pl.BlockSpec((tc // seq, hs, HEAD_DIM, HEAD_DIM), lambda i, h: (i, h, 0, 0))
    return pl.pallas_call(
        functools.partial(_gdn_sample_kernel, n_heads=n_heads, seq=seq),
        grid=(rows // tc, n_heads // hs),
        in_specs=[
            *_qkv_specs(tc, lambda i, h: blk0 + i, head, qw, vw, key_dim),
            *_qkv_specs(tc, lambda i, h: i, head, qw, vw, key_dim),
            *_qkv_specs(CONV_W, lambda i, h: 0, head, qw, vw, key_dim),
            pl.BlockSpec((tc, vw), lambda i, h: (blk0 + i, z_col0 // vw + h)),
            pl.BlockSpec((tc, vw), lambda i, h: (blk0 + i, ga_col0 // vw + h)),
            pl.BlockSpec((tc, n_lanes), lambda i, h: (blk0 + i, 0)),
            pl.BlockSpec((n_lanes, tc), lambda i, h: (0, blk0 + i)),
            *_gate_param_specs(n_lanes, lambda i, h: (0, 0)),
            pl.BlockSpec((1, HEAD_DIM), lambda i, h: (0, 0)),
            state_spec,
        ],
        out_specs=[pl.BlockSpec((tc, vw), lambda i, h: (i, h)), state_spec],
        out_shape=[jax.ShapeDtypeStruct((rows, n_heads * HEAD_DIM), F32),
                   jax.ShapeDtypeStruct(s0.shape, F32)],
        compiler_params=_cparams("parallel", "parallel"),
        name="gdn_sample",
    )(*[proj_a] * 3, *[carried] * 3, *[conv_w] * 3, proj_a, proj_b, ab, ab_t, *_gate_params(alog, dtb, n_lanes),
      norm_w.reshape(1, HEAD_DIM), s0)


def _merge(other, gate, o):
    return (other.astype(F32) + jax.nn.sigmoid(gate) * o).astype(BF16)


def _swa_prompt_kernel(sinks_ref, q_ref, kc_ref, kp_ref, vc_ref, vp_ref, ma_ref, gb_ref, o_ref):
    g0 = pl.program_id(0) * SWA_STEP_GROUPS
    i = pl.program_id(1)
    rows = KV_GROUP * WINDOW
    r = lax.broadcasted_iota(jnp.int32, (rows, 2 * WINDOW), 0) % WINDOW
    c = lax.broadcasted_iota(jnp.int32, (rows, 2 * WINDOW), 1)
    visible = ((c < WINDOW) & (c > r) & (i > 0)) | ((c >= WINDOW) & (c - WINDOW <= r))
    head = lax.broadcasted_iota(jnp.int32, (rows, 1), 0) // WINDOW
    scale = HEAD_DIM ** -0.5
    groups = range(SWA_STEP_GROUPS)
    hcols = lambda g, hh: slice((g * KV_GROUP + hh) * HEAD_DIM, (g * KV_GROUP + hh + 1) * HEAD_DIM)
    gcols = lambda g: slice(g * HEAD_DIM, (g + 1) * HEAD_DIM)
    q4 = [jnp.concatenate([q_ref[:, hcols(g, hh)] for hh in range(KV_GROUP)], axis=0) for g in groups]
    kcat = [jnp.concatenate([kp_ref[:, gcols(g)], kc_ref[:, gcols(g)]], axis=0) for g in groups]
    vcat = [jnp.concatenate([vp_ref[:, gcols(g)], vc_ref[:, gcols(g)]], axis=0) for g in groups]
    s = [_dot_nt(q4[g], kcat[g]) for g in groups]
    p, den = [], []
    for g in groups:
        sink = jnp.zeros((rows, 1), F32)
        for hh in range(KV_GROUP):
            sink = jnp.where(head == hh, sinks_ref[(g0 + g) * KV_GROUP + hh], sink)
        sg = jnp.where(visible, s[g] * scale, -jnp.inf)
        m = jnp.maximum(jnp.max(sg, axis=-1, keepdims=True), sink)
        pg = jnp.exp(sg - m)
        p.append(pg)
        den.append(jnp.sum(pg, axis=-1, keepdims=True) + jnp.exp(sink - m))
    o4 = [_dot(p[g], vcat[g]) / den[g] for g in groups]
    for g in groups:
        for hh in range(KV_GROUP):
            cs = hcols(g, hh)
            o_ref[:, cs] = _merge(ma_ref[:, cs], gb_ref[:, cs], o4[g][hh * WINDOW:(hh + 1) * WINDOW])


def _swa_prompt(proj, merged_a, sinks, *, rows, n_kv_heads, k_col0, v_col0, gb_col0):
    sg = SWA_STEP_GROUPS
    qw = sg * KV_GROUP * HEAD_DIM
    kw = sg * HEAD_DIM
    prev = lambda i: jnp.maximum(i - 1, 0)
    return pl.pallas_call(
        _swa_prompt_kernel,
        grid=(n_kv_heads // sg, rows // WINDOW),
        in_specs=[
            pl.BlockSpec(memory_space=pltpu.SMEM),
            pl.BlockSpec((WINDOW, qw), lambda g, i: (i, g)),
            pl.BlockSpec((WINDOW, kw), lambda g, i: (i, k_col0 // kw + g)),
            pl.BlockSpec((WINDOW, kw), lambda g, i: (prev(i), k_col0 // kw + g)),
            pl.BlockSpec((WINDOW, kw), lambda g, i: (i, v_col0 // kw + g)),
            pl.BlockSpec((WINDOW, kw), lambda g, i: (prev(i), v_col0 // kw + g)),
            pl.BlockSpec((WINDOW, qw), lambda g, i: (i, g)),
            pl.BlockSpec((WINDOW, qw), lambda g, i: (i, gb_col0 // qw + g)),
        ],
        out_specs=pl.BlockSpec((WINDOW, qw), lambda g, i: (i, g)),
        out_shape=jax.ShapeDtypeStruct((rows, n_kv_heads * KV_GROUP * HEAD_DIM), BF16),
        compiler_params=_cparams("parallel", "parallel"),
        name="swa_prompt",
    )(sinks, proj, proj, proj, proj, proj, merged_a, proj)


def _swa_sample_kernel(sinks_ref, q_ref, kn_ref, vn_ref, ck_ref, cv_ref, ma_ref, gb_lo_ref, gb_hi_ref,
                       o_ref, ok_ref, ov_ref, *, seq):
    n_b, w_buf, kv_cols = ck_ref.shape
    half = gb_lo_ref.shape[1]
    rows = KV_GROUP * seq
    t = lax.broadcasted_iota(jnp.int32, (rows, 1), 0) % seq
    head = lax.broadcasted_iota(jnp.int32, (rows, 1), 0) // seq
    jc = lax.broadcasted_iota(jnp.int32, (1, w_buf), 1)
    jn = lax.broadcasted_iota(jnp.int32, (1, seq), 1)
    dist_c = t + w_buf - jc
    cache_ok = (dist_c >= 0) & (dist_c < WINDOW)
    new_ok = jn <= t
    scale = HEAD_DIM ** -0.5
    n_groups = kv_cols // HEAD_DIM
    sinks = []
    for g in range(n_groups):
        sink = jnp.zeros((rows, 1), F32)
        for hh in range(KV_GROUP):
            sink = jnp.where(head == hh, sinks_ref[g * KV_GROUP + hh], sink)
        sinks.append(sink)

    def per_sequence(b, carry):
        r0 = pl.multiple_of(b * seq, seq)
        qb = q_ref[pl.ds(r0, seq), :]
        knb = kn_ref[pl.ds(r0, seq), :]
        vnb = vn_ref[pl.ds(r0, seq), :]
        mab = ma_ref[pl.ds(r0, seq), :]
        gbb = (gb_lo_ref[pl.ds(r0, seq), :], gb_hi_ref[pl.ds(r0, seq), :])
        groups = range(n_groups)
        gcols = lambda g: slice(g * HEAD_DIM, (g + 1) * HEAD_DIM)
        hcols = lambda g, hh: slice((g * KV_GROUP + hh) * HEAD_DIM, (g * KV_GROUP + hh + 1) * HEAD_DIM)
        q4 = [jnp.concatenate([qb[:, hcols(g, hh)] for hh in range(KV_GROUP)], axis=0) for g in groups]
        s_c = [_dot_nt(q4[g], ck_ref[b, :, gcols(g)]) for g in groups]
        s_n = [_dot_nt(q4[g], knb[:, gcols(g)]) for g in groups]
        p_c, p_n, den = [], [], []
        for g in groups:
            sc = jnp.where(cache_ok, s_c[g] * scale, -jnp.inf)
            sn = jnp.where(new_ok, s_n[g] * scale, -jnp.inf)
            m = jnp.maximum(jnp.maximum(jnp.max(sc, axis=-1, keepdims=True),
                                        jnp.max(sn, axis=-1, keepdims=True)), sinks[g])
            pc, pn = jnp.exp(sc - m), jnp.exp(sn - m)
            p_c.append(pc)
            p_n.append(pn)
            den.append(jnp.sum(pc, axis=-1, keepdims=True) + jnp.sum(pn, axis=-1, keepdims=True)
                       + jnp.exp(sinks[g] - m))
        o4 = [(_dot(p_c[g], cv_ref[b, :, gcols(g)]) + _dot(p_n[g], vnb[:, gcols(g)])) / den[g] for g in groups]
        for g in groups:
            for hh in range(KV_GROUP):
                cs = hcols(g, hh)
                gate = gbb[cs.start // half][:, cs.start % half:cs.start % half + HEAD_DIM]
                o_ref[pl.ds(r0, seq), cs] = _merge(mab[:, cs], gate, o4[g][hh * seq:(hh + 1) * seq])
        ok_ref[b, 0:w_buf - seq, :] = ck_ref[b, seq:w_buf, :]
        ok_ref[b, w_buf - seq:w_buf, :] = knb
        ov_ref[b, 0:w_buf - seq, :] = cv_ref[b, seq:w_buf, :]
        ov_ref[b, w_buf - seq:w_buf, :] = vnb
        return carry

    lax.fori_loop(0, n_b, per_sequence, 0)


def _swa_sample(proj, cache_k, cache_v, merged_a, sinks, *, row0, seq, k_col0, v_col0, gb_col0, nb=8):
    n_seq, w_buf, kv_cols = cache_k.shape
    q_cols = merged_a.shape[1]
    half = q_cols // 2
    tr = nb * seq
    blk0 = row0 // tr
    cache_spec = pl.BlockSpec((nb, w_buf, kv_cols), lambda i: (i, 0, 0))
    return pl.pallas_call(
        functools.partial(_swa_sample_kernel, seq=seq),
        grid=(n_seq // nb,),
        in_specs=[
            pl.BlockSpec(memory_space=pltpu.SMEM),
            pl.BlockSpec((tr, q_cols), lambda i: (blk0 + i, 0)),
            pl.BlockSpec((tr, kv_cols), lambda i: (blk0 + i, k_col0 // kv_cols)),
            pl.BlockSpec((tr, kv_cols), lambda i: (blk0 + i, v_col0 // kv_cols)),
            cache_spec, cache_spec,
            pl.BlockSpec((tr, q_cols), lambda i: (i, 0)),
            pl.BlockSpec((tr, half), lambda i: (blk0 + i, gb_col0 // half)),
            pl.BlockSpec((tr, half), lambda i: (blk0 + i, gb_col0 // half + 1)),
        ],
        out_specs=[pl.BlockSpec((tr, q_cols), lambda i: (i, 0)), cache_spec, cache_spec],
        out_shape=[jax.ShapeDtypeStruct((n_seq * seq, q_cols), BF16),
                   jax.ShapeDtypeStruct(cache_k.shape, F32),
                   jax.ShapeDtypeStruct(cache_v.shape, F32)],
        compiler_params=_cparams("parallel"),
        name="swa_sample",
    )(sinks, proj, proj, proj, cache_k, cache_v, merged_a, proj, proj)


def _outproj_kernel(x_ref, mp_ref, ms_ref, w_ref, o_ref, *, n_prompt_tiles):
    i = pl.program_id(0)

    @pl.when(i < n_prompt_tiles)
    def _():
        o_ref[...] = x_ref[...] + jnp.dot(mp_ref[...], w_ref[...], preferred_element_type=F32)

    @pl.when(i >= n_prompt_tiles)
    def _():
        o_ref[...] = x_ref[...] + jnp.dot(ms_ref[...], w_ref[...], preferred_element_type=F32)


def _outproj(x, merged_p, merged_s, w_out, *, tm=512, tn=1024):
    m, d = x.shape
    npt = merged_p.shape[0] // tm
    return pl.pallas_call(
        functools.partial(_outproj_kernel, n_prompt_tiles=npt),
        grid=(m // tm, d // tn),
        in_specs=[
            pl.BlockSpec((tm, tn), lambda i, j: (i, j)),
            pl.BlockSpec((tm, d), lambda i, j: (jnp.minimum(i, npt - 1), 0)),
            pl.BlockSpec((tm, d), lambda i, j: (jnp.maximum(i - npt, 0), 0)),
            pl.BlockSpec((d, tn), lambda i, j: (0, j)),
        ],
        out_specs=pl.BlockSpec((tm, tn), lambda i, j: (i, j)),
        out_shape=jax.ShapeDtypeStruct((m, d), F32),
        compiler_params=_cparams("parallel", "arbitrary"),
        name="outproj",
    )(x, merged_p, merged_s, w_out)


def _ple_kernel(xf_ref, x_ref, pe_ref, nw_ref, wp_ref, wg_ref, o_head_ref, o_tail_ref, xn_ref, *, n_head_tiles):
    i = pl.program_id(0)

    @pl.when(pl.program_id(1) == 0)
    def _():
        xn_ref[...] = _rms(xf_ref[...], nw_ref[...]).astype(BF16)

    def emit(o_ref):
        xn = xn_ref[...]
        pe = pe_ref[...]
        for c in range(0, o_ref.shape[1], PLE_OUT_CHUNK):
            cs = slice(c, c + PLE_OUT_CHUNK)
            gate = jnp.dot(xn, wg_ref[:, cs], preferred_element_type=F32)
            emb = jnp.dot(pe, wp_ref[:, cs], preferred_element_type=F32)
            o_ref[:, cs] = x_ref[:, cs] + emb * jax.nn.sigmoid(gate)

    @pl.when(i < n_head_tiles)
    def _():
        emit(o_head_ref)

    @pl.when(i >= n_head_tiles)
    def _():
        emit(o_tail_ref)


def _ple(x, pe, norm_w, w_proj, w_gate, *, head_rows, tm=512, tn=1024):
    m, d = x.shape
    pdim = pe.shape[1]
    nh = head_rows // tm
    nj = d // tn
    head_idx = lambda i, j: (jnp.minimum(i, nh - 1), jnp.where(i < nh, j, nj - 1))
    tail_idx = lambda i, j: (jnp.maximum(i - nh, 0), jnp.where(i >= nh, j, 0))
    return pl.pallas_call(
        functools.partial(_ple_kernel, n_head_tiles=nh),
        grid=(m // tm, nj),
        in_specs=[
            pl.BlockSpec((tm, d), lambda i, j: (i, 0), pipeline_mode=pl.Buffered(1)),
            pl.BlockSpec((tm, tn), lambda i, j: (i, j)),
            pl.BlockSpec((tm, pdim), lambda i, j: (i, 0)),
            pl.BlockSpec((1, d), lambda i, j: (0, 0)),
            pl.BlockSpec((pdim, tn), lambda i, j: (0, j)),
            pl.BlockSpec((d, tn), lambda i, j: (0, j)),
        ],
        out_specs=[pl.BlockSpec((tm, tn), head_idx), pl.BlockSpec((tm, tn), tail_idx)],
        out_shape=[jax.ShapeDtypeStruct((head_rows, d), F32), jax.ShapeDtypeStruct((m - head_rows, d), F32)],
        scratch_shapes=[pltpu.VMEM((tm, d), BF16)],
        compiler_params=_cparams("arbitrary", "arbitrary"),
        name="ple",
    )(x, x, pe, norm_w.reshape(1, d), w_proj, w_gate)


def _rope_tables(seq, dec_batch, dec_seq):
    half = HEAD_DIM // 2
    inv_freq = ROPE_THETA ** (-jnp.arange(half, dtype=F32) / half)
    pos_p = jnp.arange(seq, dtype=jnp.int32)
    pos_s = jnp.tile(PAST_LEN + jnp.arange(dec_seq, dtype=jnp.int32), dec_batch)
    ang = jnp.concatenate([pos_p, pos_s]).astype(F32)[:, None] * inv_freq[None, :]
    cos, sin = jnp.cos(ang), jnp.sin(ang)
    return jnp.concatenate([cos, cos], axis=1), jnp.concatenate([-sin, sin], axis=1)


def kernel(x_prompt, x_sample, p_prompt, p_sample, state_gdn, state_conv, cache_swa_k, cache_swa_v, ffn1_norm, ffn1_w_gate, ffn1_w_up, ffn1_w_down, mix_norm, w_in, conv_w, A_log, dt_bias, gdn_norm_w, q_norm_w, k_norm_w, sinks, w_out, ffn2_norm, ffn2_w_gate, ffn2_w_up, ffn2_w_down, ple_norm, w_ple_proj, w_ple_gate):
    depth, dec_batch, n_v_heads, dk, dv = state_gdn.shape
    assert depth == 1 and dk == HEAD_DIM and dv == HEAD_DIM
    bp, seq, d = x_prompt.shape
    assert bp == 1
    dec_seq = x_sample.shape[1]
    w_buf, n_kv_heads = cache_swa_k.shape[2], cache_swa_k.shape[3]
    n_heads = sinks.shape[1]
    assert n_heads == n_kv_heads * KV_GROUP and w_buf == WINDOW and CHUNK % dec_seq == 0
    conv_dim = conv_w.shape[2]
    key_dim = n_v_heads // GDN_REP * HEAD_DIM
    val_dim = n_v_heads * HEAD_DIM
    q_dim = n_heads * HEAD_DIM
    kv_cols = n_kv_heads * HEAD_DIM
    n_samp = dec_batch * dec_seq

    ab0 = conv_dim + val_dim
    b0 = ab0 + 2 * n_v_heads
    assert w_in.shape[2] == b0 + q_dim + 2 * kv_cols + 2 * d
    w_b = w_in[0, :, b0:].astype(BF16)
    n_ab = 128
    w_ab = jnp.pad(w_in[0, :, ab0:b0].astype(BF16), ((0, 0), (0, n_ab - 2 * n_v_heads)))
    z_col0 = conv_dim
    k_col0, v_col0 = q_dim, q_dim + kv_cols
    ga_col0 = q_dim + 2 * kv_cols
    gb_col0 = ga_col0 + d

    bf = lambda w: w[0].astype(BF16)
    pe = jnp.concatenate([p_prompt.reshape(seq, PLE_DIM), p_sample.reshape(n_samp, PLE_DIM)], axis=0)

    x = _ffn([x_prompt.reshape(seq, d), x_sample.reshape(n_samp, d)], ffn1_norm[0],
             bf(ffn1_w_gate), bf(ffn1_w_up), bf(ffn1_w_down))
    proj_a, ab = _inproj_ab(x, mix_norm[0], w_in[0], w_ab, n=ab0)
    cos, sin = _rope_tables(seq, dec_batch, dec_seq)
    proj_b = _inproj_rope(x, mix_norm[0], w_b, cos, sin, q_norm_w[0], k_norm_w[0], q_cols=q_dim, k_cols=kv_cols)

    carried = jnp.pad(state_conv[0], ((0, 0), (dec_seq - (CONV_W - 1), 0), (0, 0))).reshape(n_samp, conv_dim)
    ab_t = ab.T
    gdn_args = dict(n_heads=n_v_heads, z_col0=z_col0, ga_col0=ga_col0)
    ma_p, sg_p = _gdn_prompt(proj_a, proj_b, conv_w[0], ab, ab_t, A_log[0], dt_bias[0], gdn_norm_w[0],
                             rows=seq, **gdn_args)
    ma_s, sg_s = _gdn_sample(proj_a, proj_b, carried, conv_w[0], ab, ab_t, A_log[0], dt_bias[0], gdn_norm_w[0],
                             state_gdn[0], row0=seq, rows=n_samp, seq=dec_seq, **gdn_args)

    swa_args = dict(k_col0=k_col0, v_col0=v_col0, gb_col0=gb_col0)
    merged_p = _swa_prompt(proj_b, ma_p, sinks[0], rows=seq, n_kv_heads=n_kv_heads, **swa_args)
    merged_s, kk_s, vv_s = _swa_sample(proj_b, cache_swa_k[0].reshape(dec_batch, w_buf, kv_cols),
                                       cache_swa_v[0].reshape(dec_batch, w_buf, kv_cols),
                                       ma_s, sinks[0], row0=seq, seq=dec_seq, **swa_args)

    x = _outproj(x, merged_p, merged_s, bf(w_out))
    x = _ffn([x], ffn2_norm[0], bf(ffn2_w_gate), bf(ffn2_w_up), bf(ffn2_w_down))
    y_p, y_s = _ple(x, pe.astype(BF16), ple_norm[0], bf(w_ple_proj), bf(w_ple_gate), head_rows=seq)

    cache_shape = (1, 1, w_buf, n_kv_heads, HEAD_DIM)
    return (
        y_p.reshape(1, seq, d),
        y_s.reshape(dec_batch, dec_seq, d),
        sg_p.reshape(1, 1, n_v_heads, HEAD_DIM, HEAD_DIM),
        proj_a[seq - (CONV_W - 1):seq, :conv_dim].reshape(1, 1, CONV_W - 1, conv_dim),
        proj_b[seq - w_buf:seq, k_col0:k_col0 + kv_cols].reshape(cache_shape),
        proj_b[seq - w_buf:seq, v_col0:v_col0 + kv_cols].reshape(cache_shape),
        sg_s.reshape(1, dec_batch, n_v_heads, HEAD_DIM, HEAD_DIM),
        proj_a[seq:, :conv_dim].reshape(dec_batch, dec_seq, conv_dim)[:, dec_seq - (CONV_W - 1):].reshape(
            1, dec_batch, CONV_W - 1, conv_dim),
        kk_s.reshape(1, dec_batch, w_buf, n_kv_heads, HEAD_DIM),
        vv_s.reshape(1, dec_batch, w_buf, n_kv_heads, HEAD_DIM),
    )
```

```python
import functools

import jax
import jax.numpy as jnp
from jax import lax
from jax.experimental import pallas as pl
from jax.experimental.pallas import tpu as pltpu

F32 = jnp.float32
BF16 = jnp.bfloat16

EPS = 1e-6
HEAD_DIM = 128
KV_GROUP = 4
WINDOW = 128
ROPE_THETA = 10000.0
PAST_LEN = 8192
GDN_REP = 2
CONV_W = 4
CHUNK = 64
PLE_DIM = 256
GDN_STEP_HEADS = 4
PLE_OUT_CHUNK = 256
FFN_OUT_CHUNK = 1024
SWA_STEP_GROUPS = 4

V7X_VMEM_LIMIT_BYTES = 62 * 1024 * 1024


def _cparams(*sem):
    return pltpu.CompilerParams(dimension_semantics=sem, vmem_limit_bytes=V7X_VMEM_LIMIT_BYTES)


def _rms(x, w):
    return x * lax.rsqrt(jnp.mean(x * x, axis=-1, keepdims=True) + EPS) * w


def _dot(a, b):
    return jnp.dot(a.astype(BF16), b.astype(BF16), preferred_element_type=F32)


def _dot_nt(a, b):
    return lax.dot_general(a.astype(BF16), b.astype(BF16), (((1,), (1,)), ((), ())),
                           preferred_element_type=F32)


def _dot_tn(a, b):
    return lax.dot_general(a.astype(BF16), b.astype(BF16), (((0,), (0,)), ((), ())),
                           preferred_element_type=F32)


def _row_part_specs(parts, tm, width):
    specs, spans, start = [], [], 0
    for part in parts:
        n = part.shape[0] // tm
        specs.append(pl.BlockSpec(
            (tm, width), functools.partial(lambda i, j, s, n: (jnp.clip(i - s, 0, n - 1), 0), s=start, n=n),
            pipeline_mode=pl.Buffered(1)))
        spans.append((start, n))
        start += n
    return specs, spans


def _ffn_kernel(*refs, spans):
    n_parts = len(spans)
    x_refs = refs[:n_parts]
    nw_ref, wg_ref, wu_ref, wd_ref, o_ref, xn_ref = refs[n_parts:]
    i = pl.program_id(0)
    j = pl.program_id(1)
    for x_ref, (start, n) in zip(x_refs, spans):
        @pl.when((j == 0) & (i >= start) & (i < start + n))
        def _():
            x = x_ref[...]
            xn_ref[...] = _rms(x, nw_ref[...]).astype(BF16)
            o_ref[...] = x

    xn = xn_ref[...]
    g = jnp.dot(xn, wg_ref[...], preferred_element_type=F32)
    u = jnp.dot(xn, wu_ref[...], preferred_element_type=F32)
    h = ((0.5 * (g * jax.nn.sigmoid(g))) * u).astype(BF16)
    d = o_ref.shape[1]
    for c in range(0, d, FFN_OUT_CHUNK):
        cs = slice(c, min(c + FFN_OUT_CHUNK, d))
        o_ref[:, cs] += jnp.dot(h, wd_ref[:, cs], preferred_element_type=F32)


def _ffn(x_parts, norm_w, wg, wu, wd, *, tm=512, tf=256):
    d = x_parts[0].shape[1]
    m = sum(p.shape[0] for p in x_parts)
    f = wg.shape[1]
    x_specs, spans = _row_part_specs(x_parts, tm, d)
    return pl.pallas_call(
        functools.partial(_ffn_kernel, spans=spans),
        grid=(m // tm, f // tf),
        in_specs=[
            *x_specs,
            pl.BlockSpec((1, d), lambda i, j: (0, 0)),
            pl.BlockSpec((d, tf), lambda i, j: (0, j)),
            pl.BlockSpec((d, tf), lambda i, j: (0, j)),
            pl.BlockSpec((tf, d), lambda i, j: (j, 0)),
        ],
        out_specs=pl.BlockSpec((tm, d), lambda i, j: (i, 0)),
        out_shape=jax.ShapeDtypeStruct((m, d), F32),
        scratch_shapes=[pltpu.VMEM((tm, d), BF16)],
        compiler_params=_cparams("parallel", "arbitrary"),
        name="ffn",
    )(*x_parts, norm_w.reshape(1, d), wg, wu, wd)


def _inproj_ab_kernel(x_ref, nw_ref, w_ref, wab_ref, o_ref, ab_ref, xn_ref):
    @pl.when(pl.program_id(1) == 0)
    def _():
        xn = _rms(x_ref[...], nw_ref[...]).astype(BF16)
        xn_ref[...] = xn
        ab_ref[...] = jnp.dot(xn, wab_ref[...], preferred_element_type=F32)

    o_ref[...] = jnp.dot(xn_ref[...], w_ref[...], preferred_element_type=F32)


def _inproj_rope_kernel(x_ref, nw_ref, w_ref, cos_ref, sin_ref, qw_ref, kw_ref, o_ref, xn_ref, *,
                        n_q_blocks, n_rope_blocks):
    j = pl.program_id(1)

    @pl.when(j == 0)
    def _():
        xn_ref[...] = _rms(x_ref[...], nw_ref[...]).astype(BF16)

    @pl.when(j < n_rope_blocks)
    def _():
        w = jnp.where(j < n_q_blocks, qw_ref[...], kw_ref[...])
        cos = cos_ref[...]
        sin = sin_ref[...]
        xn = xn_ref[...]
        step = 2 * HEAD_DIM
        for c in range(0, o_ref.shape[1], step):
            acc = jnp.dot(xn, w_ref[:, c:c + step], preferred_element_type=F32)
            for h in range(step // HEAD_DIM):
                y = _rms(acc[:, h * HEAD_DIM:(h + 1) * HEAD_DIM], w)
                cs = slice(c + h * HEAD_DIM, c + (h + 1) * HEAD_DIM)
                o_ref[:, cs] = y * cos + pltpu.roll(y, HEAD_DIM // 2, 1) * sin

    @pl.when(j >= n_rope_blocks)
    def _():
        o_ref[...] = jnp.dot(xn_ref[...], w_ref[...], preferred_element_type=F32)


def _inproj_rope(x, norm_w, w, cos, sin, q_norm_w, k_norm_w, *, q_cols, k_cols, tm=512, tn=1024):
    m, d = x.shape
    n = w.shape[1]
    row = lambda i, j: (i, 0)
    fixed = lambda i, j: (0, 0)
    return pl.pallas_call(
        functools.partial(_inproj_rope_kernel, n_q_blocks=q_cols // tn, n_rope_blocks=(q_cols + k_cols) // tn),
        grid=(m // tm, n // tn),
        in_specs=[
            pl.BlockSpec((tm, d), row, pipeline_mode=pl.Buffered(1)),
            pl.BlockSpec((1, d), fixed),
            pl.BlockSpec((d, tn), lambda i, j: (0, j)),
            pl.BlockSpec((tm, HEAD_DIM), row),
            pl.BlockSpec((tm, HEAD_DIM), row),
            pl.BlockSpec((1, HEAD_DIM), fixed),
            pl.BlockSpec((1, HEAD_DIM), fixed),
        ],
        out_specs=pl.BlockSpec((tm, tn), lambda i, j: (i, j)),
        out_shape=jax.ShapeDtypeStruct((m, n), F32),
        scratch_shapes=[pltpu.VMEM((tm, d), BF16)],
        compiler_params=_cparams("parallel", "arbitrary"),
        name="inproj_rope",
    )(x, norm_w.reshape(1, d), w, cos, sin, q_norm_w.reshape(1, HEAD_DIM), k_norm_w.reshape(1, HEAD_DIM))


def _inproj_ab(x, norm_w, w, w_ab, *, n, tm=512, tn=1024):
    m, d = x.shape
    nab = w_ab.shape[1]
    return pl.pallas_call(
        _inproj_ab_kernel,
        grid=(m // tm, n // tn),
        in_specs=[
            pl.BlockSpec((tm, d), lambda i, j: (i, 0), pipeline_mode=pl.Buffered(1)),
            pl.BlockSpec((1, d), lambda i, j: (0, 0)),
            pl.BlockSpec((d, tn), lambda i, j: (0, j)),
            pl.BlockSpec((d, nab), lambda i, j: (0, 0)),
        ],
        out_specs=[pl.BlockSpec((tm, tn), lambda i, j: (i, j)), pl.BlockSpec((tm, nab), lambda i, j: (i, 0))],
        out_shape=[jax.ShapeDtypeStruct((m, n), F32), jax.ShapeDtypeStruct((m, nab), F32)],
        scratch_shapes=[pltpu.VMEM((tm, d), BF16)],
        compiler_params=_cparams("parallel", "arbitrary"),
        name="inproj_ab",
    )(x, norm_w.reshape(1, d), w, w_ab)


def _conv_silu(x, w, shifted):
    y = None
    for tap in range(CONV_W):
        sh = CONV_W - 1 - tap
        term = (x if sh == 0 else shifted(sh)) * w[tap:tap + 1]
        y = term if y is None else y + term
    return y * jax.nn.sigmoid(y)


def _shift_in_sequence(x, halo):
    row8 = lax.broadcasted_iota(jnp.int32, (8, 1), 0)

    def shifted(sh):
        rolled = pltpu.roll(x, sh, 0)
        head = jnp.where(row8 < sh, pltpu.roll(halo, sh, 0), rolled[:8])
        return jnp.concatenate([head, rolled[8:]], axis=0)

    return shifted


def _shift_in_groups(x, carried, seq):
    rows = x.shape[0]
    t = lax.broadcasted_iota(jnp.int32, (rows, 1), 0) % seq
    return lambda sh: jnp.where(t >= sh, pltpu.roll(x, sh, 0), pltpu.roll(carried, rows + sh - seq, 0))


def _l2norm_heads(y, scale):
    heads = []
    for h in range(y.shape[1] // HEAD_DIM):
        yh = y[:, h * HEAD_DIM:(h + 1) * HEAD_DIM]
        heads.append(yh * (lax.rsqrt(jnp.sum(yh * yh, axis=-1, keepdims=True) + EPS) * scale))
    return jnp.concatenate(heads, axis=1)


def _gdn_activations(pre, conv_w, shifts):
    q, k, v = (_conv_silu(x, w, s) for x, w, s in zip(pre, conv_w, shifts))
    return _l2norm_heads(q, HEAD_DIM ** -0.5), _l2norm_heads(k, 1.0), v


def _softplus(x):
    return jnp.maximum(x, 0.0) + jnp.log1p(jnp.exp(-jnp.abs(x)))


def _seg_cumsum(x, pos, seg, axis):
    d = 1
    while d < seg:
        x = x + jnp.where(pos >= d, pltpu.roll(x, d, axis), 0.0)
        d *= 2
    return x


def _gates(ab, abt8, alog_row, dtb_row, alog_col8, dtb_col8, seg):
    rows = ab.shape[0]
    pos_c = lax.broadcasted_iota(jnp.int32, (rows, 1), 0) % seg
    pos_r = lax.broadcasted_iota(jnp.int32, (1, rows), 1) % seg
    g_all = -jnp.exp(alog_row) * _softplus(ab + dtb_row)
    g_t = -jnp.exp(alog_col8) * _softplus(abt8 + dtb_col8)
    return _seg_cumsum(g_all, pos_c, seg, 0), jax.nn.sigmoid(ab), _seg_cumsum(g_t, pos_r, seg, 1)


def _pick_lane(x, idx):
    lane = lax.broadcasted_iota(jnp.int32, (1, x.shape[1]), 1)
    return jnp.sum(jnp.where(lane == idx, x, 0.0), axis=1, keepdims=True)


def _pick_sublane(x, idx):
    sub = lax.broadcasted_iota(jnp.int32, (x.shape[0], 1), 0)
    return jnp.sum(jnp.where(sub == idx, x, 0.0), axis=0, keepdims=True)


def _head_gates(gate_vals, h, n_heads):
    gc_all, sig_all, gc_t = gate_vals
    return _pick_lane(sig_all, n_heads + h), _pick_lane(gc_all, h), _pick_sublane(gc_t, h % 8)


def _load_gates(ab_ref, abt_ref, alr_ref, dtr_ref, alc_ref, dtc_ref, h0, seg):
    h8 = pl.multiple_of((h0 // 8) * 8, 8)
    return _gates(ab_ref[...], abt_ref[pl.ds(h8, 8), :], alr_ref[...], dtr_ref[...],
                  alc_ref[pl.ds(h8, 8), :], dtc_ref[pl.ds(h8, 8), :], seg)


def _chunk_masks(seg):
    r = lax.broadcasted_iota(jnp.int32, (CHUNK, CHUNK), 0)
    c = lax.broadcasted_iota(jnp.int32, (CHUNK, CHUNK), 1)
    same = (r // seg) == (c // seg)
    return same & (c <= r), same & (c < r), r == c


def _gdn_intra(q_all, k_all, v_all, gates, seg):
    incl, strict, eye = _chunk_masks(seg)
    n_chunks = q_all.shape[0] // CHUNK
    n_heads = len(gates)
    chunks = range(n_chunks)
    heads = range(n_heads)
    items = [(n, hh) for n in chunks for hh in heads]
    rows = lambda n: slice(n * CHUNK, (n + 1) * CHUNK)
    cols = lambda j: slice(j * HEAD_DIM, (j + 1) * HEAD_DIM)
    pairs = [(n, j) for n in chunks for j in range(n_heads // GDN_REP)]
    q = {(n, j): q_all[rows(n), cols(j)] for n, j in pairs}
    k = {(n, j): k_all[rows(n), cols(j)] for n, j in pairs}
    kk_qk = {p: _dot_nt(jnp.concatenate([k[p], q[p]], axis=0), k[p]) for p in pairs}
    kk = {p: kk_qk[p][:CHUNK] for p in pairs}
    qk = {p: kk_qk[p][CHUNK:] for p in pairs}
    out = {}
    low, rhs = {}, {}
    for n, hh in items:
        beta, gc_col, gc_row = gates[hh]
        beta, gc_col, gc_row = beta[rows(n)], gc_col[rows(n)], gc_row[:, rows(n)]
        pj = (n, hh // GDN_REP)
        decay = jnp.exp(jnp.where(incl, gc_col - gc_row, -jnp.inf))
        low[n, hh] = jnp.where(strict, beta * kk[pj] * decay, 0.0)
        e_gc = jnp.exp(gc_col)
        rhs[n, hh] = jnp.concatenate([v_all[rows(n), cols(hh)] * beta, k[pj] * (beta * e_gc)], axis=1)
        out[n, hh] = dict(a=jnp.where(incl, qk[pj] * decay, 0.0), qd=q[pj] * e_gc, gc=gc_col, k=k[pj])
    p = {it: -low[it] for it in items}
    t = {it: jnp.where(eye, 1.0, p[it]) for it in items}
    if seg > 2:
        p = {it: _dot(p[it], p[it]) for it in items}
        n_pow = 4
        while n_pow < seg:
            both = {it: _dot(jnp.concatenate([t[it], p[it]], axis=0), p[it]) for it in items}
            t = {it: t[it] + both[it][:CHUNK] for it in items}
            p = {it: both[it][CHUNK:] for it in items}
            n_pow *= 2
        t = {it: t[it] + _dot(t[it], p[it]) for it in items}
    for it in items:
        uw = _dot(t[it], rhs[it])
        out[it]["u"] = uw[:, :HEAD_DIM]
        out[it]["w"] = uw[:, HEAD_DIM:]
    return out


def _gdn_emit(o, z, gate, nw):
    return (_rms(o, nw) * (z * jax.nn.sigmoid(z)) * jax.nn.sigmoid(gate)).astype(BF16)


def _gdn_prompt_kernel(q_ref, k_ref, v_ref, qh_ref, kh_ref, vh_ref, qw_ref, kw_ref, vw_ref, z_ref, ga_ref,
                       ab_ref, abt_ref, alr_ref, dtr_ref, alc_ref, dtc_ref, nw_ref, o_ref, sfin_ref, s_ref, *,
                       n_heads):
    h0 = pl.program_id(0) * GDN_STEP_HEADS
    i = pl.program_id(1)

    @pl.when(i == 0)
    def _():
        s_ref[...] = jnp.zeros_like(s_ref)

    nw = nw_ref[...]
    heads = range(GDN_STEP_HEADS)
    gate_vals = _load_gates(ab_ref, abt_ref, alr_ref, dtr_ref, alc_ref, dtc_ref, h0, CHUNK)
    gates = [_head_gates(gate_vals, h0 + hh, n_heads) for hh in heads]
    pre = [r[...] for r in (q_ref, k_ref, v_ref)]
    halos = [jnp.where(i > 0, r[...], 0.0) for r in (qh_ref, kh_ref, vh_ref)]
    taps = [r[...] for r in (qw_ref, kw_ref, vw_ref)]
    q, k, v = _gdn_activations(pre, taps, [_shift_in_sequence(x, h) for x, h in zip(pre, halos)])
    intra = _gdn_intra(q, k, v, gates, CHUNK)
    state = [s_ref[hh] for hh in heads]
    for n in range(q.shape[0] // CHUNK):
        rows = slice(n * CHUNK, (n + 1) * CHUNK)
        c = [intra[n, hh] for hh in heads]
        ws_qs = [_dot(jnp.concatenate([c[hh]["w"], c[hh]["qd"]], axis=0), state[hh]) for hh in heads]
        v_new = [c[hh]["u"] - ws_qs[hh][:CHUNK] for hh in heads]
        g_last = [c[hh]["gc"][CHUNK - 1:CHUNK] for hh in heads]
        k_st = [(c[hh]["k"] * jnp.exp(g_last[hh] - c[hh]["gc"])).T for hh in heads]
        av_kv = [_dot(jnp.concatenate([c[hh]["a"], k_st[hh]], axis=0), v_new[hh]) for hh in heads]
        o = [ws_qs[hh][CHUNK:] + av_kv[hh][:CHUNK] for hh in heads]
        state = [state[hh] * jnp.exp(g_last[hh]) + av_kv[hh][CHUNK:] for hh in heads]
        for hh in heads:
            cs = slice(hh * HEAD_DIM, (hh + 1) * HEAD_DIM)
            o_ref[rows, cs] = _gdn_emit(o[hh], z_ref[rows, cs], ga_ref[rows, cs], nw)
    for hh in heads:
        s_ref[hh] = state[hh]

    @pl.when(i == pl.num_programs(1) - 1)
    def _():
        sfin_ref[...] = s_ref[...]


def _gate_param_specs(n_lanes, idx):
    row = pl.BlockSpec((1, n_lanes), idx)
    col = pl.BlockSpec((n_lanes, 1), idx)
    return [row, row, col, col]


def _gate_params(alog, dtb, n_lanes):
    pad = lambda v: jnp.pad(v.astype(F32), (0, n_lanes - v.shape[0]))
    al, dt = pad(alog), pad(dtb)
    return al.reshape(1, n_lanes), dt.reshape(1, n_lanes), al.reshape(n_lanes, 1), dt.reshape(n_lanes, 1)


def _qkv_specs(block_rows, row_idx, head_idx, qw, vw, key_dim):
    return [
        pl.BlockSpec((block_rows, qw), lambda a, b: (row_idx(a, b), head_idx(a, b))),
        pl.BlockSpec((block_rows, qw), lambda a, b: (row_idx(a, b), key_dim // qw + head_idx(a, b))),
        pl.BlockSpec((block_rows, vw), lambda a, b: (row_idx(a, b), 2 * key_dim // vw + head_idx(a, b))),
    ]


def _gdn_prompt(proj_a, proj_b, conv_w, ab, ab_t, alog, dtb, norm_w, *, rows, n_heads, z_col0, ga_col0, tc=512):
    hs = GDN_STEP_HEADS
    qw = hs // GDN_REP * HEAD_DIM
    vw = hs * HEAD_DIM
    key_dim = n_heads // GDN_REP * HEAD_DIM
    n_lanes = ab.shape[1]
    tc8 = tc // 8
    head = lambda h, i: h
    return pl.pallas_call(
        functools.partial(_gdn_prompt_kernel, n_heads=n_heads),
        grid=(n_heads // hs, rows // tc),
        in_specs=[
            *_qkv_specs(tc, lambda h, i: i, head, qw, vw, key_dim),
            *_qkv_specs(8, lambda h, i: jnp.maximum(i * tc8 - 1, 0), head, qw, vw, key_dim),
            *_qkv_specs(CONV_W, lambda h, i: 0, head, qw, vw, key_dim),
            pl.BlockSpec((tc, vw), lambda h, i: (i, z_col0 // vw + h)),
            pl.BlockSpec((tc, vw), lambda h, i: (i, ga_col0 // vw + h)),
            pl.BlockSpec((tc, n_lanes), lambda h, i: (i, 0)),
            pl.BlockSpec((n_lanes, tc), lambda h, i: (0, i)),
            *_gate_param_specs(n_lanes, lambda h, i: (0, 0)),
            pl.BlockSpec((1, HEAD_DIM), lambda h, i: (0, 0)),
        ],
        out_specs=[
            pl.BlockSpec((tc, vw), lambda h, i: (i, h)),
            pl.BlockSpec((hs, HEAD_DIM, HEAD_DIM), lambda h, i: (h, 0, 0)),
        ],
        out_shape=[jax.ShapeDtypeStruct((rows, n_heads * HEAD_DIM), BF16),
                   jax.ShapeDtypeStruct((n_heads, HEAD_DIM, HEAD_DIM), F32)],
        scratch_shapes=[pltpu.VMEM((hs, HEAD_DIM, HEAD_DIM), F32)],
        compiler_params=_cparams("parallel", "arbitrary"),
        name="gdn_prompt",
    )(*[proj_a] * 6, *[conv_w] * 3, proj_a, proj_b, ab, ab_t, *_gate_params(alog, dtb, n_lanes),
      norm_w.reshape(1, HEAD_DIM))


def _gdn_sample_kernel(q_ref, k_ref, v_ref, qc_ref, kc_ref, vc_ref, qw_ref, kw_ref, vw_ref, z_ref, ga_ref,
                       ab_ref, abt_ref, alr_ref, dtr_ref, alc_ref, dtc_ref, nw_ref, s0_ref, o_ref, s1_ref, *,
                       n_heads, seq):
    h0 = pl.program_id(1) * GDN_STEP_HEADS
    nw = nw_ref[...]
    heads = range(GDN_STEP_HEADS)
    gate_vals = _load_gates(ab_ref, abt_ref, alr_ref, dtr_ref, alc_ref, dtc_ref, h0, seq)
    gates = [_head_gates(gate_vals, h0 + hh, n_heads) for hh in heads]
    pre = [r[...] for r in (q_ref, k_ref, v_ref)]
    carried = [r[...] for r in (qc_ref, kc_ref, vc_ref)]
    taps = [r[...] for r in (qw_ref, kw_ref, vw_ref)]
    q, k, v = _gdn_activations(pre, taps, [_shift_in_groups(x, c, seq) for x, c in zip(pre, carried)])
    intra = _gdn_intra(q, k, v, gates, seq)
    per_chunk = CHUNK // seq
    for n in range(q.shape[0] // CHUNK):
        rows = slice(n * CHUNK, (n + 1) * CHUNK)
        seqs = [(hh, b) for hh in heads for b in range(per_chunk)]
        sub = lambda b: slice(b * seq, (b + 1) * seq)
        c = [intra[n, hh] for hh in heads]
        s0 = {(hh, b): s0_ref[n * per_chunk + b, hh] for hh, b in seqs}
        ws_qs = {(hh, b): _dot(jnp.concatenate([c[hh]["w"][sub(b)], c[hh]["qd"][sub(b)]], axis=0), s0[hh, b])
                 for hh, b in seqs}
        v_new = {(hh, b): c[hh]["u"][sub(b)] - ws_qs[hh, b][:seq] for hh, b in seqs}
        for hh, b in seqs:
            gc = c[hh]["gc"][sub(b)]
            g_last = gc[seq - 1:seq]
            k_st = c[hh]["k"][sub(b)] * jnp.exp(g_last - gc)
            s1_ref[n * per_chunk + b, hh] = s0[hh, b] * jnp.exp(g_last) + _dot_tn(k_st, v_new[hh, b])
        for hh in heads:
            cs = slice(hh * HEAD_DIM, (hh + 1) * HEAD_DIM)
            qs = jnp.concatenate([ws_qs[hh, b][seq:] for b in range(per_chunk)], axis=0)
            vn = jnp.concatenate([v_new[hh, b] for b in range(per_chunk)], axis=0)
            out = _gdn_emit(qs + _dot(c[hh]["a"], vn), z_ref[rows, cs], ga_ref[rows, cs], nw)
            o_ref[rows, cs] = out.astype(o_ref.dtype)


def _gdn_sample(proj_a, proj_b, carried, conv_w, ab, ab_t, alog, dtb, norm_w, s0, *, row0, rows, seq, n_heads,
                z_col0, ga_col0, tc=128):
    hs = GDN_STEP_HEADS
    head = lambda i, h: h
    qw = hs // GDN_REP * HEAD_DIM
    vw = hs * HEAD_DIM
    key_dim = n_heads // GDN_REP * HEAD_DIM
    blk0 = row0 // tc
    n_lanes = ab.shape[1]
    state_spec = pl.BlockSpec((tc // seq, hs, HEAD_DIM, HEAD_DIM), lambda i, h: (i, h, 0, 0))
    return pl.pallas_call(
        functools.partial(_gdn_sample_kernel, n_heads=n_heads, seq=seq),
        grid=(rows // tc, n_heads // hs),
        in_specs=[
            *_qkv_specs(tc, lambda i, h: blk0 + i, head, qw, vw, key_dim),
            *_qkv_specs(tc, lambda i, h: i, head, qw, vw, key_dim),
            *_qkv_specs(CONV_W, lambda i, h: 0, head, qw, vw, key_dim),
            pl.BlockSpec((tc, vw), lambda i, h: (blk0 + i, z_col0 // vw + h)),
            pl.BlockSpec((tc, vw), lambda i, h: (blk0 + i, ga_col0 // vw + h)),
            pl.BlockSpec((tc, n_lanes), lambda i, h: (blk0 + i, 0)),
            pl.BlockSpec((n_lanes, tc), lambda i, h: (0, blk0 + i)),
            *_gate_param_specs(n_lanes, lambda i, h: (0, 0)),
            pl.BlockSpec((1, HEAD_DIM), lambda i, h: (0, 0)),
            state_spec,
        ],
        out_specs=[pl.BlockSpec((tc, vw), lambda i, h: (i, h)), state_spec],
        out_shape=[jax.ShapeDtypeStruct((rows, n_heads * HEAD_DIM), F32),
                   jax.ShapeDtypeStruct(s0.shape, F32)],
        compiler_params=_cparams("parallel", "parallel"),
        name="gdn_sample",
    )(*[proj_a] * 3, *[carried] * 3, *[conv_w] * 3, proj_a, proj_b, ab, ab_t, *_gate_params(alog, dtb, n_lanes),
      norm_w.reshape(1, HEAD_DIM), s0)


def _merge(other, gate, o):
    return (other.astype(F32) + jax.nn.sigmoid(gate) * o).astype(BF16)


def _swa_prompt_kernel(sinks_ref, q_ref, kc_ref, kp_ref, vc_ref, vp_ref, ma_ref, gb_ref, o_ref):
    g0 = pl.program_id(0) * SWA_STEP_GROUPS
    i = pl.program_id(1)
    rows = KV_GROUP * WINDOW
    r = lax.broadcasted_iota(jnp.int32, (rows, 2 * WINDOW), 0) % WINDOW
    c = lax.broadcasted_iota(jnp.int32, (rows, 2 * WINDOW), 1)
    visible = ((c < WINDOW) & (c > r) & (i > 0)) | ((c >= WINDOW) & (c - WINDOW <= r))
    head = lax.broadcasted_iota(jnp.int32, (rows, 1), 0) // WINDOW
    scale = HEAD_DIM ** -0.5
    groups = range(SWA_STEP_GROUPS)
    hcols = lambda g, hh: slice((g * KV_GROUP + hh) * HEAD_DIM, (g * KV_GROUP + hh + 1) * HEAD_DIM)
    gcols = lambda g: slice(g * HEAD_DIM, (g + 1) * HEAD_DIM)
    q4 = [jnp.concatenate([q_ref[:, hcols(g, hh)] for hh in range(KV_GROUP)], axis=0) for g in groups]
    kcat = [jnp.concatenate([kp_ref[:, gcols(g)], kc_ref[:, gcols(g)]], axis=0) for g in groups]
    vcat = [jnp.concatenate([vp_ref[:, gcols(g)], vc_ref[:, gcols(g)]], axis=0) for g in groups]
    s = [_dot_nt(q4[g], kcat[g]) for g in groups]
    p, den = [], []
    for g in groups:
        sink = jnp.zeros((rows, 1), F32)
        for hh in range(KV_GROUP):
            sink = jnp.where(head == hh, sinks_ref[(g0 + g) * KV_GROUP + hh], sink)
        sg = jnp.where(visible, s[g] * scale, -jnp.inf)
        m = jnp.maximum(jnp.max(sg, axis=-1, keepdims=True), sink)
        pg = jnp.exp(sg - m)
        p.append(pg)
        den.append(jnp.sum(pg, axis=-1, keepdims=True) + jnp.exp(sink - m))
    o4 = [_dot(p[g], vcat[g]) / den[g] for g in groups]
    for g in groups:
        for hh in range(KV_GROUP):
            cs = hcols(g, hh)
            o_ref[:, cs] = _merge(ma_ref[:, cs], gb_ref[:, cs], o4[g][hh * WINDOW:(hh + 1) * WINDOW])


def _swa_prompt(proj, merged_a, sinks, *, rows, n_kv_heads, k_col0, v_col0, gb_col0):
    sg = SWA_STEP_GROUPS
    qw = sg * KV_GROUP * HEAD_DIM
    kw = sg * HEAD_DIM
    prev = lambda i: jnp.maximum(i - 1, 0)
    return pl.pallas_call(
        _swa_prompt_kernel,
        grid=(n_kv_heads // sg, rows // WINDOW),
        in_specs=[
            pl.BlockSpec(memory_space=pltpu.SMEM),
            pl.BlockSpec((WINDOW, qw), lambda g, i: (i, g)),
            pl.BlockSpec((WINDOW, kw), lambda g, i: (i, k_col0 // kw + g)),
            pl.BlockSpec((WINDOW, kw), lambda g, i: (prev(i), k_col0 // kw + g)),
            pl.BlockSpec((WINDOW, kw), lambda g, i: (i, v_col0 // kw + g)),
            pl.BlockSpec((WINDOW, kw), lambda g, i: (prev(i), v_col0 // kw + g)),
            pl.BlockSpec((WINDOW, qw), lambda g, i: (i, g)),
            pl.BlockSpec((WINDOW, qw), lambda g, i: (i, gb_col0 // qw + g)),
        ],
        out_specs=pl.BlockSpec((WINDOW, qw), lambda g, i: (i, g)),
        out_shape=jax.ShapeDtypeStruct((rows, n_kv_heads * KV_GROUP * HEAD_DIM), BF16),
        compiler_params=_cparams("parallel", "parallel"),
        name="swa_prompt",
    )(sinks, proj, proj, proj, proj, proj, merged_a, proj)


def _swa_sample_kernel(sinks_ref, q_ref, kn_ref, vn_ref, ck_ref, cv_ref, ma_ref, gb_lo_ref, gb_hi_ref,
                       o_ref, ok_ref, ov_ref, *, seq):
    n_b, buf_rows, _ = ck_ref.shape
    kv_cols = kn_ref.shape[1]
    n_groups = kv_cols // HEAD_DIM
    w_buf = buf_rows // n_groups
    half = gb_lo_ref.shape[1]
    rows = KV_GROUP * seq
    t = lax.broadcasted_iota(jnp.int32, (rows, 1), 0) % seq
    head = lax.broadcasted_iota(jnp.int32, (rows, 1), 0) // seq
    jc = lax.broadcasted_iota(jnp.int32, (1, w_buf), 1)
    jn = lax.broadcasted_iota(jnp.int32, (1, seq), 1)
    dist_c = t + w_buf - jc
    cache_ok = (dist_c >= 0) & (dist_c < WINDOW)
    new_ok = jn <= t
    scale = HEAD_DIM ** -0.5
    head_slots = lambda g: pl.ds(g, w_buf, stride=n_groups)
    sinks = []
    for g in range(n_groups):
        sink = jnp.zeros((rows, 1), F32)
        for hh in range(KV_GROUP):
            sink = jnp.where(head == hh, sinks_ref[g * KV_GROUP + hh], sink)
        sinks.append(sink)

    def per_sequence(b, carry):
        r0 = pl.multiple_of(b * seq, seq)
        qb = q_ref[pl.ds(r0, seq), :]
        knb = kn_ref[pl.ds(r0, seq), :]
        vnb = vn_ref[pl.ds(r0, seq), :]
        mab = ma_ref[pl.ds(r0, seq), :]
        gbb = (gb_lo_ref[pl.ds(r0, seq), :], gb_hi_ref[pl.ds(r0, seq), :])
        groups = range(n_groups)
        gcols = lambda g: slice(g * HEAD_DIM, (g + 1) * HEAD_DIM)
        hcols = lambda g, hh: slice((g * KV_GROUP + hh) * HEAD_DIM, (g * KV_GROUP + hh + 1) * HEAD_DIM)
        q4 = [jnp.concatenate([qb[:, hcols(g, hh)] for hh in range(KV_GROUP)], axis=0) for g in groups]
        s_c = [_dot_nt(q4[g], ck_ref[b, head_slots(g), :]) for g in groups]
        s_n = [_dot_nt(q4[g], knb[:, gcols(g)]) for g in groups]
        p_c, p_n, den = [], [], []
        for g in groups:
            sc = jnp.where(cache_ok, s_c[g] * scale, -jnp.inf)
            sn = jnp.where(new_ok, s_n[g] * scale, -jnp.inf)
            m = jnp.maximum(jnp.maximum(jnp.max(sc, axis=-1, keepdims=True),
                                        jnp.max(sn, axis=-1, keepdims=True)), sinks[g])
            pc, pn = jnp.exp(sc - m), jnp.exp(sn - m)
            p_c.append(pc)
            p_n.append(pn)
            den.append(jnp.sum(pc, axis=-1, keepdims=True) + jnp.sum(pn, axis=-1, keepdims=True)
                       + jnp.exp(sinks[g] - m))
        o4 = [(_dot(p_c[g], cv_ref[b, head_slots(g), :]) + _dot(p_n[g], vnb[:, gcols(g)])) / den[g]
              for g in groups]
        for g in groups:
            for hh in range(KV_GROUP):
                cs = hcols(g, hh)
                gate = gbb[cs.start // half][:, cs.start % half:cs.start % half + HEAD_DIM]
                o_ref[pl.ds(r0, seq), cs] = _merge(mab[:, cs], gate, o4[g][hh * seq:(hh + 1) * seq])
        kept = (w_buf - seq) * n_groups
        ok_ref[b, 0:kept, :] = ck_ref[b, buf_rows - kept:buf_rows, :]
        ov_ref[b, 0:kept, :] = cv_ref[b, buf_rows - kept:buf_rows, :]
        for g in groups:
            new_slots = pl.ds(kept + g, seq, stride=n_groups)
            ok_ref[b, new_slots, :] = knb[:, gcols(g)]
            ov_ref[b, new_slots, :] = vnb[:, gcols(g)]
        return carry

    lax.fori_loop(0, n_b, per_sequence, 0)


def _swa_sample(proj, cache_k, cache_v, merged_a, sinks, *, row0, seq, k_col0, v_col0, gb_col0, nb=8):
    n_seq, buf_rows, _ = cache_k.shape
    q_cols = merged_a.shape[1]
    kv_cols = q_cols // KV_GROUP
    half = q_cols // 2
    tr = nb * seq
    blk0 = row0 // tr
    cache_spec = pl.BlockSpec((nb, buf_rows, HEAD_DIM), lambda i: (i, 0, 0))
    return pl.pallas_call(
        functools.partial(_swa_sample_kernel, seq=seq),
        grid=(n_seq // nb,),
        in_specs=[
            pl.BlockSpec(memory_space=pltpu.SMEM),
            pl.BlockSpec((tr, q_cols), lambda i: (blk0 + i, 0)),
            pl.BlockSpec((tr, kv_cols), lambda i: (blk0 + i, k_col0 // kv_cols)),
            pl.BlockSpec((tr, kv_cols), lambda i: (blk0 + i, v_col0 // kv_cols)),
            cache_spec, cache_spec,
            pl.BlockSpec((tr, q_cols), lambda i: (i, 0)),
            pl.BlockSpec((tr, half), lambda i: (blk0 + i, gb_col0 // half)),
            pl.BlockSpec((tr, half), lambda i: (blk0 + i, gb_col0 // half + 1)),
        ],
        out_specs=[pl.BlockSpec((tr, q_cols), lambda i: (i, 0)), cache_spec, cache_spec],
        out_shape=[jax.ShapeDtypeStruct((n_seq * seq, q_cols), BF16),
                   jax.ShapeDtypeStruct(cache_k.shape, F32),
                   jax.ShapeDtypeStruct(cache_v.shape, F32)],
        compiler_params=_cparams("parallel"),
        name="swa_sample",
    )(sinks, proj, proj, proj, cache_k, cache_v, merged_a, proj, proj)


def _outproj_kernel(x_ref, mp_ref, ms_ref, w_ref, o_ref, *, n_prompt_tiles):
    i = pl.program_id(0)

    @pl.when(i < n_prompt_tiles)
    def _():
        o_ref[...] = x_ref[...] + jnp.dot(mp_ref[...], w_ref[...], preferred_element_type=F32)

    @pl.when(i >= n_prompt_tiles)
    def _():
        o_ref[...] = x_ref[...] + jnp.dot(ms_ref[...], w_ref[...], preferred_element_type=F32)


def _outproj(x, merged_p, merged_s, w_out, *, tm=512, tn=1024):
    m, d = x.shape
    npt = merged_p.shape[0] // tm
    return pl.pallas_call(
        functools.partial(_outproj_kernel, n_prompt_tiles=npt),
        grid=(m // tm, d // tn),
        in_specs=[
            pl.BlockSpec((tm, tn), lambda i, j: (i, j)),
            pl.BlockSpec((tm, d), lambda i, j: (jnp.minimum(i, npt - 1), 0)),
            pl.BlockSpec((tm, d), lambda i, j: (jnp.maximum(i - npt, 0), 0)),
            pl.BlockSpec((d, tn), lambda i, j: (0, j)),
        ],
        out_specs=pl.BlockSpec((tm, tn), lambda i, j: (i, j)),
        out_shape=jax.ShapeDtypeStruct((m, d), F32),
        compiler_params=_cparams("parallel", "arbitrary"),
        name="outproj",
    )(x, merged_p, merged_s, w_out)


def _ple_kernel(xf_ref, x_ref, pe_ref, nw_ref, wp_ref, wg_ref, o_head_ref, o_tail_ref, xn_ref, *, n_head_tiles):
    i = pl.program_id(0)

    @pl.when(pl.program_id(1) == 0)
    def _():
        xn_ref[...] = _rms(xf_ref[...], nw_ref[...]).astype(BF16)

    def emit(o_ref):
        xn = xn_ref[...]
        pe = pe_ref[...]
        for c in range(0, o_ref.shape[1], PLE_OUT_CHUNK):
            cs = slice(c, c + PLE_OUT_CHUNK)
            gate = jnp.dot(xn, wg_ref[:, cs], preferred_element_type=F32)
            emb = jnp.dot(pe, wp_ref[:, cs], preferred_element_type=F32)
            o_ref[:, cs] = x_ref[:, cs] + emb * jax.nn.sigmoid(gate)

    @pl.when(i < n_head_tiles)
    def _():
        emit(o_head_ref)

    @pl.when(i >= n_head_tiles)
    def _():
        emit(o_tail_ref)


def _ple(x, pe, norm_w, w_proj, w_gate, *, head_rows, tm=512, tn=1024):
    m, d = x.shape
    pdim = pe.shape[1]
    nh = head_rows // tm
    nj = d // tn
    head_idx = lambda i, j: (jnp.minimum(i, nh - 1), jnp.where(i < nh, j, nj - 1))
    tail_idx = lambda i, j: (jnp.maximum(i - nh, 0), jnp.where(i >= nh, j, 0))
    return pl.pallas_call(
        functools.partial(_ple_kernel, n_head_tiles=nh),
        grid=(m // tm, nj),
        in_specs=[
            pl.BlockSpec((tm, d), lambda i, j: (i, 0), pipeline_mode=pl.Buffered(1)),
            pl.BlockSpec((tm, tn), lambda i, j: (i, j)),
            pl.BlockSpec((tm, pdim), lambda i, j: (i, 0)),
            pl.BlockSpec((1, d), lambda i, j: (0, 0)),
            pl.BlockSpec((pdim, tn), lambda i, j: (0, j)),
            pl.BlockSpec((d, tn), lambda i, j: (0, j)),
        ],
        out_specs=[pl.BlockSpec((tm, tn), head_idx), pl.BlockSpec((tm, tn), tail_idx)],
        out_shape=[jax.ShapeDtypeStruct((head_rows, d), F32), jax.ShapeDtypeStruct((m - head_rows, d), F32)],
        scratch_shapes=[pltpu.VMEM((tm, d), BF16)],
        compiler_params=_cparams("arbitrary", "arbitrary"),
        name="ple",
    )(x, x, pe, norm_w.reshape(1, d), w_proj, w_gate)


def _rope_tables(seq, dec_batch, dec_seq):
    half = HEAD_DIM // 2
    inv_freq = ROPE_THETA ** (-jnp.arange(half, dtype=F32) / half)
    pos_p = jnp.arange(seq, dtype=jnp.int32)
    pos_s = jnp.tile(PAST_LEN + jnp.arange(dec_seq, dtype=jnp.int32), dec_batch)
    ang = jnp.concatenate([pos_p, pos_s]).astype(F32)[:, None] * inv_freq[None, :]
    cos, sin = jnp.cos(ang), jnp.sin(ang)
    return jnp.concatenate([cos, cos], axis=1), jnp.concatenate([-sin, sin], axis=1)


def kernel(x_prompt, x_sample, p_prompt, p_sample, state_gdn, state_conv, cache_swa_k, cache_swa_v, ffn1_norm, ffn1_w_gate, ffn1_w_up, ffn1_w_down, mix_norm, w_in, conv_w, A_log, dt_bias, gdn_norm_w, q_norm_w, k_norm_w, sinks, w_out, ffn2_norm, ffn2_w_gate, ffn2_w_up, ffn2_w_down, ple_norm, w_ple_proj, w_ple_gate):
    depth, dec_batch, n_v_heads, dk, dv = state_gdn.shape
    assert depth == 1 and dk == HEAD_DIM and dv == HEAD_DIM
    bp, seq, d = x_prompt.shape
    assert bp == 1
    dec_seq = x_sample.shape[1]
    w_buf, n_kv_heads = cache_swa_k.shape[2], cache_swa_k.shape[3]
    n_heads = sinks.shape[1]
    assert n_heads == n_kv_heads * KV_GROUP and w_buf == WINDOW and CHUNK % dec_seq == 0
    conv_dim = conv_w.shape[2]
    key_dim = n_v_heads // GDN_REP * HEAD_DIM
    val_dim = n_v_heads * HEAD_DIM
    q_dim = n_heads * HEAD_DIM
    kv_cols = n_kv_heads * HEAD_DIM
    n_samp = dec_batch * dec_seq

    ab0 = conv_dim + val_dim
    b0 = ab0 + 2 * n_v_heads
    assert w_in.shape[2] == b0 + q_dim + 2 * kv_cols + 2 * d
    w_all = w_in[0].astype(BF16)
    w_b = w_all[:, b0:]
    n_ab = 128
    w_ab = jnp.pad(w_all[:, ab0:b0], ((0, 0), (0, n_ab - 2 * n_v_heads)))
    z_col0 = conv_dim
    k_col0, v_col0 = q_dim, q_dim + kv_cols
    ga_col0 = q_dim + 2 * kv_cols
    gb_col0 = ga_col0 + d

    bf = lambda w: w[0].astype(BF16)
    pe = jnp.concatenate([p_prompt.reshape(seq, PLE_DIM), p_sample.reshape(n_samp, PLE_DIM)], axis=0)

    x = _ffn([x_prompt.reshape(seq, d), x_sample.reshape(n_samp, d)], ffn1_norm[0],
             bf(ffn1_w_gate), bf(ffn1_w_up), bf(ffn1_w_down))
    proj_a, ab = _inproj_ab(x, mix_norm[0], w_all, w_ab, n=ab0)
    cos, sin = _rope_tables(seq, dec_batch, dec_seq)
    proj_b = _inproj_rope(x, mix_norm[0], w_b, cos, sin, q_norm_w[0], k_norm_w[0], q_cols=q_dim, k_cols=kv_cols)

    carried = jnp.pad(state_conv[0], ((0, 0), (dec_seq - (CONV_W - 1), 0), (0, 0))).reshape(n_samp, conv_dim)
    ab_t = ab.T
    gdn_args = dict(n_heads=n_v_heads, z_col0=z_col0, ga_col0=ga_col0)
    ma_p, sg_p = _gdn_prompt(proj_a, proj_b, conv_w[0], ab, ab_t, A_log[0], dt_bias[0], gdn_norm_w[0],
                             rows=seq, **gdn_args)
    ma_s, sg_s = _gdn_sample(proj_a, proj_b, carried, conv_w[0], ab, ab_t, A_log[0], dt_bias[0], gdn_norm_w[0],
                             state_gdn[0], row0=seq, rows=n_samp, seq=dec_seq, **gdn_args)

    swa_args = dict(k_col0=k_col0, v_col0=v_col0, gb_col0=gb_col0)
    merged_p = _swa_prompt(proj_b, ma_p, sinks[0], rows=seq, n_kv_heads=n_kv_heads, **swa_args)
    merged_s, kk_s, vv_s = _swa_sample(proj_b, cache_swa_k[0].reshape(dec_batch, w_buf * n_kv_heads, HEAD_DIM),
                                       cache_swa_v[0].reshape(dec_batch, w_buf * n_kv_heads, HEAD_DIM),
                                       ma_s, sinks[0], row0=seq, seq=dec_seq, **swa_args)

    x = _outproj(x, merged_p, merged_s, bf(w_out))
    x = _ffn([x], ffn2_norm[0], bf(ffn2_w_gate), bf(ffn2_w_up), bf(ffn2_w_down))
    y_p, y_s = _ple(x, pe.astype(BF16), ple_norm[0], bf(w_ple_proj), bf(w_ple_gate), head_rows=seq)

    cache_shape = (1, 1, w_buf, n_kv_heads, HEAD_DIM)
    return (
        y_p.reshape(1, seq, d),
        y_s.reshape(dec_batch, dec_seq, d),
        sg_p.reshape(1, 1, n_v_heads, HEAD_DIM, HEAD_DIM),
        proj_a[seq - (CONV_W - 1):seq, :conv_dim].reshape(1, 1, CONV_W - 1, conv_dim),
        proj_b[seq - w_buf:seq, k_col0:k_col0 + kv_cols].reshape(cache_shape),
        proj_b[seq - w_buf:seq, v_col0:v_col0 + kv_cols].reshape(cache_shape),
        sg_s.reshape(1, dec_batch, n_v_heads, HEAD_DIM, HEAD_DIM),
        proj_a[seq:, :conv_dim].reshape(dec_batch, dec_seq, conv_dim)[:, dec_seq - (CONV_W - 1):].reshape(
            1, dec_batch, CONV_W - 1, conv_dim),
        kk_s.reshape(1, dec_batch, w_buf, n_kv_heads, HEAD_DIM),
        vv_s.reshape(1, dec_batch, w_buf, n_kv_heads, HEAD_DIM),
    )
```

```python
import functools

import jax
import jax.numpy as jnp
from jax import lax
from jax.experimental import pallas as pl
from jax.experimental.pallas import tpu as pltpu

F32 = jnp.float32
BF16 = jnp.bfloat16

EPS = 1e-6
HEAD_DIM = 128
KV_GROUP = 4
WINDOW = 128
ROPE_THETA = 10000.0
PAST_LEN = 8192
GDN_REP = 2
CONV_W = 4
CHUNK = 64
PLE_DIM = 256
GDN_STEP_HEADS = 8
PLE_OUT_CHUNK = 256
FFN_OUT_CHUNK = 1024
SWA_STEP_GROUPS = 4

V7X_VMEM_LIMIT_BYTES = 62 * 1024 * 1024


def _cparams(*sem):
    return pltpu.CompilerParams(dimension_semantics=sem, vmem_limit_bytes=V7X_VMEM_LIMIT_BYTES)


def _rms(x, w):
    return x * lax.rsqrt(jnp.mean(x * x, axis=-1, keepdims=True) + EPS) * w


def _dot(a, b):
    return jnp.dot(a.astype(BF16), b.astype(BF16), preferred_element_type=F32)


def _dot_nt(a, b):
    return lax.dot_general(a.astype(BF16), b.astype(BF16), (((1,), (1,)), ((), ())),
                           preferred_element_type=F32)


def _dot_tn(a, b):
    return lax.dot_general(a.astype(BF16), b.astype(BF16), (((0,), (0,)), ((), ())),
                           preferred_element_type=F32)


def _row_part_specs(parts, tm, width):
    specs, spans, start = [], [], 0
    for part in parts:
        n = part.shape[0] // tm
        specs.append(pl.BlockSpec(
            (tm, width), functools.partial(lambda i, j, s, n: (jnp.clip(i - s, 0, n - 1), 0), s=start, n=n),
            pipeline_mode=pl.Buffered(1)))
        spans.append((start, n))
        start += n
    return specs, spans


def _ffn_kernel(*refs, spans):
    n_parts = len(spans)
    x_refs = refs[:n_parts]
    nw_ref, wg_ref, wu_ref, wd_ref, o_ref, xn_ref = refs[n_parts:]
    i = pl.program_id(0)
    j = pl.program_id(1)
    for x_ref, (start, n) in zip(x_refs, spans):
        @pl.when((j == 0) & (i >= start) & (i < start + n))
        def _():
            x = x_ref[...]
            xn_ref[...] = _rms(x, nw_ref[...]).astype(BF16)
            o_ref[...] = x

    xn = xn_ref[...]
    g = jnp.dot(xn, wg_ref[...], preferred_element_type=F32)
    u = jnp.dot(xn, wu_ref[...], preferred_element_type=F32)
    h = ((0.5 * (g * jax.nn.sigmoid(g))) * u).astype(BF16)
    d = o_ref.shape[1]
    for c in range(0, d, FFN_OUT_CHUNK):
        cs = slice(c, min(c + FFN_OUT_CHUNK, d))
        o_ref[:, cs] += jnp.dot(h, wd_ref[:, cs], preferred_element_type=F32)


def _ffn(x_parts, norm_w, wg, wu, wd, *, tm=512, tf=256):
    d = x_parts[0].shape[1]
    m = sum(p.shape[0] for p in x_parts)
    f = wg.shape[1]
    x_specs, spans = _row_part_specs(x_parts, tm, d)
    return pl.pallas_call(
        functools.partial(_ffn_kernel, spans=spans),
        grid=(m // tm, f // tf),
        in_specs=[
            *x_specs,
            pl.BlockSpec((1, d), lambda i, j: (0, 0)),
            pl.BlockSpec((d, tf), lambda i, j: (0, j)),
            pl.BlockSpec((d, tf), lambda i, j: (0, j)),
            pl.BlockSpec((tf, d), lambda i, j: (j, 0)),
        ],
        out_specs=pl.BlockSpec((tm, d), lambda i, j: (i, 0)),
        out_shape=jax.ShapeDtypeStruct((m, d), F32),
        scratch_shapes=[pltpu.VMEM((tm, d), BF16)],
        compiler_params=_cparams("parallel", "arbitrary"),
        name="ffn",
    )(*x_parts, norm_w.reshape(1, d), wg, wu, wd)


def _inproj_ab_kernel(x_ref, nw_ref, w_ref, wab_ref, o_ref, ab_ref, xn_ref):
    @pl.when(pl.program_id(1) == 0)
    def _():
        xn = _rms(x_ref[...], nw_ref[...]).astype(BF16)
        xn_ref[...] = xn
        ab_ref[...] = jnp.dot(xn, wab_ref[...], preferred_element_type=F32)

    o_ref[...] = jnp.dot(xn_ref[...], w_ref[...], preferred_element_type=F32)


def _inproj_rope_kernel(x_ref, nw_ref, w_ref, cos_ref, sin_ref, qw_ref, kw_ref, o_ref, xn_ref, *,
                        n_q_blocks, n_rope_blocks):
    j = pl.program_id(1)

    @pl.when(j == 0)
    def _():
        xn_ref[...] = _rms(x_ref[...], nw_ref[...]).astype(BF16)

    @pl.when(j < n_rope_blocks)
    def _():
        w = jnp.where(j < n_q_blocks, qw_ref[...], kw_ref[...])
        cos = cos_ref[...]
        sin = sin_ref[...]
        xn = xn_ref[...]
        step = 2 * HEAD_DIM
        for c in range(0, o_ref.shape[1], step):
            acc = jnp.dot(xn, w_ref[:, c:c + step], preferred_element_type=F32)
            for h in range(step // HEAD_DIM):
                y = _rms(acc[:, h * HEAD_DIM:(h + 1) * HEAD_DIM], w)
                cs = slice(c + h * HEAD_DIM, c + (h + 1) * HEAD_DIM)
                o_ref[:, cs] = y * cos + pltpu.roll(y, HEAD_DIM // 2, 1) * sin

    @pl.when(j >= n_rope_blocks)
    def _():
        o_ref[...] = jnp.dot(xn_ref[...], w_ref[...], preferred_element_type=F32)


def _inproj_rope(x, norm_w, w, cos, sin, q_norm_w, k_norm_w, *, q_cols, k_cols, tm=512, tn=1024):
    m, d = x.shape
    n = w.shape[1]
    row = lambda i, j: (i, 0)
    fixed = lambda i, j: (0, 0)
    return pl.pallas_call(
        functools.partial(_inproj_rope_kernel, n_q_blocks=q_cols // tn, n_rope_blocks=(q_cols + k_cols) // tn),
        grid=(m // tm, n // tn),
        in_specs=[
            pl.BlockSpec((tm, d), row, pipeline_mode=pl.Buffered(1)),
            pl.BlockSpec((1, d), fixed),
            pl.BlockSpec((d, tn), lambda i, j: (0, j)),
            pl.BlockSpec((tm, HEAD_DIM), row),
            pl.BlockSpec((tm, HEAD_DIM), row),
            pl.BlockSpec((1, HEAD_DIM), fixed),
            pl.BlockSpec((1, HEAD_DIM), fixed),
        ],
        out_specs=pl.BlockSpec((tm, tn), lambda i, j: (i, j)),
        out_shape=jax.ShapeDtypeStruct((m, n), F32),
        scratch_shapes=[pltpu.VMEM((tm, d), BF16)],
        compiler_params=_cparams("parallel", "arbitrary"),
        name="inproj_rope",
    )(x, norm_w.reshape(1, d), w, cos, sin, q_norm_w.reshape(1, HEAD_DIM), k_norm_w.reshape(1, HEAD_DIM))


def _inproj_ab(x, norm_w, w, w_ab, *, n, tm=512, tn=1024):
    m, d = x.shape
    nab = w_ab.shape[1]
    return pl.pallas_call(
        _inproj_ab_kernel,
        grid=(m // tm, n // tn),
        in_specs=[
            pl.BlockSpec((tm, d), lambda i, j: (i, 0), pipeline_mode=pl.Buffered(1)),
            pl.BlockSpec((1, d), lambda i, j: (0, 0)),
            pl.BlockSpec((d, tn), lambda i, j: (0, j)),
            pl.BlockSpec((d, nab), lambda i, j: (0, 0)),
        ],
        out_specs=[pl.BlockSpec((tm, tn), lambda i, j: (i, j)), pl.BlockSpec((tm, nab), lambda i, j: (i, 0))],
        out_shape=[jax.ShapeDtypeStruct((m, n), F32), jax.ShapeDtypeStruct((m, nab), F32)],
        scratch_shapes=[pltpu.VMEM((tm, d), BF16)],
        compiler_params=_cparams("parallel", "arbitrary"),
        name="inproj_ab",
    )(x, norm_w.reshape(1, d), w, w_ab)


def _conv_silu(x, w, shifted):
    y = None
    for tap in range(CONV_W):
        sh = CONV_W - 1 - tap
        term = (x if sh == 0 else shifted(sh)) * w[tap:tap + 1]
        y = term if y is None else y + term
    return y * jax.nn.sigmoid(y)


def _shift_in_sequence(x, halo):
    row8 = lax.broadcasted_iota(jnp.int32, (8, 1), 0)

    def shifted(sh):
        rolled = pltpu.roll(x, sh, 0)
        head = jnp.where(row8 < sh, pltpu.roll(halo, sh, 0), rolled[:8])
        return jnp.concatenate([head, rolled[8:]], axis=0)

    return shifted


def _shift_in_groups(x, carried, seq):
    rows = x.shape[0]
    t = lax.broadcasted_iota(jnp.int32, (rows, 1), 0) % seq
    return lambda sh: jnp.where(t >= sh, pltpu.roll(x, sh, 0), pltpu.roll(carried, rows + sh - seq, 0))


def _l2norm_heads(y, scale):
    heads = []
    for h in range(y.shape[1] // HEAD_DIM):
        yh = y[:, h * HEAD_DIM:(h + 1) * HEAD_DIM]
        heads.append(yh * (lax.rsqrt(jnp.sum(yh * yh, axis=-1, keepdims=True) + EPS) * scale))
    return jnp.concatenate(heads, axis=1)


def _gdn_activations(pre, conv_w, shifts):
    q, k, v = (_conv_silu(x, w, s) for x, w, s in zip(pre, conv_w, shifts))
    return _l2norm_heads(q, HEAD_DIM ** -0.5), _l2norm_heads(k, 1.0), v


def _softplus(x):
    return jnp.maximum(x, 0.0) + jnp.log1p(jnp.exp(-jnp.abs(x)))


def _seg_cumsum(x, pos, seg, axis):
    d = 1
    while d < seg:
        x = x + jnp.where(pos >= d, pltpu.roll(x, d, axis), 0.0)
        d *= 2
    return x


def _gates(ab, abt8, alog_row, dtb_row, alog_col8, dtb_col8, seg):
    rows = ab.shape[0]
    pos_c = lax.broadcasted_iota(jnp.int32, (rows, 1), 0) % seg
    pos_r = lax.broadcasted_iota(jnp.int32, (1, rows), 1) % seg
    g_all = -jnp.exp(alog_row) * _softplus(ab + dtb_row)
    g_t = -jnp.exp(alog_col8) * _softplus(abt8 + dtb_col8)
    return _seg_cumsum(g_all, pos_c, seg, 0), jax.nn.sigmoid(ab), _seg_cumsum(g_t, pos_r, seg, 1)


def _pick_lane(x, idx):
    lane = lax.broadcasted_iota(jnp.int32, (1, x.shape[1]), 1)
    return jnp.sum(jnp.where(lane == idx, x, 0.0), axis=1, keepdims=True)


def _pick_sublane(x, idx):
    sub = lax.broadcasted_iota(jnp.int32, (x.shape[0], 1), 0)
    return jnp.sum(jnp.where(sub == idx, x, 0.0), axis=0, keepdims=True)


def _head_gates(gate_vals, h, n_heads):
    gc_all, sig_all, gc_t = gate_vals
    return _pick_lane(sig_all, n_heads + h), _pick_lane(gc_all, h), _pick_sublane(gc_t, h % 8)


def _load_gates(ab_ref, abt_ref, alr_ref, dtr_ref, alc_ref, dtc_ref, h0, seg):
    h8 = pl.multiple_of((h0 // 8) * 8, 8)
    return _gates(ab_ref[...], abt_ref[pl.ds(h8, 8), :], alr_ref[...], dtr_ref[...],
                  alc_ref[pl.ds(h8, 8), :], dtc_ref[pl.ds(h8, 8), :], seg)


def _chunk_masks(seg):
    r = lax.broadcasted_iota(jnp.int32, (CHUNK, CHUNK), 0)
    c = lax.broadcasted_iota(jnp.int32, (CHUNK, CHUNK), 1)
    same = (r // seg) == (c // seg)
    return same & (c <= r), same & (c < r), r == c


def _gdn_intra(q_all, k_all, v_all, gates, seg):
    incl, strict, eye = _chunk_masks(seg)
    n_chunks = q_all.shape[0] // CHUNK
    n_heads = len(gates)
    chunks = range(n_chunks)
    heads = range(n_heads)
    items = [(n, hh) for n in chunks for hh in heads]
    rows = lambda n: slice(n * CHUNK, (n + 1) * CHUNK)
    cols = lambda j: slice(j * HEAD_DIM, (j + 1) * HEAD_DIM)
    pairs = [(n, j) for n in chunks for j in range(n_heads // GDN_REP)]
    q = {(n, j): q_all[rows(n), cols(j)] for n, j in pairs}
    k = {(n, j): k_all[rows(n), cols(j)] for n, j in pairs}
    kk_qk = {p: _dot_nt(jnp.concatenate([k[p], q[p]], axis=0), k[p]) for p in pairs}
    kk = {p: kk_qk[p][:CHUNK] for p in pairs}
    qk = {p: kk_qk[p][CHUNK:] for p in pairs}
    out = {}
    low, rhs = {}, {}
    for n, hh in items:
        beta, gc_col, gc_row = gates[hh]
        beta, gc_col, gc_row = beta[rows(n)], gc_col[rows(n)], gc_row[:, rows(n)]
        pj = (n, hh // GDN_REP)
        decay = jnp.exp(jnp.where(incl, gc_col - gc_row, -jnp.inf))
        low[n, hh] = jnp.where(strict, beta * kk[pj] * decay, 0.0)
        e_gc = jnp.exp(gc_col)
        rhs[n, hh] = jnp.concatenate([v_all[rows(n), cols(hh)] * beta, k[pj] * (beta * e_gc)], axis=1)
        out[n, hh] = dict(a=jnp.where(incl, qk[pj] * decay, 0.0), qd=q[pj] * e_gc, gc=gc_col, k=k[pj])
    p = {it: -low[it] for it in items}
    t = {it: jnp.where(eye, 1.0, p[it]) for it in items}
    if seg > 2:
        p = {it: _dot(p[it], p[it]) for it in items}
        n_pow = 4
        while n_pow < seg:
            both = {it: _dot(jnp.concatenate([t[it], p[it]], axis=0), p[it]) for it in items}
            t = {it: t[it] + both[it][:CHUNK] for it in items}
            p = {it: both[it][CHUNK:] for it in items}
            n_pow *= 2
        t = {it: t[it] + _dot(t[it], p[it]) for it in items}
    for it in items:
        uw = _dot(t[it], rhs[it])
        out[it]["u"] = uw[:, :HEAD_DIM]
        out[it]["w"] = uw[:, HEAD_DIM:]
    return out


def _gdn_emit(o, z, gate, nw):
    return (_rms(o, nw) * (z * jax.nn.sigmoid(z)) * jax.nn.sigmoid(gate)).astype(BF16)


def _gdn_prompt_kernel(q_ref, k_ref, v_ref, qh_ref, kh_ref, vh_ref, qw_ref, kw_ref, vw_ref, z_ref, ga_ref,
                       ab_ref, abt_ref, alr_ref, dtr_ref, alc_ref, dtc_ref, nw_ref, o_ref, sfin_ref, s_ref, *,
                       n_heads):
    h0 = pl.program_id(0) * GDN_STEP_HEADS
    i = pl.program_id(1)

    @pl.when(i == 0)
    def _():
        s_ref[...] = jnp.zeros_like(s_ref)

    nw = nw_ref[...]
    heads = range(GDN_STEP_HEADS)
    gate_vals = _load_gates(ab_ref, abt_ref, alr_ref, dtr_ref, alc_ref, dtc_ref, h0, CHUNK)
    gates = [_head_gates(gate_vals, h0 + hh, n_heads) for hh in heads]
    pre = [r[...] for r in (q_ref, k_ref, v_ref)]
    halos = [jnp.where(i > 0, r[...], 0.0) for r in (qh_ref, kh_ref, vh_ref)]
    taps = [r[...] for r in (qw_ref, kw_ref, vw_ref)]
    q, k, v = _gdn_activations(pre, taps, [_shift_in_sequence(x, h) for x, h in zip(pre, halos)])
    intra = _gdn_intra(q, k, v, gates, CHUNK)
    state = [s_ref[hh] for hh in heads]
    for n in range(q.shape[0] // CHUNK):
        rows = slice(n * CHUNK, (n + 1) * CHUNK)
        c = [intra[n, hh] for hh in heads]
        ws_qs = [_dot(jnp.concatenate([c[hh]["w"], c[hh]["qd"]], axis=0), state[hh]) for hh in heads]
        v_new = [c[hh]["u"] - ws_qs[hh][:CHUNK] for hh in heads]
        g_last = [c[hh]["gc"][CHUNK - 1:CHUNK] for hh in heads]
        k_st = [(c[hh]["k"] * jnp.exp(g_last[hh] - c[hh]["gc"])).T for hh in heads]
        av_kv = [_dot(jnp.concatenate([c[hh]["a"], k_st[hh]], axis=0), v_new[hh]) for hh in heads]
        o = [ws_qs[hh][CHUNK:] + av_kv[hh][:CHUNK] for hh in heads]
        state = [state[hh] * jnp.exp(g_last[hh]) + av_kv[hh][CHUNK:] for hh in heads]
        for hh in heads:
            cs = slice(hh * HEAD_DIM, (hh + 1) * HEAD_DIM)
            o_ref[rows, cs] = _gdn_emit(o[hh], z_ref[rows, cs], ga_ref[rows, cs], nw)
    for hh in heads:
        s_ref[hh] = state[hh]

    @pl.when(i == pl.num_programs(1) - 1)
    def _():
        sfin_ref[...] = s_ref[...]


def _gate_param_specs(n_lanes, idx):
    row = pl.BlockSpec((1, n_lanes), idx)
    col = pl.BlockSpec((n_lanes, 1), idx)
    return [row, row, col, col]


def _gate_params(alog, dtb, n_lanes):
    pad = lambda v: jnp.pad(v.astype(F32), (0, n_lanes - v.shape[0]))
    al, dt = pad(alog), pad(dtb)
    return al.reshape(1, n_lanes), dt.reshape(1, n_lanes), al.reshape(n_lanes, 1), dt.reshape(n_lanes, 1)


def _qkv_specs(block_rows, row_idx, head_idx, qw, vw, key_dim):
    return [
        pl.BlockSpec((block_rows, qw), lambda a, b: (row_idx(a, b), head_idx(a, b))),
        pl.BlockSpec((block_rows, qw), lambda a, b: (row_idx(a, b), key_dim // qw + head_idx(a, b))),
        pl.BlockSpec((block_rows, vw), lambda a, b: (row_idx(a, b), 2 * key_dim // vw + head_idx(a, b))),
    ]


def _gdn_prompt(proj_a, proj_b, conv_w, ab, ab_t, alog, dtb, norm_w, *, rows, n_heads, z_col0, ga_col0, tc=512):
    hs = GDN_STEP_HEADS
    qw = hs // GDN_REP * HEAD_DIM
    vw = hs * HEAD_DIM
    key_dim = n_heads // GDN_REP * HEAD_DIM
    n_lanes = ab.shape[1]
    tc8 = tc // 8
    head = lambda h, i: h
    return pl.pallas_call(
        functools.partial(_gdn_prompt_kernel, n_heads=n_heads),
        grid=(n_heads // hs, rows // tc),
        in_specs=[
            *_qkv_specs(tc, lambda h, i: i, head, qw, vw, key_dim),
            *_qkv_specs(8, lambda h, i: jnp.maximum(i * tc8 - 1, 0), head, qw, vw, key_dim),
            *_qkv_specs(CONV_W, lambda h, i: 0, head, qw, vw, key_dim),
            pl.BlockSpec((tc, vw), lambda h, i: (i, z_col0 // vw + h)),
            pl.BlockSpec((tc, vw), lambda h, i: (i, ga_col0 // vw + h)),
            pl.BlockSpec((tc, n_lanes), lambda h, i: (i, 0)),
            pl.BlockSpec((n_lanes, tc), lambda h, i: (0, i)),
            *_gate_param_specs(n_lanes, lambda h, i: (0, 0)),
            pl.BlockSpec((1, HEAD_DIM), lambda h, i: (0, 0)),
        ],
        out_specs=[
            pl.BlockSpec((tc, vw), lambda h, i: (i, h)),
            pl.BlockSpec((hs, HEAD_DIM, HEAD_DIM), lambda h, i: (h, 0, 0)),
        ],
        out_shape=[jax.ShapeDtypeStruct((rows, n_heads * HEAD_DIM), BF16),
                   jax.ShapeDtypeStruct((n_heads, HEAD_DIM, HEAD_DIM), F32)],
        scratch_shapes=[pltpu.VMEM((hs, HEAD_DIM, HEAD_DIM), F32)],
        compiler_params=_cparams("parallel", "arbitrary"),
        name="gdn_prompt",
    )(*[proj_a] * 6, *[conv_w] * 3, proj_a, proj_b, ab, ab_t, *_gate_params(alog, dtb, n_lanes),
      norm_w.reshape(1, HEAD_DIM))


def _gdn_sample_kernel(q_ref, k_ref, v_ref, qc_ref, kc_ref, vc_ref, qw_ref, kw_ref, vw_ref, z_ref, ga_ref,
                       ab_ref, abt_ref, alr_ref, dtr_ref, alc_ref, dtc_ref, nw_ref, s0_ref, o_ref, s1_ref, *,
                       n_heads, seq):
    h0 = pl.program_id(1) * GDN_STEP_HEADS
    nw = nw_ref[...]
    heads = range(GDN_STEP_HEADS)
    gate_vals = _load_gates(ab_ref, abt_ref, alr_ref, dtr_ref, alc_ref, dtc_ref, h0, seq)
    gates = [_head_gates(gate_vals, h0 + hh, n_heads) for hh in heads]
    pre = [r[...] for r in (q_ref, k_ref, v_ref)]
    carried = [r[...] for r in (qc_ref, kc_ref, vc_ref)]
    taps = [r[...] for r in (qw_ref, kw_ref, vw_ref)]
    q, k, v = _gdn_activations(pre, taps, [_shift_in_groups(x, c, seq) for x, c in zip(pre, carried)])
    intra = _gdn_intra(q, k, v, gates, seq)
    per_chunk = CHUNK // seq
    for n in range(q.shape[0] // CHUNK):
        rows = slice(n * CHUNK, (n + 1) * CHUNK)
        seqs = [(hh, b) for hh in heads for b in range(per_chunk)]
        sub = lambda b: slice(b * seq, (b + 1) * seq)
        c = [intra[n, hh] for hh in heads]
        s0 = {(hh, b): s0_ref[n * per_chunk + b, hh] for hh, b in seqs}
        ws_qs = {(hh, b): _dot(jnp.concatenate([c[hh]["w"][sub(b)], c[hh]["qd"][sub(b)]], axis=0), s0[hh, b])
                 for hh, b in seqs}
        v_new = {(hh, b): c[hh]["u"][sub(b)] - ws_qs[hh, b][:seq] for hh, b in seqs}
        for hh, b in seqs:
            gc = c[hh]["gc"][sub(b)]
            g_last = gc[seq - 1:seq]
            k_st = c[hh]["k"][sub(b)] * jnp.exp(g_last - gc)
            s1_ref[n * per_chunk + b, hh] = s0[hh, b] * jnp.exp(g_last) + _dot_tn(k_st, v_new[hh, b])
        for hh in heads:
            cs = slice(hh * HEAD_DIM, (hh + 1) * HEAD_DIM)
            qs = jnp.concatenate([ws_qs[hh, b][seq:] for b in range(per_chunk)], axis=0)
            vn = jnp.concatenate([v_new[hh, b] for b in range(per_chunk)], axis=0)
            out = _gdn_emit(qs + _dot(c[hh]["a"], vn), z_ref[rows, cs], ga_ref[rows, cs], nw)
            o_ref[rows, cs] = out.astype(o_ref.dtype)


def _gdn_sample(proj_a, proj_b, carried, conv_w, ab, ab_t, alog, dtb, norm_w, s0, *, row0, rows, seq, n_heads,
                z_col0, ga_col0, tc=128):
    hs = GDN_STEP_HEADS
    head = lambda i, h: h
    qw = hs // GDN_REP * HEAD_DIM
    vw = hs * HEAD_DIM
    key_dim = n_heads // GDN_REP * HEAD_DIM
    blk0 = row0 // tc
    n_lanes = ab.shape[1]
    state_spec = pl.BlockSpec((tc // seq, hs, HEAD_DIM, HEAD_DIM), lambda i, h: (i, h, 0, 0))
    return pl.pallas_call(
        functools.partial(_gdn_sample_kernel, n_heads=n_heads, seq=seq),
        grid=(rows // tc, n_heads // hs),
        in_specs=[
            *_qkv_specs(tc, lambda i, h: blk0 + i, head, qw, vw, key_dim),
            *_qkv_specs(tc, lambda i, h: i, head, qw, vw, key_dim),
            *_qkv_specs(CONV_W, lambda i, h: 0, head, qw, vw, key_dim),
            pl.BlockSpec((tc, vw), lambda i, h: (blk0 + i, z_col0 // vw + h)),
            pl.BlockSpec((tc, vw), lambda i, h: (blk0 + i, ga_col0 // vw + h)),
            pl.BlockSpec((tc, n_lanes), lambda i, h: (blk0 + i, 0)),
            pl.BlockSpec((n_lanes, tc), lambda i, h: (0, blk0 + i)),
            *_gate_param_specs(n_lanes, lambda i, h: (0, 0)),
            pl.BlockSpec((1, HEAD_DIM), lambda i, h: (0, 0)),
            state_spec,
        ],
        out_specs=[pl.BlockSpec((tc, vw), lambda i, h: (i, h)), state_spec],
        out_shape=[jax.ShapeDtypeStruct((rows, n_heads * HEAD_DIM), F32),
                   jax.ShapeDtypeStruct(s0.shape, F32)],
        compiler_params=_cparams("parallel", "parallel"),
        name="gdn_sample",
    )(*[proj_a] * 3, *[carried] * 3, *[conv_w] * 3, proj_a, proj_b, ab, ab_t, *_gate_params(alog, dtb, n_lanes),
      norm_w.reshape(1, HEAD_DIM), s0)


def _merge(other, gate, o):
    return (other.astype(F32) + jax.nn.sigmoid(gate) * o).astype(BF16)


def _swa_prompt_kernel(sinks_ref, q_ref, kc_ref, kp_ref, vc_ref, vp_ref, ma_ref, gb_ref, o_ref):
    g0 = pl.program_id(0) * SWA_STEP_GROUPS
    i = pl.program_id(1)
    rows = KV_GROUP * WINDOW
    r = lax.broadcasted_iota(jnp.int32, (rows, 2 * WINDOW), 0) % WINDOW
    c = lax.broadcasted_iota(jnp.int32, (rows, 2 * WINDOW), 1)
    visible = ((c < WINDOW) & (c > r) & (i > 0)) | ((c >= WINDOW) & (c - WINDOW <= r))
    head = lax.broadcasted_iota(jnp.int32, (rows, 1), 0) // WINDOW
    scale = HEAD_DIM ** -0.5
    groups = range(SWA_STEP_GROUPS)
    hcols = lambda g, hh: slice((g * KV_GROUP + hh) * HEAD_DIM, (g * KV_GROUP + hh + 1) * HEAD_DIM)
    gcols = lambda g: slice(g * HEAD_DIM, (g + 1) * HEAD_DIM)
    q4 = [jnp.concatenate([q_ref[:, hcols(g, hh)] for hh in range(KV_GROUP)], axis=0) for g in groups]
    kcat = [jnp.concatenate([kp_ref[:, gcols(g)], kc_ref[:, gcols(g)]], axis=0) for g in groups]
    vcat = [jnp.concatenate([vp_ref[:, gcols(g)], vc_ref[:, gcols(g)]], axis=0) for g in groups]
    s = [_dot_nt(q4[g], kcat[g]) for g in groups]
    p, den = [], []
    for g in groups:
        sink = jnp.zeros((rows, 1), F32)
        for hh in range(KV_GROUP):
            sink = jnp.where(head == hh, sinks_ref[(g0 + g) * KV_GROUP + hh], sink)
        sg = jnp.where(visible, s[g] * scale, -jnp.inf)
        m = jnp.maximum(jnp.max(sg, axis=-1, keepdims=True), sink)
        pg = jnp.exp(sg - m)
        p.append(pg)
        den.append(jnp.sum(pg, axis=-1, keepdims=True) + jnp.exp(sink - m))
    o4 = [_dot(p[g], vcat[g]) / den[g] for g in groups]
    for g in groups:
        for hh in range(KV_GROUP):
            cs = hcols(g, hh)
            o_ref[:, cs] = _merge(ma_ref[:, cs], gb_ref[:, cs], o4[g][hh * WINDOW:(hh + 1) * WINDOW])


def _swa_prompt(proj, merged_a, sinks, *, rows, n_kv_heads, k_col0, v_col0, gb_col0):
    sg = SWA_STEP_GROUPS
    qw = sg * KV_GROUP * HEAD_DIM
    kw = sg * HEAD_DIM
    prev = lambda i: jnp.maximum(i - 1, 0)
    return pl.pallas_call(
        _swa_prompt_kernel,
        grid=(n_kv_heads // sg, rows // WINDOW),
        in_specs=[
            pl.BlockSpec(memory_space=pltpu.SMEM),
            pl.BlockSpec((WINDOW, qw), lambda g, i: (i, g)),
            pl.BlockSpec((WINDOW, kw), lambda g, i: (i, k_col0 // kw + g)),
            pl.BlockSpec((WINDOW, kw), lambda g, i: (prev(i), k_col0 // kw + g)),
            pl.BlockSpec((WINDOW, kw), lambda g, i: (i, v_col0 // kw + g)),
            pl.BlockSpec((WINDOW, kw), lambda g, i: (prev(i), v_col0 // kw + g)),
            pl.BlockSpec((WINDOW, qw), lambda g, i: (i, g)),
            pl.BlockSpec((WINDOW, qw), lambda g, i: (i, gb_col0 // qw + g)),
        ],
        out_specs=pl.BlockSpec((WINDOW, qw), lambda g, i: (i, g)),
        out_shape=jax.ShapeDtypeStruct((rows, n_kv_heads * KV_GROUP * HEAD_DIM), BF16),
        compiler_params=_cparams("parallel", "parallel"),
        name="swa_prompt",
    )(sinks, proj, proj, proj, proj, proj, merged_a, proj)


def _swa_sample_kernel(sinks_ref, q_ref, kn_ref, vn_ref, ck_ref, cv_ref, ma_ref, gb_lo_ref, gb_hi_ref,
                       o_ref, ok_ref, ov_ref, *, seq):
    n_b, buf_rows, _ = ck_ref.shape
    kv_cols = kn_ref.shape[1]
    n_groups = kv_cols // HEAD_DIM
    w_buf = buf_rows // n_groups
    half = gb_lo_ref.shape[1]
    rows = KV_GROUP * seq
    t = lax.broadcasted_iota(jnp.int32, (rows, 1), 0) % seq
    head = lax.broadcasted_iota(jnp.int32, (rows, 1), 0) // seq
    jc = lax.broadcasted_iota(jnp.int32, (1, w_buf), 1)
    jn = lax.broadcasted_iota(jnp.int32, (1, seq), 1)
    dist_c = t + w_buf - jc
    cache_ok = (dist_c >= 0) & (dist_c < WINDOW)
    new_ok = jn <= t
    scale = HEAD_DIM ** -0.5
    head_slots = lambda g: pl.ds(g, w_buf, stride=n_groups)
    sinks = []
    for g in range(n_groups):
        sink = jnp.zeros((rows, 1), F32)
        for hh in range(KV_GROUP):
            sink = jnp.where(head == hh, sinks_ref[g * KV_GROUP + hh], sink)
        sinks.append(sink)

    def per_sequence(b, carry):
        r0 = pl.multiple_of(b * seq, seq)
        qb = q_ref[pl.ds(r0, seq), :]
        knb = kn_ref[pl.ds(r0, seq), :]
        vnb = vn_ref[pl.ds(r0, seq), :]
        mab = ma_ref[pl.ds(r0, seq), :]
        gbb = (gb_lo_ref[pl.ds(r0, seq), :], gb_hi_ref[pl.ds(r0, seq), :])
        groups = range(n_groups)
        gcols = lambda g: slice(g * HEAD_DIM, (g + 1) * HEAD_DIM)
        hcols = lambda g, hh: slice((g * KV_GROUP + hh) * HEAD_DIM, (g * KV_GROUP + hh + 1) * HEAD_DIM)
        q4 = [jnp.concatenate([qb[:, hcols(g, hh)] for hh in range(KV_GROUP)], axis=0) for g in groups]
        s_c = [_dot_nt(q4[g], ck_ref[b, head_slots(g), :]) for g in groups]
        s_n = [_dot_nt(q4[g], knb[:, gcols(g)]) for g in groups]
        p_c, p_n, den = [], [], []
        for g in groups:
            sc = jnp.where(cache_ok, s_c[g] * scale, -jnp.inf)
            sn = jnp.where(new_ok, s_n[g] * scale, -jnp.inf)
            m = jnp.maximum(jnp.maximum(jnp.max(sc, axis=-1, keepdims=True),
                                        jnp.max(sn, axis=-1, keepdims=True)), sinks[g])
            pc, pn = jnp.exp(sc - m), jnp.exp(sn - m)
            p_c.append(pc)
            p_n.append(pn)
            den.append(jnp.sum(pc, axis=-1, keepdims=True) + jnp.sum(pn, axis=-1, keepdims=True)
                       + jnp.exp(sinks[g] - m))
        o4 = [(_dot(p_c[g], cv_ref[b, head_slots(g), :]) + _dot(p_n[g], vnb[:, gcols(g)])) / den[g]
              for g in groups]
        for g in groups:
            for hh in range(KV_GROUP):
                cs = hcols(g, hh)
                gate = gbb[cs.start // half][:, cs.start % half:cs.start % half + HEAD_DIM]
                o_ref[pl.ds(r0, seq), cs] = _merge(mab[:, cs], gate, o4[g][hh * seq:(hh + 1) * seq])
        kept = (w_buf - seq) * n_groups
        ok_ref[b, 0:kept, :] = ck_ref[b, buf_rows - kept:buf_rows, :]
        ov_ref[b, 0:kept, :] = cv_ref[b, buf_rows - kept:buf_rows, :]
        for g in groups:
            new_slots = pl.ds(kept + g, seq, stride=n_groups)
            ok_ref[b, new_slots, :] = knb[:, gcols(g)]
            ov_ref[b, new_slots, :] = vnb[:, gcols(g)]
        return carry

    lax.fori_loop(0, n_b, per_sequence, 0)


def _swa_sample(proj, cache_k, cache_v, merged_a, sinks, *, row0, seq, k_col0, v_col0, gb_col0, nb=8):
    n_seq, buf_rows, _ = cache_k.shape
    q_cols = merged_a.shape[1]
    kv_cols = q_cols // KV_GROUP
    half = q_cols // 2
    tr = nb * seq
    blk0 = row0 // tr
    cache_spec = pl.BlockSpec((nb, buf_rows, HEAD_DIM), lambda i: (i, 0, 0))
    return pl.pallas_call(
        functools.partial(_swa_sample_kernel, seq=seq),
        grid=(n_seq // nb,),
        in_specs=[
            pl.BlockSpec(memory_space=pltpu.SMEM),
            pl.BlockSpec((tr, q_cols), lambda i: (blk0 + i, 0)),
            pl.BlockSpec((tr, kv_cols), lambda i: (blk0 + i, k_col0 // kv_cols)),
            pl.BlockSpec((tr, kv_cols), lambda i: (blk0 + i, v_col0 // kv_cols)),
            cache_spec, cache_spec,
            pl.BlockSpec((tr, q_cols), lambda i: (i, 0)),
            pl.BlockSpec((tr, half), lambda i: (blk0 + i, gb_col0 // half)),
            pl.BlockSpec((tr, half), lambda i: (blk0 + i, gb_col0 // half + 1)),
        ],
        out_specs=[pl.BlockSpec((tr, q_cols), lambda i: (i, 0)), cache_spec, cache_spec],
        out_shape=[jax.ShapeDtypeStruct((n_seq * seq, q_cols), BF16),
                   jax.ShapeDtypeStruct(cache_k.shape, F32),
                   jax.ShapeDtypeStruct(cache_v.shape, F32)],
        compiler_params=_cparams("parallel"),
        name="swa_sample",
    )(sinks, proj, proj, proj, cache_k, cache_v, merged_a, proj, proj)


def _outproj_kernel(x_ref, mp_ref, ms_ref, w_ref, o_ref, *, n_prompt_tiles):
    i = pl.program_id(0)

    @pl.when(i < n_prompt_tiles)
    def _():
        o_ref[...] = x_ref[...] + jnp.dot(mp_ref[...], w_ref[...], preferred_element_type=F32)

    @pl.when(i >= n_prompt_tiles)
    def _():
        o_ref[...] = x_ref[...] + jnp.dot(ms_ref[...], w_ref[...], preferred_element_type=F32)


def _outproj(x, merged_p, merged_s, w_out, *, tm=512, tn=1024):
    m, d = x.shape
    npt = merged_p.shape[0] // tm
    return pl.pallas_call(
        functools.partial(_outproj_kernel, n_prompt_tiles=npt),
        grid=(m // tm, d // tn),
        in_specs=[
            pl.BlockSpec((tm, tn), lambda i, j: (i, j)),
            pl.BlockSpec((tm, d), lambda i, j: (jnp.minimum(i, npt - 1), 0)),
            pl.BlockSpec((tm, d), lambda i, j: (jnp.maximum(i - npt, 0), 0)),
            pl.BlockSpec((d, tn), lambda i, j: (0, j)),
        ],
        out_specs=pl.BlockSpec((tm, tn), lambda i, j: (i, j)),
        out_shape=jax.ShapeDtypeStruct((m, d), F32),
        compiler_params=_cparams("parallel", "arbitrary"),
        name="outproj",
    )(x, merged_p, merged_s, w_out)


def _ple_kernel(xf_ref, x_ref, pe_ref, nw_ref, wp_ref, wg_ref, o_head_ref, o_tail_ref, xn_ref, *, n_head_tiles):
    i = pl.program_id(0)

    @pl.when(pl.program_id(1) == 0)
    def _():
        xn_ref[...] = _rms(xf_ref[...], nw_ref[...]).astype(BF16)

    def emit(o_ref):
        xn = xn_ref[...]
        pe = pe_ref[...]
        for c in range(0, o_ref.shape[1], PLE_OUT_CHUNK):
            cs = slice(c, c + PLE_OUT_CHUNK)
            gate = jnp.dot(xn, wg_ref[:, cs], preferred_element_type=F32)
            emb = jnp.dot(pe, wp_ref[:, cs], preferred_element_type=F32)
            o_ref[:, cs] = x_ref[:, cs] + emb * jax.nn.sigmoid(gate)

    @pl.when(i < n_head_tiles)
    def _():
        emit(o_head_ref)

    @pl.when(i >= n_head_tiles)
    def _():
        emit(o_tail_ref)


def _ple(x, pe, norm_w, w_proj, w_gate, *, head_rows, tm=512, tn=1024):
    m, d = x.shape
    pdim = pe.shape[1]
    nh = head_rows // tm
    nj = d // tn
    head_idx = lambda i, j: (jnp.minimum(i, nh - 1), jnp.where(i < nh, j, nj - 1))
    tail_idx = lambda i, j: (jnp.maximum(i - nh, 0), jnp.where(i >= nh, j, 0))
    return pl.pallas_call(
        functools.partial(_ple_kernel, n_head_tiles=nh),
        grid=(m // tm, nj),
        in_specs=[
            pl.BlockSpec((tm, d), lambda i, j: (i, 0), pipeline_mode=pl.Buffered(1)),
            pl.BlockSpec((tm, tn), lambda i, j: (i, j)),
            pl.BlockSpec((tm, pdim), lambda i, j: (i, 0)),
            pl.BlockSpec((1, d), lambda i, j: (0, 0)),
            pl.BlockSpec((pdim, tn), lambda i, j: (0, j)),
            pl.BlockSpec((d, tn), lambda i, j: (0, j)),
        ],
        out_specs=[pl.BlockSpec((tm, tn), head_idx), pl.BlockSpec((tm, tn), tail_idx)],
        out_shape=[jax.ShapeDtypeStruct((head_rows, d), F32), jax.ShapeDtypeStruct((m - head_rows, d), F32)],
        scratch_shapes=[pltpu.VMEM((tm, d), BF16)],
        compiler_params=_cparams("arbitrary", "arbitrary"),
        name="ple",
    )(x, x, pe, norm_w.reshape(1, d), w_proj, w_gate)


def _rope_tables(seq, dec_batch, dec_seq):
    half = HEAD_DIM // 2
    inv_freq = ROPE_THETA ** (-jnp.arange(half, dtype=F32) / half)
    pos_p = jnp.arange(seq, dtype=jnp.int32)
    pos_s = jnp.tile(PAST_LEN + jnp.arange(dec_seq, dtype=jnp.int32), dec_batch)
    ang = jnp.concatenate([pos_p, pos_s]).astype(F32)[:, None] * inv_freq[None, :]
    cos, sin = jnp.cos(ang), jnp.sin(ang)
    return jnp.concatenate([cos, cos], axis=1), jnp.concatenate([-sin, sin], axis=1)


def kernel(x_prompt, x_sample, p_prompt, p_sample, state_gdn, state_conv, cache_swa_k, cache_swa_v, ffn1_norm, ffn1_w_gate, ffn1_w_up, ffn1_w_down, mix_norm, w_in, conv_w, A_log, dt_bias, gdn_norm_w, q_norm_w, k_norm_w, sinks, w_out, ffn2_norm, ffn2_w_gate, ffn2_w_up, ffn2_w_down, ple_norm, w_ple_proj, w_ple_gate):
    depth, dec_batch, n_v_heads, dk, dv = state_gdn.shape
    assert depth == 1 and dk == HEAD_DIM and dv == HEAD_DIM
    bp, seq, d = x_prompt.shape
    assert bp == 1
    dec_seq = x_sample.shape[1]
    w_buf, n_kv_heads = cache_swa_k.shape[2], cache_swa_k.shape[3]
    n_heads = sinks.shape[1]
    assert n_heads == n_kv_heads * KV_GROUP and w_buf == WINDOW and CHUNK % dec_seq == 0
    conv_dim = conv_w.shape[2]
    key_dim = n_v_heads // GDN_REP * HEAD_DIM
    val_dim = n_v_heads * HEAD_DIM
    q_dim = n_heads * HEAD_DIM
    kv_cols = n_kv_heads * HEAD_DIM
    n_samp = dec_batch * dec_seq

    ab0 = conv_dim + val_dim
    b0 = ab0 + 2 * n_v_heads
    assert w_in.shape[2] == b0 + q_dim + 2 * kv_cols + 2 * d
    w_all = w_in[0].astype(BF16)
    w_b = w_all[:, b0:]
    n_ab = 128
    w_ab = jnp.pad(w_all[:, ab0:b0], ((0, 0), (0, n_ab - 2 * n_v_heads)))
    z_col0 = conv_dim
    k_col0, v_col0 = q_dim, q_dim + kv_cols
    ga_col0 = q_dim + 2 * kv_cols
    gb_col0 = ga_col0 + d

    bf = lambda w: w[0].astype(BF16)
    pe = jnp.concatenate([p_prompt.reshape(seq, PLE_DIM), p_sample.reshape(n_samp, PLE_DIM)], axis=0)

    x = _ffn([x_prompt.reshape(seq, d), x_sample.reshape(n_samp, d)], ffn1_norm[0],
             bf(ffn1_w_gate), bf(ffn1_w_up), bf(ffn1_w_down))
    proj_a, ab = _inproj_ab(x, mix_norm[0], w_all, w_ab, n=ab0)
    cos, sin = _rope_tables(seq, dec_batch, dec_seq)
    proj_b = _inproj_rope(x, mix_norm[0], w_b, cos, sin, q_norm_w[0], k_norm_w[0], q_cols=q_dim, k_cols=kv_cols)

    carried = jnp.pad(state_conv[0], ((0, 0), (dec_seq - (CONV_W - 1), 0), (0, 0))).reshape(n_samp, conv_dim)
    ab_t = ab.T
    gdn_args = dict(n_heads=n_v_heads, z_col0=z_col0, ga_col0=ga_col0)
    ma_p, sg_p = _gdn_prompt(proj_a, proj_b, conv_w[0], ab, ab_t, A_log[0], dt_bias[0], gdn_norm_w[0],
                             rows=seq, **gdn_args)
    ma_s, sg_s = _gdn_sample(proj_a, proj_b, carried, conv_w[0], ab, ab_t, A_log[0], dt_bias[0], gdn_norm_w[0],
                             state_gdn[0], row0=seq, rows=n_samp, seq=dec_seq, **gdn_args)

    swa_args = dict(k_col0=k_col0, v_col0=v_col0, gb_col0=gb_col0)
    merged_p = _swa_prompt(proj_b, ma_p, sinks[0], rows=seq, n_kv_heads=n_kv_heads, **swa_args)
    merged_s, kk_s, vv_s = _swa_sample(proj_b, cache_swa_k[0].reshape(dec_batch, w_buf * n_kv_heads, HEAD_DIM),
                                       cache_swa_v[0].reshape(dec_batch, w_buf * n_kv_heads, HEAD_DIM),
                                       ma_s, sinks[0], row0=seq, seq=dec_seq, **swa_args)

    x = _outproj(x, merged_p, merged_s, bf(w_out))
    x = _ffn([x], ffn2_norm[0], bf(ffn2_w_gate), bf(ffn2_w_up), bf(ffn2_w_down))
    y_p, y_s = _ple(x, pe.astype(BF16), ple_norm[0], bf(w_ple_proj), bf(w_ple_gate), head_rows=seq)

    cache_shape = (1, 1, w_buf, n_kv_heads, HEAD_DIM)
    return (
        y_p.reshape(1, seq, d),
        y_s.reshape(dec_batch, dec_seq, d),
        sg_p.reshape(1, 1, n_v_heads, HEAD_DIM, HEAD_DIM),
        proj_a[seq - (CONV_W - 1):seq, :conv_dim].reshape(1, 1, CONV_W - 1, conv_dim),
        proj_b[seq - w_buf:seq, k_col0:k_col0 + kv_cols].reshape(cache_shape),
        proj_b[seq - w_buf:seq, v_col0:v_col0 + kv_cols].reshape(cache_shape),
        sg_s.reshape(1, dec_batch, n_v_heads, HEAD_DIM, HEAD_DIM),
        proj_a[seq:, :conv_dim].reshape(dec_batch, dec_seq, conv_dim)[:, dec_seq - (CONV_W - 1):].reshape(
            1, dec_batch, CONV_W - 1, conv_dim),
        kk_s.reshape(1, dec_batch, w_buf, n_kv_heads, HEAD_DIM),
        vv_s.reshape(1, dec_batch, w_buf, n_kv_heads, HEAD_DIM),
    )
```

```python
import functools

import jax
import jax.numpy as jnp
from jax import lax
from jax.experimental import pallas as pl
from jax.experimental.pallas import tpu as pltpu

F32 = jnp.float32
BF16 = jnp.bfloat16

EPS = 1e-6
HEAD_DIM = 128
KV_GROUP = 4
WINDOW = 128
ROPE_THETA = 10000.0
PAST_LEN = 8192
GDN_REP = 2
CONV_W = 4
CHUNK = 64
PLE_DIM = 256
GDN_STEP_HEADS = 8
PLE_OUT_CHUNK = 256
FFN_OUT_CHUNK = 1024
SWA_STEP_GROUPS = 4

V7X_VMEM_LIMIT_BYTES = 62 * 1024 * 1024


def _cparams(*sem):
    return pltpu.CompilerParams(dimension_semantics=sem, vmem_limit_bytes=V7X_VMEM_LIMIT_BYTES)


def _rms(x, w):
    return x * lax.rsqrt(jnp.mean(x * x, axis=-1, keepdims=True) + EPS) * w


def _dot(a, b):
    return jnp.dot(a.astype(BF16), b.astype(BF16), preferred_element_type=F32)


def _dot_nt(a, b):
    return lax.dot_general(a.astype(BF16), b.astype(BF16), (((1,), (1,)), ((), ())),
                           preferred_element_type=F32)


def _dot_tn(a, b):
    return lax.dot_general(a.astype(BF16), b.astype(BF16), (((0,), (0,)), ((), ())),
                           preferred_element_type=F32)


def _row_part_specs(parts, tm, width):
    specs, spans, start = [], [], 0
    for part in parts:
        n = part.shape[0] // tm
        specs.append(pl.BlockSpec(
            (tm, width), functools.partial(lambda i, j, s, n: (jnp.clip(i - s, 0, n - 1), 0), s=start, n=n),
            pipeline_mode=pl.Buffered(1)))
        spans.append((start, n))
        start += n
    return specs, spans


def _ffn_kernel(*refs, spans):
    n_parts = len(spans)
    x_refs = refs[:n_parts]
    nw_ref, wg_ref, wu_ref, wd_ref, o_ref, xn_ref = refs[n_parts:]
    i = pl.program_id(0)
    j = pl.program_id(1)
    for x_ref, (start, n) in zip(x_refs, spans):
        @pl.when((j == 0) & (i >= start) & (i < start + n))
        def _():
            x = x_ref[...]
            xn_ref[...] = _rms(x, nw_ref[...]).astype(BF16)
            o_ref[...] = x

    xn = xn_ref[...]
    g = jnp.dot(xn, wg_ref[...], preferred_element_type=F32)
    u = jnp.dot(xn, wu_ref[...], preferred_element_type=F32)
    h = ((0.5 * (g * jax.nn.sigmoid(g))) * u).astype(BF16)
    d = o_ref.shape[1]
    for c in range(0, d, FFN_OUT_CHUNK):
        cs = slice(c, min(c + FFN_OUT_CHUNK, d))
        o_ref[:, cs] += jnp.dot(h, wd_ref[:, cs], preferred_element_type=F32)


def _ffn(x_parts, norm_w, wg, wu, wd, *, tm=512, tf=256):
    d = x_parts[0].shape[1]
    m = sum(p.shape[0] for p in x_parts)
    f = wg.shape[1]
    x_specs, spans = _row_part_specs(x_parts, tm, d)
    return pl.pallas_call(
        functools.partial(_ffn_kernel, spans=spans),
        grid=(m // tm, f // tf),
        in_specs=[
            *x_specs,
            pl.BlockSpec((1, d), lambda i, j: (0, 0)),
            pl.BlockSpec((d, tf), lambda i, j: (0, j)),
            pl.BlockSpec((d, tf), lambda i, j: (0, j)),
            pl.BlockSpec((tf, d), lambda i, j: (j, 0)),
        ],
        out_specs=pl.BlockSpec((tm, d), lambda i, j: (i, 0)),
        out_shape=jax.ShapeDtypeStruct((m, d), F32),
        scratch_shapes=[pltpu.VMEM((tm, d), BF16)],
        compiler_params=_cparams("parallel", "arbitrary"),
        name="ffn",
    )(*x_parts, norm_w.reshape(1, d), wg, wu, wd)


def _inproj_ab_kernel(x_ref, nw_ref, w_ref, wab_ref, o_ref, ab_ref, xn_ref):
    @pl.when(pl.program_id(1) == 0)
    def _():
        xn = _rms(x_ref[...], nw_ref[...]).astype(BF16)
        xn_ref[...] = xn
        ab_ref[...] = jnp.dot(xn, wab_ref[...], preferred_element_type=F32)

    o_ref[...] = jnp.dot(xn_ref[...], w_ref[...], preferred_element_type=F32)


def _inproj_rope_kernel(x_ref, nw_ref, w_ref, cos_ref, sin_ref, qw_ref, kw_ref, o_ref, xn_ref, *,
                        n_q_blocks, n_rope_blocks):
    j = pl.program_id(1)

    @pl.when(j == 0)
    def _():
        xn_ref[...] = _rms(x_ref[...], nw_ref[...]).astype(BF16)

    @pl.when(j < n_rope_blocks)
    def _():
        w = jnp.where(j < n_q_blocks, qw_ref[...], kw_ref[...])
        cos = cos_ref[...]
        sin = sin_ref[...]
        xn = xn_ref[...]
        step = 2 * HEAD_DIM
        for c in range(0, o_ref.shape[1], step):
            acc = jnp.dot(xn, w_ref[:, c:c + step], preferred_element_type=F32)
            for h in range(step // HEAD_DIM):
                y = _rms(acc[:, h * HEAD_DIM:(h + 1) * HEAD_DIM], w)
                cs = slice(c + h * HEAD_DIM, c + (h + 1) * HEAD_DIM)
                o_ref[:, cs] = y * cos + pltpu.roll(y, HEAD_DIM // 2, 1) * sin

    @pl.when(j >= n_rope_blocks)
    def _():
        o_ref[...] = jnp.dot(xn_ref[...], w_ref[...], preferred_element_type=F32)


def _inproj_rope(x, norm_w, w, cos, sin, q_norm_w, k_norm_w, *, q_cols, k_cols, tm=512, tn=1024):
    m, d = x.shape
    n = w.shape[1]
    row = lambda i, j: (i, 0)
    fixed = lambda i, j: (0, 0)
    return pl.pallas_call(
        functools.partial(_inproj_rope_kernel, n_q_blocks=q_cols // tn, n_rope_blocks=(q_cols + k_cols) // tn),
        grid=(m // tm, n // tn),
        in_specs=[
            pl.BlockSpec((tm, d), row),
            pl.BlockSpec((1, d), fixed),
            pl.BlockSpec((d, tn), lambda i, j: (0, j)),
            pl.BlockSpec((tm, HEAD_DIM), row),
            pl.BlockSpec((tm, HEAD_DIM), row),
            pl.BlockSpec((1, HEAD_DIM), fixed),
            pl.BlockSpec((1, HEAD_DIM), fixed),
        ],
        out_specs=pl.BlockSpec((tm, tn), lambda i, j: (i, j)),
        out_shape=jax.ShapeDtypeStruct((m, n), F32),
        scratch_shapes=[pltpu.VMEM((tm, d), BF16)],
        compiler_params=_cparams("parallel", "arbitrary"),
        name="inproj_rope",
    )(x, norm_w.reshape(1, d), w, cos, sin, q_norm_w.reshape(1, HEAD_DIM), k_norm_w.reshape(1, HEAD_DIM))


def _inproj_ab(x, norm_w, w, w_ab, *, n, tm=512, tn=1024):
    m, d = x.shape
    nab = w_ab.shape[1]
    return pl.pallas_call(
        _inproj_ab_kernel,
        grid=(m // tm, n // tn),
        in_specs=[
            pl.BlockSpec((tm, d), lambda i, j: (i, 0)),
            pl.BlockSpec((1, d), lambda i, j: (0, 0)),
            pl.BlockSpec((d, tn), lambda i, j: (0, j)),
            pl.BlockSpec((d, nab), lambda i, j: (0, 0)),
        ],
        out_specs=[pl.BlockSpec((tm, tn), lambda i, j: (i, j)), pl.BlockSpec((tm, nab), lambda i, j: (i, 0))],
        out_shape=[jax.ShapeDtypeStruct((m, n), F32), jax.ShapeDtypeStruct((m, nab), F32)],
        scratch_shapes=[pltpu.VMEM((tm, d), BF16)],
        compiler_params=_cparams("parallel", "arbitrary"),
        name="inproj_ab",
    )(x, norm_w.reshape(1, d), w, w_ab)


def _conv_silu(x, w, shifted):
    y = None
    for tap in range(CONV_W):
        sh = CONV_W - 1 - tap
        term = (x if sh == 0 else shifted(sh)) * w[tap:tap + 1]
        y = term if y is None else y + term
    return y * jax.nn.sigmoid(y)


def _shift_in_sequence(x, halo):
    row8 = lax.broadcasted_iota(jnp.int32, (8, 1), 0)

    def shifted(sh):
        rolled = pltpu.roll(x, sh, 0)
        head = jnp.where(row8 < sh, pltpu.roll(halo, sh, 0), rolled[:8])
        return jnp.concatenate([head, rolled[8:]], axis=0)

    return shifted


def _shift_in_groups(x, carried, seq):
    rows = x.shape[0]
    t = lax.broadcasted_iota(jnp.int32, (rows, 1), 0) % seq
    return lambda sh: jnp.where(t >= sh, pltpu.roll(x, sh, 0), pltpu.roll(carried, rows + sh - seq, 0))


def _l2norm_heads(y, scale):
    heads = []
    for h in range(y.shape[1] // HEAD_DIM):
        yh = y[:, h * HEAD_DIM:(h + 1) * HEAD_DIM]
        heads.append(yh * (lax.rsqrt(jnp.sum(yh * yh, axis=-1, keepdims=True) + EPS) * scale))
    return jnp.concatenate(heads, axis=1)


def _gdn_activations(pre, conv_w, shifts):
    q, k, v = (_conv_silu(x, w, s) for x, w, s in zip(pre, conv_w, shifts))
    return _l2norm_heads(q, HEAD_DIM ** -0.5), _l2norm_heads(k, 1.0), v


def _softplus(x):
    return jnp.maximum(x, 0.0) + jnp.log1p(jnp.exp(-jnp.abs(x)))


def _seg_cumsum(x, pos, seg, axis):
    d = 1
    while d < seg:
        x = x + jnp.where(pos >= d, pltpu.roll(x, d, axis), 0.0)
        d *= 2
    return x


def _gates(ab, abt8, alog_row, dtb_row, alog_col8, dtb_col8, seg):
    rows = ab.shape[0]
    pos_c = lax.broadcasted_iota(jnp.int32, (rows, 1), 0) % seg
    pos_r = lax.broadcasted_iota(jnp.int32, (1, rows), 1) % seg
    g_all = -jnp.exp(alog_row) * _softplus(ab + dtb_row)
    g_t = -jnp.exp(alog_col8) * _softplus(abt8 + dtb_col8)
    return _seg_cumsum(g_all, pos_c, seg, 0), jax.nn.sigmoid(ab), _seg_cumsum(g_t, pos_r, seg, 1)


def _pick_lane(x, idx):
    lane = lax.broadcasted_iota(jnp.int32, (1, x.shape[1]), 1)
    return jnp.sum(jnp.where(lane == idx, x, 0.0), axis=1, keepdims=True)


def _pick_sublane(x, idx):
    sub = lax.broadcasted_iota(jnp.int32, (x.shape[0], 1), 0)
    return jnp.sum(jnp.where(sub == idx, x, 0.0), axis=0, keepdims=True)


def _head_gates(gate_vals, h, n_heads):
    gc_all, sig_all, gc_t = gate_vals
    return _pick_lane(sig_all, n_heads + h), _pick_lane(gc_all, h), _pick_sublane(gc_t, h % 8)


def _load_gates(ab_ref, abt_ref, alr_ref, dtr_ref, alc_ref, dtc_ref, h0, seg):
    h8 = pl.multiple_of((h0 // 8) * 8, 8)
    return _gates(ab_ref[...], abt_ref[pl.ds(h8, 8), :], alr_ref[...], dtr_ref[...],
                  alc_ref[pl.ds(h8, 8), :], dtc_ref[pl.ds(h8, 8), :], seg)


def _chunk_masks(seg):
    r = lax.broadcasted_iota(jnp.int32, (CHUNK, CHUNK), 0)
    c = lax.broadcasted_iota(jnp.int32, (CHUNK, CHUNK), 1)
    same = (r // seg) == (c // seg)
    return same & (c <= r), same & (c < r), r == c


def _gdn_intra(q_all, k_all, v_all, gates, seg):
    incl, strict, eye = _chunk_masks(seg)
    n_chunks = q_all.shape[0] // CHUNK
    n_heads = len(gates)
    chunks = range(n_chunks)
    heads = range(n_heads)
    items = [(n, hh) for n in chunks for hh in heads]
    rows = lambda n: slice(n * CHUNK, (n + 1) * CHUNK)
    cols = lambda j: slice(j * HEAD_DIM, (j + 1) * HEAD_DIM)
    pairs = [(n, j) for n in chunks for j in range(n_heads // GDN_REP)]
    q = {(n, j): q_all[rows(n), cols(j)] for n, j in pairs}
    k = {(n, j): k_all[rows(n), cols(j)] for n, j in pairs}
    kk_qk = {p: _dot_nt(jnp.concatenate([k[p], q[p]], axis=0), k[p]) for p in pairs}
    kk = {p: kk_qk[p][:CHUNK] for p in pairs}
    qk = {p: kk_qk[p][CHUNK:] for p in pairs}
    out = {}
    low, rhs = {}, {}
    for n, hh in items:
        beta, gc_col, gc_row = gates[hh]
        beta, gc_col, gc_row = beta[rows(n)], gc_col[rows(n)], gc_row[:, rows(n)]
        pj = (n, hh // GDN_REP)
        decay = jnp.exp(jnp.where(incl, gc_col - gc_row, -jnp.inf))
        low[n, hh] = jnp.where(strict, beta * kk[pj] * decay, 0.0)
        e_gc = jnp.exp(gc_col)
        rhs[n, hh] = jnp.concatenate([v_all[rows(n), cols(hh)] * beta, k[pj] * (beta * e_gc)], axis=1)
        out[n, hh] = dict(a=jnp.where(incl, qk[pj] * decay, 0.0), qd=q[pj] * e_gc, gc=gc_col, k=k[pj])
    p = {it: -low[it] for it in items}
    t = {it: jnp.where(eye, 1.0, p[it]) for it in items}
    if seg > 2:
        p = {it: _dot(p[it], p[it]) for it in items}
        n_pow = 4
        while n_pow < seg:
            both = {it: _dot(jnp.concatenate([t[it], p[it]], axis=0), p[it]) for it in items}
            t = {it: t[it] + both[it][:CHUNK] for it in items}
            p = {it: both[it][CHUNK:] for it in items}
            n_pow *= 2
        t = {it: t[it] + _dot(t[it], p[it]) for it in items}
    for it in items:
        uw = _dot(t[it], rhs[it])
        out[it]["u"] = uw[:, :HEAD_DIM]
        out[it]["w"] = uw[:, HEAD_DIM:]
    return out


def _gdn_emit(o, z, gate, nw):
    return (_rms(o, nw) * (z * jax.nn.sigmoid(z)) * jax.nn.sigmoid(gate)).astype(BF16)


def _gdn_prompt_kernel(q_ref, k_ref, v_ref, qh_ref, kh_ref, vh_ref, qw_ref, kw_ref, vw_ref, z_ref, ga_ref,
                       ab_ref, abt_ref, alr_ref, dtr_ref, alc_ref, dtc_ref, nw_ref, o_ref, sfin_ref, s_ref, *,
                       n_heads):
    h0 = pl.program_id(0) * GDN_STEP_HEADS
    i = pl.program_id(1)

    @pl.when(i == 0)
    def _():
        s_ref[...] = jnp.zeros_like(s_ref)

    nw = nw_ref[...]
    heads = range(GDN_STEP_HEADS)
    gate_vals = _load_gates(ab_ref, abt_ref, alr_ref, dtr_ref, alc_ref, dtc_ref, h0, CHUNK)
    gates = [_head_gates(gate_vals, h0 + hh, n_heads) for hh in heads]
    pre = [r[...] for r in (q_ref, k_ref, v_ref)]
    halos = [jnp.where(i > 0, r[...], 0.0) for r in (qh_ref, kh_ref, vh_ref)]
    taps = [r[...] for r in (qw_ref, kw_ref, vw_ref)]
    q, k, v = _gdn_activations(pre, taps, [_shift_in_sequence(x, h) for x, h in zip(pre, halos)])
    intra = _gdn_intra(q, k, v, gates, CHUNK)
    state = [s_ref[hh] for hh in heads]
    for n in range(q.shape[0] // CHUNK):
        rows = slice(n * CHUNK, (n + 1) * CHUNK)
        c = [intra[n, hh] for hh in heads]
        ws_qs = [_dot(jnp.concatenate([c[hh]["w"], c[hh]["qd"]], axis=0), state[hh]) for hh in heads]
        v_new = [c[hh]["u"] - ws_qs[hh][:CHUNK] for hh in heads]
        g_last = [c[hh]["gc"][CHUNK - 1:CHUNK] for hh in heads]
        k_st = [(c[hh]["k"] * jnp.exp(g_last[hh] - c[hh]["gc"])).T for hh in heads]
        av_kv = [_dot(jnp.concatenate([c[hh]["a"], k_st[hh]], axis=0), v_new[hh]) for hh in heads]
        o = [ws_qs[hh][CHUNK:] + av_kv[hh][:CHUNK] for hh in heads]
        state = [state[hh] * jnp.exp(g_last[hh]) + av_kv[hh][CHUNK:] for hh in heads]
        for hh in heads:
            cs = slice(hh * HEAD_DIM, (hh + 1) * HEAD_DIM)
            o_ref[rows, cs] = _gdn_emit(o[hh], z_ref[rows, cs], ga_ref[rows, cs], nw)
    for hh in heads:
        s_ref[hh] = state[hh]

    @pl.when(i == pl.num_programs(1) - 1)
    def _():
        sfin_ref[...] = s_ref[...]


def _gate_param_specs(n_lanes, idx):
    row = pl.BlockSpec((1, n_lanes), idx)
    col = pl.BlockSpec((n_lanes, 1), idx)
    return [row, row, col, col]


def _gate_params(alog, dtb, n_lanes):
    pad = lambda v: jnp.pad(v.astype(F32), (0, n_lanes - v.shape[0]))
    al, dt = pad(alog), pad(dtb)
    return al.reshape(1, n_lanes), dt.reshape(1, n_lanes), al.reshape(n_lanes, 1), dt.reshape(n_lanes, 1)


def _qkv_specs(block_rows, row_idx, head_idx, qw, vw, key_dim):
    return [
        pl.BlockSpec((block_rows, qw), lambda a, b: (row_idx(a, b), head_idx(a, b))),
        pl.BlockSpec((block_rows, qw), lambda a, b: (row_idx(a, b), key_dim // qw + head_idx(a, b))),
        pl.BlockSpec((block_rows, vw), lambda a, b: (row_idx(a, b), 2 * key_dim // vw + head_idx(a, b))),
    ]


def _gdn_prompt(proj_a, proj_b, conv_w, ab, ab_t, alog, dtb, norm_w, *, rows, n_heads, z_col0, ga_col0, tc=512):
    hs = GDN_STEP_HEADS
    qw = hs // GDN_REP * HEAD_DIM
    vw = hs * HEAD_DIM
    key_dim = n_heads // GDN_REP * HEAD_DIM
    n_lanes = ab.shape[1]
    tc8 = tc // 8
    head = lambda h, i: h
    return pl.pallas_call(
        functools.partial(_gdn_prompt_kernel, n_heads=n_heads),
        grid=(n_heads // hs, rows // tc),
        in_specs=[
            *_qkv_specs(tc, lambda h, i: i, head, qw, vw, key_dim),
            *_qkv_specs(8, lambda h, i: jnp.maximum(i * tc8 - 1, 0), head, qw, vw, key_dim),
            *_qkv_specs(CONV_W, lambda h, i: 0, head, qw, vw, key_dim),
            pl.BlockSpec((tc, vw), lambda h, i: (i, z_col0 // vw + h)),
            pl.BlockSpec((tc, vw), lambda h, i: (i, ga_col0 // vw + h)),
            pl.BlockSpec((tc, n_lanes), lambda h, i: (i, 0)),
            pl.BlockSpec((n_lanes, tc), lambda h, i: (0, i)),
            *_gate_param_specs(n_lanes, lambda h, i: (0, 0)),
            pl.BlockSpec((1, HEAD_DIM), lambda h, i: (0, 0)),
        ],
        out_specs=[
            pl.BlockSpec((tc, vw), lambda h, i: (i, h)),
            pl.BlockSpec((hs, HEAD_DIM, HEAD_DIM), lambda h, i: (h, 0, 0)),
        ],
        out_shape=[jax.ShapeDtypeStruct((rows, n_heads * HEAD_DIM), BF16),
                   jax.ShapeDtypeStruct((n_heads, HEAD_DIM, HEAD_DIM), F32)],
        scratch_shapes=[pltpu.VMEM((hs, HEAD_DIM, HEAD_DIM), F32)],
        compiler_params=_cparams("parallel", "arbitrary"),
        name="gdn_prompt",
    )(*[proj_a] * 6, *[conv_w] * 3, proj_a, proj_b, ab, ab_t, *_gate_params(alog, dtb, n_lanes),
      norm_w.reshape(1, HEAD_DIM))


def _gdn_sample_kernel(q_ref, k_ref, v_ref, qc_ref, kc_ref, vc_ref, qw_ref, kw_ref, vw_ref, z_ref, ga_ref,
                       ab_ref, abt_ref, alr_ref, dtr_ref, alc_ref, dtc_ref, nw_ref, s0_ref, o_ref, s1_ref, *,
                       n_heads, seq):
    h0 = pl.program_id(1) * GDN_STEP_HEADS
    nw = nw_ref[...]
    heads = range(GDN_STEP_HEADS)
    gate_vals = _load_gates(ab_ref, abt_ref, alr_ref, dtr_ref, alc_ref, dtc_ref, h0, seq)
    gates = [_head_gates(gate_vals, h0 + hh, n_heads) for hh in heads]
    pre = [r[...] for r in (q_ref, k_ref, v_ref)]
    carried = [r[...] for r in (qc_ref, kc_ref, vc_ref)]
    taps = [r[...] for r in (qw_ref, kw_ref, vw_ref)]
    q, k, v = _gdn_activations(pre, taps, [_shift_in_groups(x, c, seq) for x, c in zip(pre, carried)])
    intra = _gdn_intra(q, k, v, gates, seq)
    per_chunk = CHUNK // seq
    for n in range(q.shape[0] // CHUNK):
        rows = slice(n * CHUNK, (n + 1) * CHUNK)
        seqs = [(hh, b) for hh in heads for b in range(per_chunk)]
        sub = lambda b: slice(b * seq, (b + 1) * seq)
        c = [intra[n, hh] for hh in heads]
        s0 = {(hh, b): s0_ref[n * per_chunk + b, hh] for hh, b in seqs}
        ws_qs = {(hh, b): _dot(jnp.concatenate([c[hh]["w"][sub(b)], c[hh]["qd"][sub(b)]], axis=0), s0[hh, b])
                 for hh, b in seqs}
        v_new = {(hh, b): c[hh]["u"][sub(b)] - ws_qs[hh, b][:seq] for hh, b in seqs}
        for hh, b in seqs:
            gc = c[hh]["gc"][sub(b)]
            g_last = gc[seq - 1:seq]
            k_st = c[hh]["k"][sub(b)] * jnp.exp(g_last - gc)
            s1_ref[n * per_chunk + b, hh] = s0[hh, b] * jnp.exp(g_last) + _dot_tn(k_st, v_new[hh, b])
        for hh in heads:
            cs = slice(hh * HEAD_DIM, (hh + 1) * HEAD_DIM)
            qs = jnp.concatenate([ws_qs[hh, b][seq:] for b in range(per_chunk)], axis=0)
            vn = jnp.concatenate([v_new[hh, b] for b in range(per_chunk)], axis=0)
            out = _gdn_emit(qs + _dot(c[hh]["a"], vn), z_ref[rows, cs], ga_ref[rows, cs], nw)
            o_ref[rows, cs] = out.astype(o_ref.dtype)


def _gdn_sample(proj_a, proj_b, carried, conv_w, ab, ab_t, alog, dtb, norm_w, s0, *, row0, rows, seq, n_heads,
                z_col0, ga_col0, tc=128):
    hs = GDN_STEP_HEADS
    head = lambda i, h: h
    qw = hs // GDN_REP * HEAD_DIM
    vw = hs * HEAD_DIM
    key_dim = n_heads // GDN_REP * HEAD_DIM
    blk0 = row0 // tc
    n_lanes = ab.shape[1]
    state_spec = pl.BlockSpec((tc // seq, hs, HEAD_DIM, HEAD_DIM), lambda i, h: (i, h, 0, 0))
    return pl.pallas_call(
        functools.partial(_gdn_sample_kernel, n_heads=n_heads, seq=seq),
        grid=(rows // tc, n_heads // hs),
        in_specs=[
            *_qkv_specs(tc, lambda i, h: blk0 + i, head, qw, vw, key_dim),
            *_qkv_specs(tc, lambda i, h: i, head, qw, vw, key_dim),
            *_qkv_specs(CONV_W, lambda i, h: 0, head, qw, vw, key_dim),
            pl.BlockSpec((tc, vw), lambda i, h: (blk0 + i, z_col0 // vw + h)),
            pl.BlockSpec((tc, vw), lambda i, h: (blk0 + i, ga_col0 // vw + h)),
            pl.BlockSpec((tc, n_lanes), lambda i, h: (blk0 + i, 0)),
            pl.BlockSpec((n_lanes, tc), lambda i, h: (0, blk0 + i)),
            *_gate_param_specs(n_lanes, lambda i, h: (0, 0)),
            pl.BlockSpec((1, HEAD_DIM), lambda i, h: (0, 0)),
            state_spec,
        ],
        out_specs=[pl.BlockSpec((tc, vw), lambda i, h: (i, h)), state_spec],
        out_shape=[jax.ShapeDtypeStruct((rows, n_heads * HEAD_DIM), F32),
                   jax.ShapeDtypeStruct(s0.shape, F32)],
        compiler_params=_cparams("parallel", "parallel"),
        name="gdn_sample",
    )(*[proj_a] * 3, *[carried] * 3, *[conv_w] * 3, proj_a, proj_b, ab, ab_t, *_gate_params(alog, dtb, n_lanes),
      norm_w.reshape(1, HEAD_DIM), s0)


def _merge(other, gate, o):
    return (other.astype(F32) + jax.nn.sigmoid(gate) * o).astype(BF16)


def _swa_prompt_kernel(sinks_ref, q_ref, kc_ref, kp_ref, vc_ref, vp_ref, ma_ref, gb_ref, o_ref):
    g0 = pl.program_id(0) * SWA_STEP_GROUPS
    i = pl.program_id(1)
    rows = KV_GROUP * WINDOW
    r = lax.broadcasted_iota(jnp.int32, (rows, 2 * WINDOW), 0) % WINDOW
    c = lax.broadcasted_iota(jnp.int32, (rows, 2 * WINDOW), 1)
    visible = ((c < WINDOW) & (c > r) & (i > 0)) | ((c >= WINDOW) & (c - WINDOW <= r))
    head = lax.broadcasted_iota(jnp.int32, (rows, 1), 0) // WINDOW
    scale = HEAD_DIM ** -0.5
    groups = range(SWA_STEP_GROUPS)
    hcols = lambda g, hh: slice((g * KV_GROUP + hh) * HEAD_DIM, (g * KV_GROUP + hh + 1) * HEAD_DIM)
    gcols = lambda g: slice(g * HEAD_DIM, (g + 1) * HEAD_DIM)
    q4 = [jnp.concatenate([q_ref[:, hcols(g, hh)] for hh in range(KV_GROUP)], axis=0) for g in groups]
    kcat = [jnp.concatenate([kp_ref[:, gcols(g)], kc_ref[:, gcols(g)]], axis=0) for g in groups]
    vcat = [jnp.concatenate([vp_ref[:, gcols(g)], vc_ref[:, gcols(g)]], axis=0) for g in groups]
    s = [_dot_nt(q4[g], kcat[g]) for g in groups]
    p, den = [], []
    for g in groups:
        sink = jnp.zeros((rows, 1), F32)
        for hh in range(KV_GROUP):
            sink = jnp.where(head == hh, sinks_ref[(g0 + g) * KV_GROUP + hh], sink)
        sg = jnp.where(visible, s[g] * scale, -jnp.inf)
        m = jnp.maximum(jnp.max(sg, axis=-1, keepdims=True), sink)
        pg = jnp.exp(sg - m)
        p.append(pg)
        den.append(jnp.sum(pg, axis=-1, keepdims=True) + jnp.exp(sink - m))
    o4 = [_dot(p[g], vcat[g]) / den[g] for g in groups]
    for g in groups:
        for hh in range(KV_GROUP):
            cs = hcols(g, hh)
            o_ref[:, cs] = _merge(ma_ref[:, cs], gb_ref[:, cs], o4[g][hh * WINDOW:(hh + 1) * WINDOW])


def _swa_prompt(proj, merged_a, sinks, *, rows, n_kv_heads, k_col0, v_col0, gb_col0):
    sg = SWA_STEP_GROUPS
    qw = sg * KV_GROUP * HEAD_DIM
    kw = sg * HEAD_DIM
    prev = lambda i: jnp.maximum(i - 1, 0)
    return pl.pallas_call(
        _swa_prompt_kernel,
        grid=(n_kv_heads // sg, rows // WINDOW),
        in_specs=[
            pl.BlockSpec(memory_space=pltpu.SMEM),
            pl.BlockSpec((WINDOW, qw), lambda g, i: (i, g)),
            pl.BlockSpec((WINDOW, kw), lambda g, i: (i, k_col0 // kw + g)),
            pl.BlockSpec((WINDOW, kw), lambda g, i: (prev(i), k_col0 // kw + g)),
            pl.BlockSpec((WINDOW, kw), lambda g, i: (i, v_col0 // kw + g)),
            pl.BlockSpec((WINDOW, kw), lambda g, i: (prev(i), v_col0 // kw + g)),
            pl.BlockSpec((WINDOW, qw), lambda g, i: (i, g)),
            pl.BlockSpec((WINDOW, qw), lambda g, i: (i, gb_col0 // qw + g)),
        ],
        out_specs=pl.BlockSpec((WINDOW, qw), lambda g, i: (i, g)),
        out_shape=jax.ShapeDtypeStruct((rows, n_kv_heads * KV_GROUP * HEAD_DIM), BF16),
        compiler_params=_cparams("parallel", "parallel"),
        name="swa_prompt",
    )(sinks, proj, proj, proj, proj, proj, merged_a, proj)


def _swa_sample_kernel(sinks_ref, q_ref, kn_ref, vn_ref, ck_ref, cv_ref, ma_ref, gb_lo_ref, gb_hi_ref,
                       o_ref, ok_ref, ov_ref, *, seq):
    n_b, buf_rows, _ = ck_ref.shape
    kv_cols = kn_ref.shape[1]
    n_groups = kv_cols // HEAD_DIM
    w_buf = buf_rows // n_groups
    half = gb_lo_ref.shape[1]
    rows = KV_GROUP * seq
    t = lax.broadcasted_iota(jnp.int32, (rows, 1), 0) % seq
    head = lax.broadcasted_iota(jnp.int32, (rows, 1), 0) // seq
    jc = lax.broadcasted_iota(jnp.int32, (1, w_buf), 1)
    jn = lax.broadcasted_iota(jnp.int32, (1, seq), 1)
    dist_c = t + w_buf - jc
    cache_ok = (dist_c >= 0) & (dist_c < WINDOW)
    new_ok = jn <= t
    scale = HEAD_DIM ** -0.5
    head_slots = lambda g: pl.ds(g, w_buf, stride=n_groups)
    sinks = []
    for g in range(n_groups):
        sink = jnp.zeros((rows, 1), F32)
        for hh in range(KV_GROUP):
            sink = jnp.where(head == hh, sinks_ref[g * KV_GROUP + hh], sink)
        sinks.append(sink)

    def per_sequence(b, carry):
        r0 = pl.multiple_of(b * seq, seq)
        qb = q_ref[pl.ds(r0, seq), :]
        knb = kn_ref[pl.ds(r0, seq), :]
        vnb = vn_ref[pl.ds(r0, seq), :]
        mab = ma_ref[pl.ds(r0, seq), :]
        gbb = (gb_lo_ref[pl.ds(r0, seq), :], gb_hi_ref[pl.ds(r0, seq), :])
        groups = range(n_groups)
        gcols = lambda g: slice(g * HEAD_DIM, (g + 1) * HEAD_DIM)
        hcols = lambda g, hh: slice((g * KV_GROUP + hh) * HEAD_DIM, (g * KV_GROUP + hh + 1) * HEAD_DIM)
        q4 = [jnp.concatenate([qb[:, hcols(g, hh)] for hh in range(KV_GROUP)], axis=0) for g in groups]
        s_c = [_dot_nt(q4[g], ck_ref[b, head_slots(g), :]) for g in groups]
        s_n = [_dot_nt(q4[g], knb[:, gcols(g)]) for g in groups]
        p_c, p_n, den = [], [], []
        for g in groups:
            sc = jnp.where(cache_ok, s_c[g] * scale, -jnp.inf)
            sn = jnp.where(new_ok, s_n[g] * scale, -jnp.inf)
            m = jnp.maximum(jnp.maximum(jnp.max(sc, axis=-1, keepdims=True),
                                        jnp.max(sn, axis=-1, keepdims=True)), sinks[g])
            pc, pn = jnp.exp(sc - m), jnp.exp(sn - m)
            p_c.append(pc)
            p_n.append(pn)
            den.append(jnp.sum(pc, axis=-1, keepdims=True) + jnp.sum(pn, axis=-1, keepdims=True)
                       + jnp.exp(sinks[g] - m))
        o4 = [(_dot(p_c[g], cv_ref[b, head_slots(g), :]) + _dot(p_n[g], vnb[:, gcols(g)])) / den[g]
              for g in groups]
        for g in groups:
            for hh in range(KV_GROUP):
                cs = hcols(g, hh)
                gate = gbb[cs.start // half][:, cs.start % half:cs.start % half + HEAD_DIM]
                o_ref[pl.ds(r0, seq), cs] = _merge(mab[:, cs], gate, o4[g][hh * seq:(hh + 1) * seq])
        kept = (w_buf - seq) * n_groups
        ok_ref[b, 0:kept, :] = ck_ref[b, buf_rows - kept:buf_rows, :]
        ov_ref[b, 0:kept, :] = cv_ref[b, buf_rows - kept:buf_rows, :]
        for g in groups:
            new_slots = pl.ds(kept + g, seq, stride=n_groups)
            ok_ref[b, new_slots, :] = knb[:, gcols(g)]
            ov_ref[b, new_slots, :] = vnb[:, gcols(g)]
        return carry

    lax.fori_loop(0, n_b, per_sequence, 0)


def _swa_sample(proj, cache_k, cache_v, merged_a, sinks, *, row0, seq, k_col0, v_col0, gb_col0, nb=8):
    n_seq, buf_rows, _ = cache_k.shape
    q_cols = merged_a.shape[1]
    kv_cols = q_cols // KV_GROUP
    half = q_cols // 2
    tr = nb * seq
    blk0 = row0 // tr
    cache_spec = pl.BlockSpec((nb, buf_rows, HEAD_DIM), lambda i: (i, 0, 0))
    return pl.pallas_call(
        functools.partial(_swa_sample_kernel, seq=seq),
        grid=(n_seq // nb,),
        in_specs=[
            pl.BlockSpec(memory_space=pltpu.SMEM),
            pl.BlockSpec((tr, q_cols), lambda i: (blk0 + i, 0)),
            pl.BlockSpec((tr, kv_cols), lambda i: (blk0 + i, k_col0 // kv_cols)),
            pl.BlockSpec((tr, kv_cols), lambda i: (blk0 + i, v_col0 // kv_cols)),
            cache_spec, cache_spec,
            pl.BlockSpec((tr, q_cols), lambda i: (i, 0)),
            pl.BlockSpec((tr, half), lambda i: (blk0 + i, gb_col0 // half)),
            pl.BlockSpec((tr, half), lambda i: (blk0 + i, gb_col0 // half + 1)),
        ],
        out_specs=[pl.BlockSpec((tr, q_cols), lambda i: (i, 0)), cache_spec, cache_spec],
        out_shape=[jax.ShapeDtypeStruct((n_seq * seq, q_cols), BF16),
                   jax.ShapeDtypeStruct(cache_k.shape, F32),
                   jax.ShapeDtypeStruct(cache_v.shape, F32)],
        compiler_params=_cparams("parallel"),
        name="swa_sample",
    )(sinks, proj, proj, proj, cache_k, cache_v, merged_a, proj, proj)


def _outproj_kernel(x_ref, mp_ref, ms_ref, w_ref, o_ref, *, n_prompt_tiles):
    i = pl.program_id(0)

    @pl.when(i < n_prompt_tiles)
    def _():
        o_ref[...] = x_ref[...] + jnp.dot(mp_ref[...], w_ref[...], preferred_element_type=F32)

    @pl.when(i >= n_prompt_tiles)
    def _():
        o_ref[...] = x_ref[...] + jnp.dot(ms_ref[...], w_ref[...], preferred_element_type=F32)


def _outproj(x, merged_p, merged_s, w_out, *, tm=512, tn=1024):
    m, d = x.shape
    npt = merged_p.shape[0] // tm
    return pl.pallas_call(
        functools.partial(_outproj_kernel, n_prompt_tiles=npt),
        grid=(m // tm, d // tn),
        in_specs=[
            pl.BlockSpec((tm, tn), lambda i, j: (i, j)),
            pl.BlockSpec((tm, d), lambda i, j: (jnp.minimum(i, npt - 1), 0)),
            pl.BlockSpec((tm, d), lambda i, j: (jnp.maximum(i - npt, 0), 0)),
            pl.BlockSpec((d, tn), lambda i, j: (0, j)),
        ],
        out_specs=pl.BlockSpec((tm, tn), lambda i, j: (i, j)),
        out_shape=jax.ShapeDtypeStruct((m, d), F32),
        compiler_params=_cparams("parallel", "arbitrary"),
        name="outproj",
    )(x, merged_p, merged_s, w_out)


def _ple_kernel(xf_ref, x_ref, pe_ref, nw_ref, wp_ref, wg_ref, o_head_ref, o_tail_ref, xn_ref, *, n_head_tiles):
    i = pl.program_id(0)

    @pl.when(pl.program_id(1) == 0)
    def _():
        xn_ref[...] = _rms(xf_ref[...], nw_ref[...]).astype(BF16)

    def emit(o_ref):
        xn = xn_ref[...]
        pe = pe_ref[...]
        for c in range(0, o_ref.shape[1], PLE_OUT_CHUNK):
            cs = slice(c, c + PLE_OUT_CHUNK)
            gate = jnp.dot(xn, wg_ref[:, cs], preferred_element_type=F32)
            emb = jnp.dot(pe, wp_ref[:, cs], preferred_element_type=F32)
            o_ref[:, cs] = x_ref[:, cs] + emb * jax.nn.sigmoid(gate)

    @pl.when(i < n_head_tiles)
    def _():
        emit(o_head_ref)

    @pl.when(i >= n_head_tiles)
    def _():
        emit(o_tail_ref)


def _ple(x, pe, norm_w, w_proj, w_gate, *, head_rows, tm=512, tn=1024):
    m, d = x.shape
    pdim = pe.shape[1]
    nh = head_rows // tm
    nj = d // tn
    head_idx = lambda i, j: (jnp.minimum(i, nh - 1), jnp.where(i < nh, j, nj - 1))
    tail_idx = lambda i, j: (jnp.maximum(i - nh, 0), jnp.where(i >= nh, j, 0))
    return pl.pallas_call(
        functools.partial(_ple_kernel, n_head_tiles=nh),
        grid=(m // tm, nj),
        in_specs=[
            pl.BlockSpec((tm, d), lambda i, j: (i, 0)),
            pl.BlockSpec((tm, tn), lambda i, j: (i, j)),
            pl.BlockSpec((tm, pdim), lambda i, j: (i, 0)),
            pl.BlockSpec((1, d), lambda i, j: (0, 0)),
            pl.BlockSpec((pdim, tn), lambda i, j: (0, j)),
            pl.BlockSpec((d, tn), lambda i, j: (0, j)),
        ],
        out_specs=[pl.BlockSpec((tm, tn), head_idx), pl.BlockSpec((tm, tn), tail_idx)],
        out_shape=[jax.ShapeDtypeStruct((head_rows, d), F32), jax.ShapeDtypeStruct((m - head_rows, d), F32)],
        scratch_shapes=[pltpu.VMEM((tm, d), BF16)],
        compiler_params=_cparams("arbitrary", "arbitrary"),
        name="ple",
    )(x, x, pe, norm_w.reshape(1, d), w_proj, w_gate)


def _rope_tables(seq, dec_batch, dec_seq):
    half = HEAD_DIM // 2
    inv_freq = ROPE_THETA ** (-jnp.arange(half, dtype=F32) / half)
    pos_p = jnp.arange(seq, dtype=jnp.int32)
    pos_s = jnp.tile(PAST_LEN + jnp.arange(dec_seq, dtype=jnp.int32), dec_batch)
    ang = jnp.concatenate([pos_p, pos_s]).astype(F32)[:, None] * inv_freq[None, :]
    cos, sin = jnp.cos(ang), jnp.sin(ang)
    return jnp.concatenate([cos, cos], axis=1), jnp.concatenate([-sin, sin], axis=1)


def kernel(x_prompt, x_sample, p_prompt, p_sample, state_gdn, state_conv, cache_swa_k, cache_swa_v, ffn1_norm, ffn1_w_gate, ffn1_w_up, ffn1_w_down, mix_norm, w_in, conv_w, A_log, dt_bias, gdn_norm_w, q_norm_w, k_norm_w, sinks, w_out, ffn2_norm, ffn2_w_gate, ffn2_w_up, ffn2_w_down, ple_norm, w_ple_proj, w_ple_gate):
    depth, dec_batch, n_v_heads, dk, dv = state_gdn.shape
    assert depth == 1 and dk == HEAD_DIM and dv == HEAD_DIM
    bp, seq, d = x_prompt.shape
    assert bp == 1
    dec_seq = x_sample.shape[1]
    w_buf, n_kv_heads = cache_swa_k.shape[2], cache_swa_k.shape[3]
    n_heads = sinks.shape[1]
    assert n_heads == n_kv_heads * KV_GROUP and w_buf == WINDOW and CHUNK % dec_seq == 0
    conv_dim = conv_w.shape[2]
    key_dim = n_v_heads // GDN_REP * HEAD_DIM
    val_dim = n_v_heads * HEAD_DIM
    q_dim = n_heads * HEAD_DIM
    kv_cols = n_kv_heads * HEAD_DIM
    n_samp = dec_batch * dec_seq

    ab0 = conv_dim + val_dim
    b0 = ab0 + 2 * n_v_heads
    assert w_in.shape[2] == b0 + q_dim + 2 * kv_cols + 2 * d
    w_all = w_in[0].astype(BF16)
    w_b = w_all[:, b0:]
    n_ab = 128
    w_ab = jnp.pad(w_all[:, ab0:b0], ((0, 0), (0, n_ab - 2 * n_v_heads)))
    z_col0 = conv_dim
    k_col0, v_col0 = q_dim, q_dim + kv_cols
    ga_col0 = q_dim + 2 * kv_cols
    gb_col0 = ga_col0 + d

    bf = lambda w: w[0].astype(BF16)
    pe = jnp.concatenate([p_prompt.reshape(seq, PLE_DIM), p_sample.reshape(n_samp, PLE_DIM)], axis=0)

    x = _ffn([x_prompt.reshape(seq, d), x_sample.reshape(n_samp, d)], ffn1_norm[0],
             bf(ffn1_w_gate), bf(ffn1_w_up), bf(ffn1_w_down))
    proj_a, ab = _inproj_ab(x, mix_norm[0], w_all, w_ab, n=ab0)
    cos, sin = _rope_tables(seq, dec_batch, dec_seq)
    proj_b = _inproj_rope(x, mix_norm[0], w_b, cos, sin, q_norm_w[0], k_norm_w[0], q_cols=q_dim, k_cols=kv_cols)

    carried = jnp.pad(state_conv[0], ((0, 0), (dec_seq - (CONV_W - 1), 0), (0, 0))).reshape(n_samp, conv_dim)
    ab_t = ab.T
    gdn_args = dict(n_heads=n_v_heads, z_col0=z_col0, ga_col0=ga_col0)
    ma_p, sg_p = _gdn_prompt(proj_a, proj_b, conv_w[0], ab, ab_t, A_log[0], dt_bias[0], gdn_norm_w[0],
                             rows=seq, **gdn_args)
    ma_s, sg_s = _gdn_sample(proj_a, proj_b, carried, conv_w[0], ab, ab_t, A_log[0], dt_bias[0], gdn_norm_w[0],
                             state_gdn[0], row0=seq, rows=n_samp, seq=dec_seq, **gdn_args)

    swa_args = dict(k_col0=k_col0, v_col0=v_col0, gb_col0=gb_col0)
    merged_p = _swa_prompt(proj_b, ma_p, sinks[0], rows=seq, n_kv_heads=n_kv_heads, **swa_args)
    merged_s, kk_s, vv_s = _swa_sample(proj_b, cache_swa_k[0].reshape(dec_batch, w_buf * n_kv_heads, HEAD_DIM),
                                       cache_swa_v[0].reshape(dec_batch, w_buf * n_kv_heads, HEAD_DIM),
                                       ma_s, sinks[0], row0=seq, seq=dec_seq, **swa_args)

    x = _outproj(x, merged_p, merged_s, bf(w_out))
    x = _ffn([x], ffn2_norm[0], bf(ffn2_w_gate), bf(ffn2_w_up), bf(ffn2_w_down))
    y_p, y_s = _ple(x, pe.astype(BF16), ple_norm[0], bf(w_ple_proj), bf(w_ple_gate), head_rows=seq)

    cache_shape = (1, 1, w_buf, n_kv_heads, HEAD_DIM)
    return (
        y_p.reshape(1, seq, d),
        y_s.reshape(dec_batch, dec_seq, d),
        sg_p.reshape(1, 1, n_v_heads, HEAD_DIM, HEAD_DIM),
        proj_a[seq - (CONV_W - 1):seq, :conv_dim].reshape(1, 1, CONV_W - 1, conv_dim),
        proj_b[seq - w_buf:seq, k_col0:k_col0 + kv_cols].reshape(cache_shape),
        proj_b[seq - w_buf:seq, v_col0:v_col0 + kv_cols].reshape(cache_shape),
        sg_s.reshape(1, dec_batch, n_v_heads, HEAD_DIM, HEAD_DIM),
        proj_a[seq:, :conv_dim].reshape(dec_batch, dec_seq, conv_dim)[:, dec_seq - (CONV_W - 1):].reshape(
            1, dec_batch, CONV_W - 1, conv_dim),
        kk_s.reshape(1, dec_batch, w_buf, n_kv_heads, HEAD_DIM),
        vv_s.reshape(1, dec_batch, w_buf, n_kv_heads, HEAD_DIM),
    )
```
